```python
import math
import jax
import jax.numpy as jnp
from jax import lax
import numpy as np


D_MODEL = 2048
BATCH = 4
SEQ = 2048
DEPTH = 2

GRID_W = 64
CTX_LEN = 256

N_BRANCH = 4
MIX_W = D_MODEL // N_BRANCH

NA_HEADS = 4
NA_HEAD_DIM = MIX_W // NA_HEADS
NA_WIN_ROWS = 8
NA_WIN_COLS = 16

GLA_HEADS = 4
GLA_VAL_DIM = MIX_W // GLA_HEADS
GLA_KEY_DIM = GLA_VAL_DIM // 2
GLA_RANK = 16
GLA_TAU = 16.0
GLA_CHUNK = 64

FNET_GROUPS = 4
FNET_GROUP_DIM = MIX_W // FNET_GROUPS

DIFF_HEADS = 4
DIFF_V_DIM = MIX_W // DIFF_HEADS
DIFF_QK_DIM = DIFF_V_DIM // 2
DIFF_Q_BLOCK = 128
ROPE_BASE = 10000.0

N_GROUPS = 4
EXPERTS_PER_GROUP = 8
TOP_K_IN_GROUP = 2
EXPERT_DIM = D_MODEL // 4

DEEPNORM_ALPHA = (2 * DEPTH) ** 0.25
DEEPNORM_BETA = (8 * DEPTH) ** -0.25
LN_EPS = 1e-6
RMS_EPS = 1e-5
NEG_INF = -1e30

GLA_QK_W = GLA_HEADS * GLA_KEY_DIM
DIFF_QK_W = DIFF_HEADS * 2 * DIFF_QK_DIM
KV_SPLITS = (MIX_W, MIX_W, GLA_QK_W, MIX_W, 2 * GLA_RANK, DIFF_QK_W, MIX_W)
Q_SPLITS = (MIX_W, GLA_QK_W, MIX_W, MIX_W, DIFF_QK_W)
GATE_W = N_BRANCH * D_MODEL
ALL_SPLITS = KV_SPLITS + Q_SPLITS + (GATE_W,)
KV_COLS = sum(KV_SPLITS)
IN_COLS = sum(ALL_SPLITS)

kernel_name = 'hybrid_na_gla_fnet_diff_hmoe_dit'


def _split(x, sizes):
    return jnp.split(x, np.cumsum(sizes)[:-1].tolist(), axis=-1)


def _layer_norm(x):
    xf = x.astype(jnp.float32)
    mu = jnp.mean(xf, -1, keepdims=True)
    var = jnp.mean(jnp.square(xf - mu), -1, keepdims=True)
    return ((xf - mu) * lax.rsqrt(var + LN_EPS)).astype(x.dtype)


def _layer_norm_affine(x, g, b):
    return _layer_norm(x) * g + b


def _modulate(x, shift, scale):
    return _layer_norm(x) * (1 + scale) + shift


def _rms_norm(x, gain):
    xf = x.astype(jnp.float32)
    y = xf * lax.rsqrt(jnp.mean(jnp.square(xf), -1, keepdims=True) + RMS_EPS)
    return (y * gain.astype(jnp.float32)).astype(x.dtype)


def _heads(x, n_heads):
    b, n, _ = x.shape
    return x.reshape(b, n, n_heads, -1).transpose(0, 2, 1, 3)


def _merge_heads(x):
    b, h, n, d = x.shape
    return x.transpose(0, 2, 1, 3).reshape(b, n, h * d)


def _diff_heads(x):
    b, n, _ = x.shape
    return x.reshape(b, n, DIFF_HEADS, 2, DIFF_QK_DIM).transpose(0, 2, 3, 1, 4)


def _flip(x):
    return jnp.flip(x, axis=2)


def _axial_rope(n, dtype):
    half = DIFF_QK_DIM // 2
    inv = ROPE_BASE ** (-jnp.arange(0, half, 2, dtype=jnp.float32) / half)
    t = jnp.arange(n)
    ang_r = (t // GRID_W).astype(jnp.float32)[:, None] * inv
    ang_c = (t % GRID_W).astype(jnp.float32)[:, None] * inv
    return (jnp.cos(ang_r).astype(dtype), jnp.sin(ang_r).astype(dtype),
            jnp.cos(ang_c).astype(dtype), jnp.sin(ang_c).astype(dtype))


def _rotate(x, cos, sin):
    x1, x2 = jnp.split(x, 2, axis=-1)
    return jnp.concatenate([x1 * cos - x2 * sin, x1 * sin + x2 * cos], -1)


def _rope_2d(x, cos_r, sin_r, cos_c, sin_c):
    xr, xc = jnp.split(x, 2, axis=-1)
    return jnp.concatenate([_rotate(xr, cos_r, sin_r), _rotate(xc, cos_c, sin_c)], -1)


def _softmax_attn(q, k, v):
    s = jnp.einsum('bhqd,bhkd->bhqk', q, k).astype(jnp.float32) * (q.shape[-1] ** -0.5)
    return jnp.einsum('bhqk,bhkd->bhqd', jax.nn.softmax(s, -1).astype(v.dtype), v)


def _neighbourhood_attn(q, k, v, k_ctx, v_ctx, rpb):
    b, h, n, dh = q.shape
    rows = n // GRID_W
    kr = min(NA_WIN_ROWS, rows)
    kc = NA_WIN_COLS
    ncb = GRID_W // kc
    kcb = 2 * kc
    r = jnp.arange(rows)
    row_idx = jnp.clip(r - kr // 2, 0, rows - kr)[:, None] + jnp.arange(kr)
    j = jnp.arange(ncb)
    band_col = jnp.clip(j * kc - kc // 2, 0, GRID_W - kcb)[:, None] + jnp.arange(kcb)
    q_col = j[:, None] * kc + jnp.arange(kc)
    win_start = jnp.clip(q_col - kc // 2, 0, GRID_W - kc)
    key_col = band_col[:, None, :]
    in_win = (key_col >= win_start[..., None]) & (key_col < win_start[..., None] + kc)
    dr = row_idx - r[:, None]
    dc = key_col - q_col[..., None]
    bias = rpb[:, (dr + NA_WIN_ROWS - 1)[:, None, None, :, None],
               jnp.clip(dc + kc - 1, 0, 2 * kc - 2)[None, :, :, None, :]]
    ri = row_idx[:, None, :, None]
    ci = band_col[None, :, None, :]
    kg = k.reshape(b, h, rows, GRID_W, dh)[:, :, ri, ci]
    vg = v.reshape(b, h, rows, GRID_W, dh)[:, :, ri, ci]
    qb = q.reshape(b, h, rows, ncb, kc, dh)
    scale = dh ** -0.5
    s_lat = jnp.einsum('bhrjqd,bhrjxyd->bhrjqxy', qb, kg).astype(jnp.float32) * scale + bias[None]
    s_lat = jnp.where(in_win[:, :, None, :], s_lat, NEG_INF)
    s_ctx = jnp.einsum('bhrjqd,bhcd->bhrjqc', qb, k_ctx).astype(jnp.float32) * scale
    n_lat = kr * kcb
    p = jax.nn.softmax(jnp.concatenate([s_lat.reshape(b, h, rows, ncb, kc, n_lat), s_ctx], -1), -1)
    p = p.astype(v.dtype)
    p_lat = p[..., :n_lat].reshape(b, h, rows, ncb, kc, kr, kcb)
    o = (jnp.einsum('bhrjqxy,bhrjxyd->bhrjqd', p_lat, vg)
         + jnp.einsum('bhrjqc,bhcd->bhrjqd', p[..., n_lat:], v_ctx))
    return o.reshape(b, h, n, dh)


def _gla_chunked(q, k, v, log_a, s0):
    b, h, n, dk = q.shape
    dv = v.shape[-1]
    nc, c = n // GLA_CHUNK, GLA_CHUNK
    cum = jnp.cumsum(log_a.astype(jnp.float32).reshape(b, h, nc, c, dk), axis=3)
    last = cum[:, :, :, -1:, :]
    kf = k.astype(jnp.float32).reshape(b, h, nc, c, dk)
    vf = v.astype(jnp.float32).reshape(b, h, nc, c, dv)
    qf = q.astype(jnp.float32).reshape(b, h, nc, c, dk) * jnp.exp(cum) * (dk ** -0.5)
    k_in = kf * jnp.exp(-cum)
    k_end = kf * jnp.exp(last - cum)
    causal = jnp.tril(jnp.ones((c, c), dtype=bool))
    a = jnp.where(causal, jnp.einsum('bhnid,bhnjd->bhnij', qf, k_in), 0.0)
    o_intra = jnp.einsum('bhnij,bhnjv->bhniv', a, vf)
    u = jnp.einsum('bhnjd,bhnjv->bhndv', k_end, vf)
    g = jnp.exp(last[:, :, :, 0, :])

    def step(s, inp):
        g_c, u_c = inp
        return g_c[..., None] * s + u_c, s

    s_final, s_prev = lax.scan(step, s0.astype(jnp.float32),
                               (jnp.moveaxis(g, 2, 0), jnp.moveaxis(u, 2, 0)))
    s_prev = jnp.moveaxis(s_prev, 0, 2)
    o = o_intra + jnp.einsum('bhnid,bhndv->bhniv', qf, s_prev)
    return o.reshape(b, h, n, dv).astype(v.dtype), s_final


def _gla_final_state(k, v, log_a):
    cum = jnp.cumsum(log_a.astype(jnp.float32), axis=2)
    k_end = k.astype(jnp.float32) * jnp.exp(cum[:, :, -1:, :] - cum)
    return jnp.einsum('bhtd,bhtv->bhdv', k_end, v.astype(jnp.float32))


def _gla_output(o, r, gain):
    b, h, n, dv = o.shape
    y = _rms_norm(o.transpose(0, 2, 1, 3), gain).reshape(b, n, h * dv)
    return y * jax.nn.silu(r)


def _fourier(u):
    b, n, _ = u.shape
    ug = u.reshape(b, n, FNET_GROUPS, FNET_GROUP_DIM).astype(jnp.float32)
    y = jnp.fft.fft2(ug, axes=(1, 3), norm='ortho').real
    return y.reshape(b, n, FNET_GROUPS * FNET_GROUP_DIM).astype(u.dtype)


def _diff_attn(q, k, v, lam):
    s = jnp.einsum('bhmqd,bhmkd->bhmqk', q, k).astype(jnp.float32) * (DIFF_QK_DIM ** -0.5)
    p = jax.nn.softmax(s, -1)
    w = p[:, :, 0] - lam * p[:, :, 1]
    return jnp.einsum('bhqk,bhkd->bhqd', w.astype(v.dtype), v)


def _diff_attn_blocked(q, k, v, lam):
    b, h, m, n, d = q.shape
    nb = n // DIFF_Q_BLOCK
    qb = jnp.moveaxis(q.reshape(b, h, m, nb, DIFF_Q_BLOCK, d), 3, 0)
    ob = lax.map(lambda qq: _diff_attn(qq, k, v, lam), qb)
    return jnp.moveaxis(ob, 0, 2).reshape(b, h, n, -1)


def _diff_output(o, gain, lam_init):
    b, h, n, dv = o.shape
    return (_rms_norm(o.transpose(0, 2, 1, 3), gain) * (1.0 - lam_init)).reshape(b, n, h * dv)


def _kv_side(na_k, na_v, gla_k, gla_v, gla_lr, diff_k, diff_v, w_gate, b_gate):
    b, n, _ = na_k.shape
    z = jnp.einsum('bnsr,srk->bnsk', gla_lr.reshape(b, n, 2, GLA_RANK), w_gate) + b_gate
    log_a = jax.nn.log_sigmoid(z.astype(jnp.float32)) / GLA_TAU
    return (_heads(na_k, NA_HEADS), _heads(na_v, NA_HEADS),
            _heads(gla_k, GLA_HEADS), _heads(gla_v, GLA_HEADS),
            _heads(log_a[:, :, 0], GLA_HEADS), _heads(log_a[:, :, 1], GLA_HEADS),
            _diff_heads(diff_k), _heads(diff_v, DIFF_HEADS))


def _merge_branches(ys, gates, w_branch_l, w_out_l):
    y = jnp.stack(ys, axis=2)
    z = jnp.einsum('bngi,gid->bngd', y, w_branch_l)
    g = jax.nn.sigmoid(gates.reshape(z.shape))
    return jnp.sum(g * z, axis=2) @ w_out_l


def _context_mix(q_pieces, kv, lam, lam_init, gla_norm_l, diff_norm_l, w_branch_l, w_out_l):
    na_q, gla_q, gla_r, fnet_u, diff_q, gates = q_pieces
    na_k, na_v, gk, gv, la_f, la_b, dk, dv = kv
    b = na_q.shape[0]
    y_na = _merge_heads(_softmax_attn(_heads(na_q, NA_HEADS), na_k, na_v))
    gq = _heads(gla_q, GLA_HEADS)
    zeros = jnp.zeros((b, GLA_HEADS, GLA_KEY_DIM, GLA_VAL_DIM), jnp.float32)
    o_f, s_f = _gla_chunked(gq, gk, gv, la_f, zeros)
    o_b, s_b = _gla_chunked(_flip(gq), _flip(gk), _flip(gv), _flip(la_b), zeros)
    y_gla = _gla_output(o_f + _flip(o_b), gla_r, gla_norm_l)
    y_fnet = _fourier(fnet_u)
    y_diff = _diff_output(_diff_attn(_diff_heads(diff_q), dk, dv, lam), diff_norm_l, lam_init)
    out = _merge_branches((y_na, y_gla, y_fnet, y_diff), gates, w_branch_l, w_out_l)
    return out, s_f, s_b


def _latent_mix(q_pieces, kv, kv_ctx, s_f, s_b, rope, lam, lam_init, rpb_l, gla_norm_l,
                diff_norm_l, w_branch_l, w_out_l):
    na_q, gla_q, gla_r, fnet_u, diff_q, gates = q_pieces
    na_k, na_v, gk, gv, la_f, la_b, dk, dv = kv
    na_kc, na_vc, _, _, _, _, dkc, dvc = kv_ctx
    y_na = _merge_heads(_neighbourhood_attn(_heads(na_q, NA_HEADS), na_k, na_v, na_kc, na_vc, rpb_l))
    gq = _heads(gla_q, GLA_HEADS)
    o_f, _ = _gla_chunked(gq, gk, gv, la_f, s_f)
    o_b, _ = _gla_chunked(_flip(gq), _flip(gk), _flip(gv), _flip(la_b), s_b)
    y_gla = _gla_output(o_f + _flip(o_b), gla_r, gla_norm_l)
    y_fnet = _fourier(fnet_u)
    dq = _rope_2d(_diff_heads(diff_q), *rope)
    k_all = jnp.concatenate([dkc, _rope_2d(dk, *rope)], axis=3)
    v_all = jnp.concatenate([dvc, dv], axis=2)
    y_diff = _diff_output(_diff_attn_blocked(dq, k_all, v_all, lam), diff_norm_l, lam_init)
    return _merge_branches((y_na, y_gla, y_fnet, y_diff), gates, w_branch_l, w_out_l)


def _hier_moe(x, w_group, b_group, w_router, b_router, w_gu, w_down):
    g_logits = (x @ w_group).astype(jnp.float32) + b_group
    g_sel = jnp.argmax(g_logits, -1)
    g_w = jnp.take_along_axis(jax.nn.softmax(g_logits, -1), g_sel[:, None], 1)
    e_logits = jnp.einsum('td,gde->tge', x, w_router).astype(jnp.float32) + b_router
    e_sel = jnp.take_along_axis(e_logits, g_sel[:, None, None], 1)[:, 0]
    top_v, top_i = lax.top_k(e_sel, TOP_K_IN_GROUP)
    top_w = jax.nn.softmax(top_v, -1) * g_w
    within = jnp.sum(jax.nn.one_hot(top_i, EXPERTS_PER_GROUP, dtype=jnp.float32) * top_w[..., None], 1)
    combine = (jax.nn.one_hot(g_sel, N_GROUPS, dtype=jnp.float32)[:, :, None]
               * within[:, None, :]).astype(x.dtype)
    out = jnp.zeros_like(x)
    for g in range(N_GROUPS):
        hg, hu = jnp.split(jnp.einsum('td,edf->tef', x, w_gu[g]), 2, axis=-1)
        out = out + jnp.einsum('tef,efd->td', jax.nn.silu(hg) * hu * combine[:, g, :, None], w_down[g])
    return out


def setup_inputs(seed: int = 0) -> dict:
    key = jax.random.key(seed)
    ks = jax.random.split(key, 25)

    def nrm(k, shape, s):
        return jax.random.normal(k, shape, jnp.float32) * s

    L = DEPTH
    D = D_MODEL
    return {
        'x': nrm(ks[0], (BATCH, SEQ, D), 1.0),
        'c': nrm(ks[1], (BATCH, D), 1.0),
        'ctx': nrm(ks[2], (BATCH, CTX_LEN, D), 1.0),
        'c_ctx': nrm(ks[3], (D,), 1.0),
        'w_mod': nrm(ks[4], (L, D, 6 * D), 0.5 * D ** -0.5),
        'b_mod': nrm(ks[5], (L, 6 * D), 0.02),
        'w_in': nrm(ks[6], (L, D, IN_COLS), D ** -0.5),
        'na_rpb': nrm(ks[7], (L, NA_HEADS, 2 * NA_WIN_ROWS - 1, 2 * NA_WIN_COLS - 1), 0.1),
        'gla_w_gate': nrm(ks[8], (L, 2, GLA_RANK, GLA_QK_W), GLA_RANK ** -0.5),
        'gla_b_gate': nrm(ks[9], (L, 2, GLA_QK_W), 0.1),
        'gla_norm': 1.0 + nrm(ks[10], (L, GLA_VAL_DIM), 0.02),
        'diff_lambda': nrm(ks[11], (L, 4, DIFF_QK_DIM), 0.1),
        'diff_norm': 1.0 + nrm(ks[12], (L, DIFF_V_DIM), 0.02),
        'w_branch': nrm(ks[13], (L, N_BRANCH, MIX_W, D), DEEPNORM_BETA * MIX_W ** -0.5),
        'w_out': nrm(ks[14], (L, D, D), DEEPNORM_BETA * D ** -0.5),
        'ln1_g': 1.0 + nrm(ks[15], (L, D), 0.02),
        'ln1_b': nrm(ks[16], (L, D), 0.02),
        'ln2_g': 1.0 + nrm(ks[17], (L, D), 0.02),
        'ln2_b': nrm(ks[18], (L, D), 0.02),
        'w_group': nrm(ks[19], (L, D, N_GROUPS), D ** -0.5),
        'b_group': nrm(ks[20], (L, N_GROUPS), 0.01),
        'w_router': nrm(ks[21], (L, N_GROUPS, D, EXPERTS_PER_GROUP), D ** -0.5),
        'b_router': nrm(ks[22], (L, N_GROUPS, EXPERTS_PER_GROUP), 0.01),
        'w_gu': nrm(ks[23], (L, N_GROUPS, EXPERTS_PER_GROUP, D, 2 * EXPERT_DIM), DEEPNORM_BETA * D ** -0.5),
        'w_down': nrm(ks[24], (L, N_GROUPS, EXPERTS_PER_GROUP, EXPERT_DIM, D), DEEPNORM_BETA * EXPERT_DIM ** -0.5),
    }


def reference(x, c, ctx, c_ctx, w_mod, b_mod, w_in, na_rpb, gla_w_gate, gla_b_gate, gla_norm,
              diff_lambda, diff_norm, w_branch, w_out, ln1_g, ln1_b, ln2_g, ln2_b,
              w_group, b_group, w_router, b_router, w_gu, w_down):
    rope = _axial_rope(x.shape[1], x.dtype)
    cond_lat = jax.nn.silu(c)
    cond_ctx = jax.nn.silu(c_ctx)
    xl, xc = x, ctx
    for l in range(DEPTH):
        last = l == DEPTH - 1
        lam_init = 0.8 - 0.6 * math.exp(-0.3 * l)
        lq1, lk1, lq2, lk2 = diff_lambda[l].astype(jnp.float32)
        lam = jnp.exp(jnp.sum(lq1 * lk1)) - jnp.exp(jnp.sum(lq2 * lk2)) + lam_init
        mod_lat = jnp.split((cond_lat @ w_mod[l] + b_mod[l])[:, None, :], 6, axis=-1)
        n_ctx_mod = 2 if last else 6
        mod_ctx = jnp.split(cond_ctx @ w_mod[l][:, :n_ctx_mod * D_MODEL] + b_mod[l][:n_ctx_mod * D_MODEL],
                            n_ctx_mod, axis=-1)
        p_lat = _split(_modulate(xl, mod_lat[0], mod_lat[1]) @ w_in[l], ALL_SPLITS)
        h_ctx = _modulate(xc, mod_ctx[0], mod_ctx[1])
        if last:
            p_ctx = _split(h_ctx @ w_in[l][:, :KV_COLS], KV_SPLITS)
        else:
            p_ctx = _split(h_ctx @ w_in[l], ALL_SPLITS)
        kv_lat = _kv_side(*p_lat[:7], gla_w_gate[l], gla_b_gate[l])
        kv_ctx = _kv_side(*p_ctx[:7], gla_w_gate[l], gla_b_gate[l])
        if last:
            _, _, gk_c, gv_c, la_f_c, la_b_c, _, _ = kv_ctx
            s_f = _gla_final_state(gk_c, gv_c, la_f_c)
            s_b = _gla_final_state(_flip(gk_c), _flip(gv_c), _flip(la_b_c))
        else:
            mix_ctx, s_f, s_b = _context_mix(p_ctx[7:], kv_ctx, lam, lam_init, gla_norm[l], diff_norm[l],
                                             w_branch[l], w_out[l])
            xc = _layer_norm_affine(DEEPNORM_ALPHA * xc + mod_ctx[2] * mix_ctx, ln1_g[l], ln1_b[l])
        mix_lat = _latent_mix(p_lat[7:], kv_lat, kv_ctx, s_f, s_b, rope, lam, lam_init, na_rpb[l],
                              gla_norm[l], diff_norm[l], w_branch[l], w_out[l])
        xl = _layer_norm_affine(DEEPNORM_ALPHA * xl + mod_lat[2] * mix_lat, ln1_g[l], ln1_b[l])
        moe_w = (w_group[l], b_group[l], w_router[l], b_router[l], w_gu[l], w_down[l])
        h_lat = _modulate(xl, mod_lat[3], mod_lat[4]).reshape(-1, D_MODEL)
        if last:
            f_lat = _hier_moe(h_lat, *moe_w).reshape(xl.shape)
        else:
            h_c = _modulate(xc, mod_ctx[3], mod_ctx[4]).reshape(-1, D_MODEL)
            f_all = _hier_moe(jnp.concatenate([h_lat, h_c], axis=0), *moe_w)
            n_lat_tok = h_lat.shape[0]
            f_lat = f_all[:n_lat_tok].reshape(xl.shape)
            f_ctx = f_all[n_lat_tok:].reshape(xc.shape)
            xc = _layer_norm_affine(DEEPNORM_ALPHA * xc + mod_ctx[5] * f_ctx, ln2_g[l], ln2_b[l])
        xl = _layer_norm_affine(DEEPNORM_ALPHA * xl + mod_lat[5] * f_lat, ln2_g[l], ln2_b[l])
    return xl
```

```python
import functools
import math

import numpy as np
import jax
import jax.numpy as jnp
from jax import lax
from jax.experimental import pallas as pl
from jax.experimental.pallas import tpu as pltpu

F32 = jnp.float32
BF16 = jnp.bfloat16

D_MODEL = 2048
BATCH = 4
SEQ = 2048
DEPTH = 2
GRID_W = 64
CTX_LEN = 256
N_BRANCH = 4
MIX_W = D_MODEL // N_BRANCH
NA_HEADS = 4
NA_HEAD_DIM = MIX_W // NA_HEADS
NA_WIN_ROWS = 8
NA_WIN_COLS = 16
GLA_HEADS = 4
GLA_VAL_DIM = MIX_W // GLA_HEADS
GLA_KEY_DIM = GLA_VAL_DIM // 2
GLA_RANK = 16
GLA_TAU = 16.0
GLA_CHUNK = 64
FNET_GROUPS = 4
FNET_GROUP_DIM = MIX_W // FNET_GROUPS
DIFF_HEADS = 4
DIFF_V_DIM = MIX_W // DIFF_HEADS
DIFF_QK_DIM = DIFF_V_DIM // 2
ROPE_BASE = 10000.0
N_GROUPS = 4
EXPERTS_PER_GROUP = 8
N_EXPERTS = N_GROUPS * EXPERTS_PER_GROUP
EXPERT_DIM = D_MODEL // 4
DEEPNORM_ALPHA = (2 * DEPTH) ** 0.25
LN_EPS = 1e-6
RMS_EPS = 1e-5
NEG_INF = -1e30

GLA_QK_W = GLA_HEADS * GLA_KEY_DIM
DIFF_QK_W = DIFF_HEADS * 2 * DIFF_QK_DIM
KV_SPLITS = (MIX_W, MIX_W, GLA_QK_W, MIX_W, 2 * GLA_RANK, DIFF_QK_W, MIX_W)
Q_SPLITS = (MIX_W, GLA_QK_W, MIX_W, MIX_W, DIFF_QK_W)
GATE_W = N_BRANCH * D_MODEL
ALL_SPLITS = KV_SPLITS + Q_SPLITS + (GATE_W,)

LANES = 128
SUBLANES = 8
VMEM_LIMIT = 56 * 1024 * 1024

T_LAT = BATCH * SEQ
T_CTX = BATCH * CTX_LEN
T_ALL = T_LAT + T_CTX
ROW_SLABS = D_MODEL // LANES

CB_GATES = 0
CB_NA_K = 64
CB_NA_V = 68
CB_GLA_K = 72
CB_GLA_V = 74
CB_DIFF_K = 78
CB_DIFF_V = 82
CB_NA_Q = 86
CB_GLA_Q = 90
CB_GLA_R = 92
CB_FNET_U = 96
CB_DIFF_Q = 100
P_COLS = 104 * LANES
_PIECE_ORDER = (12, 0, 1, 2, 3, 5, 6, 7, 8, 9, 10, 11)
_LR_PIECE = 4

PROJ_TM = 1024
PROJ_TN = 1024
ROW_TM = 256
EXPERT_TM = 256
NA_QROWS = 8
NA_KROWS = 16
DIFF_QB = 256
CTX_BLK0 = T_LAT // CTX_LEN


def _cparams(sem, vmem=VMEM_LIMIT):
    return pltpu.CompilerParams(dimension_semantics=sem, vmem_limit_bytes=vmem)


def _dot(a, b):
    return jnp.dot(a, b, preferred_element_type=F32)


def _dot_nt(a, b):
    return lax.dot_general(a, b, (((1,), (1,)), ((), ())), preferred_element_type=F32)


def _dot_tn(a, b):
    return lax.dot_general(a, b, (((0,), (0,)), ((), ())), preferred_element_type=F32)


def _sigmoid(x):
    return 1.0 / (1.0 + jnp.exp(-x))


def _ln_rows(x):
    mu = jnp.mean(x, -1, keepdims=True)
    xc = x - mu
    var = jnp.mean(xc * xc, -1, keepdims=True)
    return xc * lax.rsqrt(var + LN_EPS)


def _rms_rows(x, gain):
    return x * lax.rsqrt(jnp.mean(x * x, -1, keepdims=True) + RMS_EPS) * gain


def _mod_row(tile_rows):
    per_batch = SEQ // tile_rows
    return lambda i: jnp.minimum(i // per_batch, BATCH)


def _mod_body(c_ref, w_ref, b_ref, o_ref):
    c = c_ref[...]
    s = (c * _sigmoid(c)).astype(BF16)
    o_ref[...] = _dot(s, w_ref[...].astype(BF16)) + b_ref[...]


def _modulation(c8, w_mod, b_mod):
    depth, d, cols = w_mod.shape
    tn = 1024
    return pl.pallas_call(
        _mod_body,
        grid=(depth, cols // tn),
        in_specs=[pl.BlockSpec((SUBLANES, d), lambda l, j: (0, 0)),
                  pl.BlockSpec((None, d, tn), lambda l, j: (l, 0, j)),
                  pl.BlockSpec((None, 1, tn), lambda l, j: (l, 0, j))],
        out_specs=pl.BlockSpec((None, SUBLANES, tn), lambda l, j: (l, 0, j)),
        out_shape=jax.ShapeDtypeStruct((depth, SUBLANES, cols), F32),
        compiler_params=_cparams(("arbitrary", "arbitrary")),
    )(c8, w_mod, b_mod.reshape(depth, 1, cols))


def _proj_body(x_ref, sh_ref, sc_ref, w_ref, wlr_ref, p_ref, lr_ref, h_scr):
    @pl.when(pl.program_id(1) == 0)
    def _():
        def chunk(c, carry):
            r = pl.ds(pl.multiple_of(c * ROW_TM, ROW_TM), ROW_TM)
            h = _ln_rows(x_ref[r, :]) * (1.0 + sc_ref[...]) + sh_ref[...]
            hb = h.astype(BF16)
            h_scr[r, :] = hb
            lr_ref[r, :] = _dot(hb, wlr_ref[...])
            return carry
        lax.fori_loop(0, PROJ_TM // ROW_TM, chunk, 0)

    p_ref[...] = _dot(h_scr[...], w_ref[...]).astype(BF16)


def _projection(x_all, mod, layer, w_main, w_lr):
    rows = x_all.shape[0]
    brow = _mod_row(PROJ_TM)
    mrow = lambda i: layer * SUBLANES + brow(i)
    return pl.pallas_call(
        _proj_body,
        grid=(rows // PROJ_TM, P_COLS // PROJ_TN),
        in_specs=[pl.BlockSpec((PROJ_TM, D_MODEL), lambda i, j: (i, 0)),
                  pl.BlockSpec((None, 1, D_MODEL), lambda i, j: (mrow(i), 0, 0)),
                  pl.BlockSpec((None, 1, D_MODEL), lambda i, j: (mrow(i), 0, 1)),
                  pl.BlockSpec((D_MODEL, PROJ_TN), lambda i, j: (0, j)),
                  pl.BlockSpec((D_MODEL, LANES), lambda i, j: (0, 0))],
        out_specs=[pl.BlockSpec((PROJ_TM, PROJ_TN), lambda i, j: (i, j)),
                   pl.BlockSpec((PROJ_TM, LANES), lambda i, j: (i, 0))],
        out_shape=[jax.ShapeDtypeStruct((rows, P_COLS), BF16),
                   jax.ShapeDtypeStruct((rows, LANES), F32)],
        scratch_shapes=[pltpu.VMEM((PROJ_TM, D_MODEL), BF16)],
        compiler_params=_cparams(("arbitrary", "arbitrary")),
    )(x_all, mod, mod, w_main, w_lr)


def _na_key_row_start(rb):
    return jnp.clip(rb * NA_QROWS - NA_WIN_ROWS // 2, 0, SEQ // GRID_W - NA_KROWS)


def _softmax_pv(s_list, v_list):
    m = s_list[0].max(-1, keepdims=True)
    for s in s_list[1:]:
        m = jnp.maximum(m, s.max(-1, keepdims=True))
    den = None
    o = None
    for s, v in zip(s_list, v_list):
        p = jnp.exp(s - m)
        ps = p.sum(-1, keepdims=True)
        den = ps if den is None else den + ps
        pv = _dot(p.astype(BF16), v)
        o = pv if o is None else o + pv
    return o / den


def _na_lat_body(q_ref, k_ref, v_ref, kc_ref, vc_ref, tile_ref, o_ref, bias_ref):
    rb = pl.program_id(1)
    krows = NA_KROWS * GRID_W
    grid_rows = SEQ // GRID_W

    @pl.when(pl.program_id(2) == 0)
    def _():
        left = lax.broadcasted_iota(jnp.int32, (GRID_W, LANES), 1) < GRID_W
        k0 = _na_key_row_start(rb)
        for qr in range(NA_QROWS):
            r = rb * NA_QROWS + qr
            r0 = jnp.clip(r - NA_WIN_ROWS // 2, 0, grid_rows - NA_WIN_ROWS)
            for kp in range(NA_KROWS // 2):
                idx = []
                for kk in range(2):
                    kr = k0 + 2 * kp + kk
                    inside = (kr >= r0) & (kr < r0 + NA_WIN_ROWS)
                    idx.append(jnp.where(inside, kr - r + NA_WIN_ROWS - 1, 2 * NA_WIN_ROWS - 1))
                bias_ref[qr * GRID_W:(qr + 1) * GRID_W, kp * LANES:(kp + 1) * LANES] = jnp.where(
                    left, tile_ref[idx[0]], tile_ref[idx[1]])

    start = pl.multiple_of(_na_key_row_start(rb) * GRID_W, 4 * GRID_W)
    k = k_ref[pl.ds(start, krows), :]
    v = v_ref[pl.ds(start, krows), :]
    kc = kc_ref[...]
    vc = vc_ref[...]
    scale = NA_HEAD_DIM ** -0.5
    half = NA_QROWS * GRID_W // 2
    for part in range(2):
        rows = slice(part * half, (part + 1) * half)
        q = q_ref[rows, :]
        s_lat = _dot_nt(q, k) * scale + bias_ref[rows, :]
        s_ctx = _dot_nt(q, kc) * scale
        o_ref[rows, :] = _softmax_pv([s_lat, s_ctx], [v, vc]).astype(BF16)


def _na_ctx_body(q_ref, kc_ref, vc_ref, o_ref):
    s = _dot_nt(q_ref[...], kc_ref[...]) * (NA_HEAD_DIM ** -0.5)
    o_ref[...] = _softmax_pv([s], [vc_ref[...]]).astype(BF16)


def _na_bias_tiles(rpb):
    n_dr = 2 * NA_WIN_ROWS - 1
    sel_c = np.zeros((GRID_W, GRID_W, 2 * NA_WIN_COLS - 1), np.float32)
    ok_c = np.zeros((GRID_W, GRID_W), bool)
    for qc in range(GRID_W):
        w0 = int(np.clip(qc - NA_WIN_COLS // 2, 0, GRID_W - NA_WIN_COLS))
        for kc in range(GRID_W):
            if w0 <= kc < w0 + NA_WIN_COLS:
                ok_c[qc, kc] = True
                sel_c[qc, kc, int(np.clip(kc - qc + NA_WIN_COLS - 1, 0, 2 * NA_WIN_COLS - 2))] = 1.0
    t = jnp.einsum('xyc,hac->haxy', jnp.asarray(sel_c), rpb.astype(F32), precision=lax.Precision.HIGHEST)
    t = jnp.where(jnp.asarray(ok_c)[None, None], t, NEG_INF)
    t = jnp.concatenate([t, jnp.full((NA_HEADS, 1, GRID_W, GRID_W), NEG_INF, F32)], axis=1)
    assert t.shape[1] == n_dr + 1
    return jnp.concatenate([t, t], axis=-1)


def _na_lat(p, tiles):
    qrows = NA_QROWS * GRID_W
    n_rb = SEQ // qrows
    return pl.pallas_call(
        _na_lat_body,
        grid=(NA_HEADS, n_rb, BATCH),
        in_specs=[pl.BlockSpec((qrows, LANES), lambda h, rb, b: (b * n_rb + rb, CB_NA_Q + h)),
                  pl.BlockSpec((SEQ, LANES), lambda h, rb, b: (b, CB_NA_K + h)),
                  pl.BlockSpec((SEQ, LANES), lambda h, rb, b: (b, CB_NA_V + h)),
                  pl.BlockSpec((CTX_LEN, LANES), lambda h, rb, b: (CTX_BLK0 + b, CB_NA_K + h)),
                  pl.BlockSpec((CTX_LEN, LANES), lambda h, rb, b: (CTX_BLK0 + b, CB_NA_V + h)),
                  pl.BlockSpec((None, 2 * NA_WIN_ROWS, GRID_W, LANES), lambda h, rb, b: (h, 0, 0, 0))],
        out_specs=pl.BlockSpec((qrows, LANES), lambda h, rb, b: (b * n_rb + rb, h)),
        out_shape=jax.ShapeDtypeStruct((T_LAT, MIX_W), BF16),
        scratch_shapes=[pltpu.VMEM((qrows, NA_KROWS * GRID_W), F32)],
        compiler_params=_cparams(("arbitrary",) * 3),
    )(p, p, p, p, p, tiles)


def _na_ctx(p):
    return pl.pallas_call(
        _na_ctx_body,
        grid=(BATCH, NA_HEADS),
        in_specs=[pl.BlockSpec((CTX_LEN, LANES), lambda b, h: (CTX_BLK0 + b, CB_NA_Q + h)),
                  pl.BlockSpec((CTX_LEN, LANES), lambda b, h: (CTX_BLK0 + b, CB_NA_K + h)),
                  pl.BlockSpec((CTX_LEN, LANES), lambda b, h: (CTX_BLK0 + b, CB_NA_V + h))],
        out_specs=pl.BlockSpec((CTX_LEN, LANES), lambda b, h: (b, h)),
        out_shape=jax.ShapeDtypeStruct((T_CTX, MIX_W), BF16),
        compiler_params=_cparams(("arbitrary",) * 2),
    )(p, p, p)


def _log_sigmoid(z):
    return jnp.minimum(z, 0.0) - jnp.log1p(jnp.exp(-jnp.abs(z)))


def _gla_body(q_ref, k_ref, v_ref, r_ref, lr_ref, wgf_ref, wgb_ref, bgf_ref, bgb_ref, gain_ref, s0_ref,
              y_ref, sfin_ref, la_f, la_b, o_scr, st_scr, *, n):
    c = GLA_CHUNK
    nc = n // c
    lr = lr_ref[...].astype(BF16)
    la_f[...] = _log_sigmoid(_dot(lr, wgf_ref[...]) + bgf_ref[...]) / GLA_TAU
    la_b[...] = _log_sigmoid(_dot(lr, wgb_ref[...]) + bgb_ref[...]) / GLA_TAU
    o_scr[...] = jnp.zeros_like(o_scr)
    st_scr[...] = s0_ref[...]

    lane = lax.broadcasted_iota(jnp.int32, (c, LANES), 1)
    head_mask = (lane < GLA_KEY_DIM, lane >= GLA_KEY_DIM)
    ri = lax.broadcasted_iota(jnp.int32, (c, c), 0)
    ci = lax.broadcasted_iota(jnp.int32, (c, c), 1)
    tri_mask = (ri >= ci, ci >= ri)
    tri_mat = tuple(jnp.where(m, 1.0, 0.0).astype(BF16) for m in tri_mask)
    scale = GLA_KEY_DIM ** -0.5

    def step(ch, carry):
        for d in range(2):
            cc = ch if d == 0 else nc - 1 - ch
            r = pl.ds(pl.multiple_of(cc * c, c), c)
            la = (la_f if d == 0 else la_b)[r, :]
            la_hi = la.astype(BF16)
            la_lo = (la - la_hi.astype(F32)).astype(BF16)
            cum = _dot(tri_mat[d], la_hi) + _dot(tri_mat[d], la_lo)
            tot = cum[c - 1:c, :] if d == 0 else cum[0:1, :]
            q = q_ref[r, :].astype(F32)
            k = k_ref[r, :].astype(F32)
            qf = q * jnp.exp(cum) * scale
            k_in = (k * jnp.exp(-cum)).astype(BF16)
            k_end = k * jnp.exp(tot - cum)
            g = jnp.exp(tot)
            for hh in range(2):
                cols = slice(hh * GLA_VAL_DIM, (hh + 1) * GLA_VAL_DIM)
                qh = jnp.where(head_mask[hh], qf, 0.0).astype(BF16)
                a = jnp.where(tri_mask[d], _dot_nt(qh, k_in), 0.0)
                vh = v_ref[r, cols]
                st = st_scr[hh * 2 + d]
                o = _dot(a.astype(BF16), vh) + _dot_nt(qh, st.astype(BF16))
                o_scr[r, cols] = o_scr[r, cols] + o
                kh = jnp.where(head_mask[hh], k_end, 0.0).astype(BF16)
                st_scr[hh * 2 + d] = g * st + _dot_tn(vh, kh)
        return carry

    lax.fori_loop(0, nc, step, 0)
    sfin_ref[...] = st_scr[...]
    for hh in range(2):
        cols = slice(hh * GLA_VAL_DIM, (hh + 1) * GLA_VAL_DIM)
        rr = r_ref[:, cols].astype(F32)
        y_ref[:, cols] = (_rms_rows(o_scr[:, cols], gain_ref[...]) * (rr * _sigmoid(rr))).astype(BF16)


def _gla(p, lr, wg_f, wg_b, bg_f, bg_b, gain, s0, *, ctx):
    n = CTX_LEN if ctx else SEQ
    blk0 = CTX_BLK0 if ctx else 0
    pairs = GLA_HEADS // 2
    st_spec = pl.BlockSpec((None, None, 4, GLA_VAL_DIM, LANES), lambda b, pr: (b, pr, 0, 0, 0))
    return pl.pallas_call(
        functools.partial(_gla_body, n=n),
        grid=(BATCH, pairs),
        in_specs=[pl.BlockSpec((n, LANES), lambda b, pr: (blk0 + b, CB_GLA_Q + pr)),
                  pl.BlockSpec((n, LANES), lambda b, pr: (blk0 + b, CB_GLA_K + pr)),
                  pl.BlockSpec((n, 2 * LANES), lambda b, pr: (blk0 + b, CB_GLA_V // 2 + pr)),
                  pl.BlockSpec((n, 2 * LANES), lambda b, pr: (blk0 + b, CB_GLA_R // 2 + pr)),
                  pl.BlockSpec((n, LANES), lambda b, pr: (blk0 + b, 0)),
                  pl.BlockSpec((LANES, LANES), lambda b, pr: (0, pr)),
                  pl.BlockSpec((LANES, LANES), lambda b, pr: (0, pr)),
                  pl.BlockSpec((1, LANES), lambda b, pr: (0, pr)),
                  pl.BlockSpec((1, LANES), lambda b, pr: (0, pr)),
                  pl.BlockSpec((1, GLA_VAL_DIM), lambda b, pr: (0, 0)),
                  st_spec],
        out_specs=[pl.BlockSpec((n, 2 * LANES), lambda b, pr: (b, pr)), st_spec],
        out_shape=[jax.ShapeDtypeStruct((BATCH * n, MIX_W), BF16),
                   jax.ShapeDtypeStruct((BATCH, pairs, 4, GLA_VAL_DIM, LANES), F32)],
        scratch_shapes=[pltpu.VMEM((n, LANES), F32), pltpu.VMEM((n, LANES), F32),
                        pltpu.VMEM((n, 2 * LANES), F32), pltpu.VMEM((4, GLA_VAL_DIM, LANES), F32)],
        compiler_params=_cparams(("arbitrary",) * 2),
    )(p, p, p, p, lr, wg_f, wg_b, bg_f, bg_b, gain, s0)


def _gla_gate_weights(w_gate, b_gate):
    wf = jnp.zeros((LANES, GLA_QK_W), F32).at[:GLA_RANK].set(w_gate[0])
    wb = jnp.zeros((LANES, GLA_QK_W), F32).at[GLA_RANK:2 * GLA_RANK].set(w_gate[1])
    return wf.astype(BF16), wb.astype(BF16), b_gate[0:1], b_gate[1:2]


def _dft_tables(n):
    def cs(m):
        idx = (np.arange(m)[:, None] * np.arange(m)[None, :]) % m
        ang = 2.0 * np.pi * idx / m
        return np.cos(ang) / np.sqrt(m), np.sin(ang) / np.sqrt(m)
    cn, sn = cs(n)
    cg, sg = cs(FNET_GROUP_DIM)
    as_bf16 = lambda a: jnp.asarray(a, F32).astype(BF16)
    return as_bf16(cn), as_bf16(sn), as_bf16(np.concatenate([cg, sg], axis=1))


def _fnet_body(u_ref, cn_ref, sn_ref, csg_ref, o_ref):
    t = _dot(u_ref[...], csg_ref[...]).astype(BF16)
    y = _dot(cn_ref[...], t[:, :FNET_GROUP_DIM]) - _dot(sn_ref[...], t[:, FNET_GROUP_DIM:])
    o_ref[...] = y.astype(BF16)


def _fnet(p, *, ctx):
    n = CTX_LEN if ctx else SEQ
    blk0 = CTX_BLK0 if ctx else 0
    cn, sn, csg = _dft_tables(n)
    return pl.pallas_call(
        _fnet_body,
        grid=(BATCH, FNET_GROUPS),
        in_specs=[pl.BlockSpec((n, LANES), lambda b, g: (blk0 + b, CB_FNET_U + g)),
                  pl.BlockSpec((n, n), lambda b, g: (0, 0)),
                  pl.BlockSpec((n, n), lambda b, g: (0, 0)),
                  pl.BlockSpec((FNET_GROUP_DIM, 2 * FNET_GROUP_DIM), lambda b, g: (0, 0))],
        out_specs=pl.BlockSpec((n, LANES), lambda b, g: (b, g)),
        out_shape=jax.ShapeDtypeStruct((BATCH * n, MIX_W), BF16),
        compiler_params=_cparams(("arbitrary",) * 2),
    )(p, cn, sn, csg)


def _rope_tables():
    half = DIFF_QK_DIM // 2
    inv = ROPE_BASE ** (-jnp.arange(0, half, 2, dtype=F32) / half)
    t = jnp.arange(SEQ)
    ang_r = (t // GRID_W).astype(F32)[:, None] * inv
    ang_c = (t % GRID_W).astype(F32)[:, None] * inv
    cos = jnp.concatenate([jnp.cos(ang_r)] * 2 + [jnp.cos(ang_c)] * 2, axis=1)
    sin = jnp.concatenate([jnp.sin(ang_r)] * 2 + [jnp.sin(ang_c)] * 2, axis=1)
    rot = np.zeros((LANES, LANES), np.float32)
    q4 = half // 2
    for base in range(0, LANES, half):
        for j in range(q4):
            rot[base + j + q4, base + j] = -1.0
            rot[base + j, base + j + q4] = 1.0
    return jnp.tile(cos, (1, 2)), jnp.tile(sin, (1, 2)), jnp.asarray(rot, BF16)


def _rope(x, cos, sin, rot):
    return x.astype(F32) * cos + _dot(x, rot) * sin


def _diff_out(q, k_list, v_list, lam, gain, out_scale):
    q = q * (DIFF_QK_DIM ** -0.5)
    lane = lax.broadcasted_iota(jnp.int32, q.shape, 1)
    o = None
    for sub, coef in ((lane < DIFF_QK_DIM, None), (lane >= DIFF_QK_DIM, lam)):
        qs = jnp.where(sub, q, 0.0).astype(BF16)
        os = _softmax_pv([_dot_nt(qs, k) for k in k_list], v_list)
        o = os if coef is None else o - coef * os
    return (_rms_rows(o, gain) * out_scale).astype(BF16)


def _diff_lat_body(q_ref, k_ref, v_ref, kc_ref, vc_ref, cosq_ref, sinq_ref, cosk_ref, sink_ref, rot_ref,
                   lam_ref, gain_ref, o_ref, k_scr, *, out_scale):
    @pl.when(pl.program_id(2) == 0)
    def _():
        k_scr[...] = _rope(k_ref[...], cosk_ref[...], sink_ref[...], rot_ref[...]).astype(BF16)

    q = _rope(q_ref[...], cosq_ref[...], sinq_ref[...], rot_ref[...])
    o_ref[...] = _diff_out(q, [kc_ref[...], k_scr[...]], [vc_ref[...], v_ref[...]], lam_ref[:, 0:1],
                           gain_ref[...], out_scale)


def _diff_ctx_body(q_ref, kc_ref, vc_ref, lam_ref, gain_ref, o_ref, *, out_scale):
    o_ref[...] = _diff_out(q_ref[...].astype(F32), [kc_ref[...]], [vc_ref[...]], lam_ref[:, 0:1],
                           gain_ref[...], out_scale)


def _diff_lat(p, rope, lam_row, gain, lam_init):
    cos, sin, rot = rope
    nqb = SEQ // DIFF_QB
    const = lambda b, h, qb: (0, 0)
    return pl.pallas_call(
        functools.partial(_diff_lat_body, out_scale=1.0 - lam_init),
        grid=(BATCH, DIFF_HEADS, nqb),
        in_specs=[pl.BlockSpec((DIFF_QB, LANES), lambda b, h, qb: (b * nqb + qb, CB_DIFF_Q + h)),
                  pl.BlockSpec((SEQ, LANES), lambda b, h, qb: (b, CB_DIFF_K + h)),
                  pl.BlockSpec((SEQ, LANES), lambda b, h, qb: (b, CB_DIFF_V + h)),
                  pl.BlockSpec((CTX_LEN, LANES), lambda b, h, qb: (CTX_BLK0 + b, CB_DIFF_K + h)),
                  pl.BlockSpec((CTX_LEN, LANES), lambda b, h, qb: (CTX_BLK0 + b, CB_DIFF_V + h)),
                  pl.BlockSpec((DIFF_QB, LANES), lambda b, h, qb: (qb, 0)),
                  pl.BlockSpec((DIFF_QB, LANES), lambda b, h, qb: (qb, 0)),
                  pl.BlockSpec((SEQ, LANES), const),
                  pl.BlockSpec((SEQ, LANES), const),
                  pl.BlockSpec((LANES, LANES), const),
                  pl.BlockSpec((1, LANES), const),
                  pl.BlockSpec((1, DIFF_V_DIM), const)],
        out_specs=pl.BlockSpec((DIFF_QB, LANES), lambda b, h, qb: (b * nqb + qb, h)),
        out_shape=jax.ShapeDtypeStruct((T_LAT, MIX_W), BF16),
        scratch_shapes=[pltpu.VMEM((SEQ, LANES), BF16)],
        compiler_params=_cparams(("arbitrary",) * 3),
    )(p, p, p, p, p, cos, sin, cos, sin, rot, lam_row, gain)


def _diff_ctx(p, lam_row, gain, lam_init):
    const = lambda b, h: (0, 0)
    return pl.pallas_call(
        functools.partial(_diff_ctx_body, out_scale=1.0 - lam_init),
        grid=(BATCH, DIFF_HEADS),
        in_specs=[pl.BlockSpec((CTX_LEN, LANES), lambda b, h: (CTX_BLK0 + b, CB_DIFF_Q + h)),
                  pl.BlockSpec((CTX_LEN, LANES), lambda b, h: (CTX_BLK0 + b, CB_DIFF_K + h)),
                  pl.BlockSpec((CTX_LEN, LANES), lambda b, h: (CTX_BLK0 + b, CB_DIFF_V + h)),
                  pl.BlockSpec((1, LANES), const),
                  pl.BlockSpec((1, DIFF_V_DIM), const)],
        out_specs=pl.BlockSpec((CTX_LEN, LANES), lambda b, h: (b, h)),
        out_shape=jax.ShapeDtypeStruct((T_CTX, MIX_W), BF16),
        compiler_params=_cparams(("arbitrary",) * 2),
    )(p, p, p, lam_row, gain)


N_LAT_TILES = T_LAT // ROW_TM


def _merge_body(*refs, with_ctx):
    gates = refs[0:N_BRANCH]
    y_lat = refs[N_BRANCH:2 * N_BRANCH]
    y_ctx = refs[2 * N_BRANCH:3 * N_BRANCH] if with_ctx else None
    wb_ref, m_ref = refs[-2], refs[-1]
    is_lat = pl.program_id(0) < N_LAT_TILES
    acc = None
    for i in range(N_BRANCH):
        y = y_lat[i][...]
        if with_ctx:
            y = jnp.where(is_lat, y, y_ctx[i][...])
        t = _sigmoid(gates[i][...].astype(F32)) * _dot(y, wb_ref[i])
        acc = t if acc is None else acc + t
    m_ref[...] = acc.astype(BF16)


def _merge(p, ys_lat, ys_ctx, w_branch):
    with_ctx = ys_ctx is not None
    t_rows = T_ALL if with_ctx else T_LAT
    gate_specs = [pl.BlockSpec((ROW_TM, D_MODEL), functools.partial(lambda i, g: (i, g), g=g))
                  for g in range(N_BRANCH)]
    y_specs = [pl.BlockSpec((ROW_TM, MIX_W), lambda i: (jnp.minimum(i, N_LAT_TILES - 1), 0))
               for _ in range(N_BRANCH)]
    ys = list(ys_lat)
    if with_ctx:
        y_specs += [pl.BlockSpec((ROW_TM, MIX_W), lambda i: (jnp.maximum(i - N_LAT_TILES, 0), 0))
                    for _ in range(N_BRANCH)]
        ys += list(ys_ctx)
    return pl.pallas_call(
        functools.partial(_merge_body, with_ctx=with_ctx),
        grid=(t_rows // ROW_TM,),
        in_specs=gate_specs + y_specs + [pl.BlockSpec((N_BRANCH, MIX_W, D_MODEL), lambda i: (0, 0, 0))],
        out_specs=pl.BlockSpec((ROW_TM, D_MODEL), lambda i: (i, 0)),
        out_shape=jax.ShapeDtypeStruct((t_rows, D_MODEL), BF16),
        compiler_params=_cparams(("arbitrary",)),
    )(p, p, p, p, *ys, w_branch)


def _residual_ln(x, gate, f, g, b):
    return _ln_rows(DEEPNORM_ALPHA * x + gate * f) * g + b


def _outproj_body(m_ref, x_ref, gate_ref, w_ref, g_ref, b_ref, o_ref):
    f = _dot(m_ref[...], w_ref[...])
    o_ref[...] = _residual_ln(x_ref[...], gate_ref[...], f, g_ref[...], b_ref[...])


def _outproj(m, x_all, mod, layer, w_out, ln_g, ln_b, t_rows):
    brow = _mod_row(ROW_TM)
    vec = pl.BlockSpec((1, D_MODEL), lambda i: (0, 0))
    return pl.pallas_call(
        _outproj_body,
        grid=(t_rows // ROW_TM,),
        in_specs=[pl.BlockSpec((ROW_TM, D_MODEL), lambda i: (i, 0)),
                  pl.BlockSpec((ROW_TM, D_MODEL), lambda i: (i, 0)),
                  pl.BlockSpec((None, 1, D_MODEL), lambda i: (layer * SUBLANES + brow(i), 0, 2)),
                  pl.BlockSpec((D_MODEL, D_MODEL), lambda i: (0, 0)),
                  vec, vec],
        out_specs=pl.BlockSpec((ROW_TM, D_MODEL), lambda i: (i, 0)),
        out_shape=jax.ShapeDtypeStruct((t_rows, D_MODEL), F32),
        compiler_params=_cparams(("arbitrary",)),
    )(m, x_all, mod, w_out, ln_g, ln_b)


RT_E1, RT_E2, RT_W1, RT_W2, RT_RANK1, RT_RANK2 = range(6)


def _first_lane_of_max(vals, lane_f):
    m = vals.max(-1, keepdims=True)
    idx = jnp.where(vals == m, lane_f, float(LANES)).min(-1, keepdims=True)
    return m, idx


def _router_body(x_ref, sh_ref, sc_ref, wr_ref, br_ref, h_ref, route_ref, cnt_ref, run_scr):
    @pl.when(pl.program_id(0) == 0)
    def _():
        run_scr[...] = jnp.zeros_like(run_scr)

    tm = ROW_TM
    h = _ln_rows(x_ref[...]) * (1.0 + sc_ref[...]) + sh_ref[...]
    hb = h.astype(BF16)
    for s in range(ROW_SLABS):
        h_ref[pl.ds(s, tm, stride=ROW_SLABS), :] = hb[:, s * LANES:(s + 1) * LANES].astype(F32)
    logits = _dot(hb, wr_ref[...]) + br_ref[...]
    lane = lax.broadcasted_iota(jnp.int32, (tm, LANES), 1)
    lane_f = lane.astype(F32)
    is_group = lane < N_GROUPS
    g_logits = jnp.where(is_group, logits, NEG_INF)
    g_max, g_sel = _first_lane_of_max(g_logits, lane_f)
    g_w = 1.0 / jnp.where(is_group, jnp.exp(logits - g_max), 0.0).sum(-1, keepdims=True)
    lo = N_GROUPS + EXPERTS_PER_GROUP * g_sel
    e_logits = jnp.where((lane_f >= lo) & (lane_f < lo + EXPERTS_PER_GROUP), logits, NEG_INF)
    v1, i1 = _first_lane_of_max(e_logits, lane_f)
    v2, i2 = _first_lane_of_max(jnp.where(lane_f == i1, NEG_INF, e_logits), lane_f)
    t = jnp.exp(v2 - v1)
    w1 = g_w / (1.0 + t)
    w2 = g_w * t / (1.0 + t)
    e1 = i1 - N_GROUPS
    e2 = i2 - N_GROUPS

    oh1 = lane_f == e1
    oh2 = lane_f == e2
    both = jnp.where(oh1 | oh2, 1.0, 0.0)
    ri = lax.broadcasted_iota(jnp.int32, (tm, tm), 0)
    ci = lax.broadcasted_iota(jnp.int32, (tm, tm), 1)
    before = jnp.where(ci < ri, 1.0, 0.0).astype(BF16)
    excl = _dot(before, both.astype(BF16)) + run_scr[0:1, :]
    rank1 = jnp.where(oh1, excl, 0.0).sum(-1, keepdims=True)
    rank2 = jnp.where(oh2, excl, 0.0).sum(-1, keepdims=True)
    run_scr[...] = run_scr[...] + both.sum(0, keepdims=True)
    cnt_ref[...] = run_scr[...]

    rec = jnp.zeros((tm, LANES), F32)
    for idx, val in ((RT_E1, e1), (RT_E2, e2), (RT_W1, w1), (RT_W2, w2), (RT_RANK1, rank1), (RT_RANK2, rank2)):
        rec = jnp.where(lane == idx, val, rec)
    route_ref[...] = rec


def _router(x1, mod, layer, w_r, b_r, t_rows):
    brow = _mod_row(ROW_TM)
    mrow = lambda i: layer * SUBLANES + brow(i)
    return pl.pallas_call(
        _router_body,
        grid=(t_rows // ROW_TM,),
        in_specs=[pl.BlockSpec((ROW_TM, D_MODEL), lambda i: (i, 0)),
                  pl.BlockSpec((None, 1, D_MODEL), lambda i: (mrow(i), 0, 3)),
                  pl.BlockSpec((None, 1, D_MODEL), lambda i: (mrow(i), 0, 4)),
                  pl.BlockSpec((D_MODEL, LANES), lambda i: (0, 0)),
                  pl.BlockSpec((1, LANES), lambda i: (0, 0))],
        out_specs=[pl.BlockSpec((ROW_TM * ROW_SLABS, LANES), lambda i: (i, 0)),
                   pl.BlockSpec((ROW_TM, LANES), lambda i: (i, 0)),
                   pl.BlockSpec((SUBLANES, LANES), lambda i: (0, 0))],
        out_shape=[jax.ShapeDtypeStruct((t_rows * ROW_SLABS, LANES), F32),
                   jax.ShapeDtypeStruct((t_rows, LANES), F32),
                   jax.ShapeDtypeStruct((SUBLANES, LANES), F32)],
        scratch_shapes=[pltpu.VMEM((SUBLANES, LANES), F32)],
        compiler_params=_cparams(("arbitrary",)),
    )(x1, mod, mod, w_r, b_r)


def _router_weights(w_group, b_group, w_router, b_router):
    wr = jnp.concatenate([w_group, jnp.transpose(w_router, (1, 0, 2)).reshape(D_MODEL, N_EXPERTS)], axis=1)
    br = jnp.concatenate([b_group, b_router.reshape(N_EXPERTS)])
    pad = LANES - wr.shape[1]
    return jnp.pad(wr, ((0, 0), (0, pad))).astype(BF16), jnp.pad(br, (0, pad)).reshape(1, LANES)


def _token_rows(ref, tok):
    return ref.at[pl.ds(pl.multiple_of(tok * ROW_SLABS, ROW_SLABS), ROW_SLABS), :]


DMA_UNROLL = 8


def _dispatch_body(pos_ref, h_ref, xs_in_ref, xs_ref, sem):
    del xs_in_ref
    tt = ROW_TM
    base = pl.program_id(0) * tt

    def issue(j, carry):
        for k in range(2):
            pltpu.make_async_copy(_token_rows(h_ref, j), _token_rows(xs_ref, pos_ref[2 * (base + j) + k]),
                                  sem).start()
        return carry

    def drain(j, carry):
        pltpu.make_async_copy(_token_rows(h_ref, 0), _token_rows(xs_ref, 0), sem).wait()
        return carry

    lax.fori_loop(0, tt, issue, 0, unroll=DMA_UNROLL)
    lax.fori_loop(0, 2 * tt, drain, 0, unroll=DMA_UNROLL)


def _dispatch(pos, h, n_slots):
    t_rows = h.shape[0] // ROW_SLABS
    xs0 = jnp.zeros((n_slots * ROW_SLABS, LANES), F32)
    return pl.pallas_call(
        _dispatch_body,
        grid_spec=pltpu.PrefetchScalarGridSpec(
            num_scalar_prefetch=1,
            grid=(t_rows // ROW_TM,),
            in_specs=[pl.BlockSpec((ROW_TM * ROW_SLABS, LANES), lambda i, pos: (i, 0)),
                      pl.BlockSpec(memory_space=pl.ANY)],
            out_specs=pl.BlockSpec(memory_space=pl.ANY),
            scratch_shapes=[pltpu.SemaphoreType.DMA(())]),
        out_shape=jax.ShapeDtypeStruct(xs0.shape, F32),
        input_output_aliases={2: 0},
        compiler_params=_cparams(("arbitrary",)),
    )(pos, h, xs0)


def _expert_body(te_ref, nu_ref, xs_ref, wgu_ref, wd_ref, ys_ref, wgu_b, wd_b):
    i = pl.program_id(0)
    tm = EXPERT_TM
    used = i < nu_ref[0]

    @pl.when(used)
    def _():
        @pl.when((i == 0) | (te_ref[i] != te_ref[jnp.maximum(i - 1, 0)]))
        def _():
            def cast(c, carry):
                r = pl.ds(pl.multiple_of(c * ROW_TM, ROW_TM), ROW_TM)
                wgu_b[r, :] = wgu_ref[r, :].astype(BF16)
                return carry
            lax.fori_loop(0, D_MODEL // ROW_TM, cast, 0)
            wd_b[...] = wd_ref[...].astype(BF16)

        x = jnp.concatenate([xs_ref[pl.ds(s, tm, stride=ROW_SLABS), :].astype(BF16) for s in range(ROW_SLABS)],
                            axis=1)
        hgu = _dot(x, wgu_b[...])
        hg = hgu[:, :EXPERT_DIM]
        hu = hgu[:, EXPERT_DIM:]
        act = (hg * _sigmoid(hg) * hu).astype(BF16)
        y = _dot(act, wd_b[...])
        for s in range(ROW_SLABS):
            ys_ref[pl.ds(s, tm, stride=ROW_SLABS), :] = y[:, s * LANES:(s + 1) * LANES]

    @pl.when(jnp.logical_not(used))
    def _():
        ys_ref[...] = jnp.zeros_like(ys_ref)


def _experts(tile_expert, n_used, xs, w_gu, w_down, layer):
    n_tiles = xs.shape[0] // (EXPERT_TM * ROW_SLABS)
    blk = EXPERT_TM * ROW_SLABS
    wsel = lambda i, te, nu: (layer * N_EXPERTS + te[i], 0, 0)
    return pl.pallas_call(
        _expert_body,
        grid_spec=pltpu.PrefetchScalarGridSpec(
            num_scalar_prefetch=2,
            grid=(n_tiles,),
            in_specs=[pl.BlockSpec((blk, LANES), lambda i, te, nu: (jnp.minimum(i, nu[0] - 1), 0)),
                      pl.BlockSpec((None, D_MODEL, 2 * EXPERT_DIM), wsel),
                      pl.BlockSpec((None, EXPERT_DIM, D_MODEL), wsel)],
            out_specs=pl.BlockSpec((blk, LANES), lambda i, te, nu: (i, 0)),
            scratch_shapes=[pltpu.VMEM((D_MODEL, 2 * EXPERT_DIM), BF16),
                            pltpu.VMEM((EXPERT_DIM, D_MODEL), BF16)]),
        out_shape=jax.ShapeDtypeStruct(xs.shape, F32),
        compiler_params=_cparams(("arbitrary",)),
    )(tile_expert, n_used, xs, w_gu, w_down)


def _final_body(pos_ref, x_ref, ys_ref, route_ref, gate_ref, g_ref, b_ref, out_ref, buf, sem):
    tm = ROW_TM
    base = pl.program_id(0) * 2 * tm

    def issue(j, carry):
        pltpu.make_async_copy(_token_rows(ys_ref, pos_ref[base + j]), _token_rows(buf, j), sem).start()
        return carry

    def drain(j, carry):
        pltpu.make_async_copy(_token_rows(ys_ref, 0), _token_rows(buf, 0), sem).wait()
        return carry

    lax.fori_loop(0, 2 * tm, issue, 0, unroll=DMA_UNROLL)
    lax.fori_loop(0, 2 * tm, drain, 0, unroll=DMA_UNROLL)

    rt = route_ref[...]
    w1 = rt[:, RT_W1:RT_W1 + 1]
    w2 = rt[:, RT_W2:RT_W2 + 1]
    f = jnp.concatenate(
        [w1 * buf[pl.ds(s, tm, stride=2 * ROW_SLABS), :]
         + w2 * buf[pl.ds(ROW_SLABS + s, tm, stride=2 * ROW_SLABS), :] for s in range(ROW_SLABS)], axis=1)
    out_ref[...] = _residual_ln(x_ref[...], gate_ref[...], f, g_ref[...], b_ref[...])


def _final(pos, x1, ys, route, mod, layer, ln_g, ln_b, t_rows):
    brow = _mod_row(ROW_TM)
    vec = pl.BlockSpec((1, D_MODEL), lambda i, pos: (0, 0))
    return pl.pallas_call(
        _final_body,
        grid_spec=pltpu.PrefetchScalarGridSpec(
            num_scalar_prefetch=1,
            grid=(t_rows // ROW_TM,),
            in_specs=[pl.BlockSpec((ROW_TM, D_MODEL), lambda i, pos: (i, 0)),
                      pl.BlockSpec(memory_space=pl.ANY),
                      pl.BlockSpec((ROW_TM, LANES), lambda i, pos: (i, 0)),
                      pl.BlockSpec((None, 1, D_MODEL), lambda i, pos: (layer * SUBLANES + brow(i), 0, 5)),
                      vec, vec],
            out_specs=pl.BlockSpec((ROW_TM, D_MODEL), lambda i, pos: (i, 0)),
            scratch_shapes=[pltpu.VMEM((2 * ROW_TM * ROW_SLABS, LANES), F32),
                            pltpu.SemaphoreType.DMA(())]),
        out_shape=jax.ShapeDtypeStruct((t_rows, D_MODEL), F32),
        compiler_params=_cparams(("arbitrary",)),
    )(pos, x1, ys, route, mod, ln_g, ln_b)


def _moe(x1, mod, layer, router_w, w_gu, w_down, ln_g, ln_b, t_rows):
    h, route, cnt = _router(x1, mod, layer, *router_w, t_rows)
    n_tiles = -(-(2 * t_rows + N_EXPERTS * (EXPERT_TM - 1)) // EXPERT_TM)
    counts = cnt[0, :N_EXPERTS].astype(jnp.int32)
    tiles_per = (counts + EXPERT_TM - 1) // EXPERT_TM
    tile_end = jnp.cumsum(tiles_per)
    tile_start = tile_end - tiles_per
    n_used = tile_end[-1:]
    e = route[:, RT_E1:RT_E2 + 1].astype(jnp.int32)
    rank = route[:, RT_RANK1:RT_RANK2 + 1].astype(jnp.int32)
    pos = (tile_start[e] * EXPERT_TM + rank).reshape(-1)
    ti = jnp.minimum(jnp.arange(n_tiles, dtype=jnp.int32), n_used[0] - 1)
    tile_expert = jnp.sum((ti[:, None] >= tile_end[None, :]).astype(jnp.int32), axis=1)
    xs = _dispatch(pos, h, n_tiles * EXPERT_TM)
    ys = _experts(tile_expert, n_used, xs, w_gu, w_down, layer)
    return _final(pos, x1, ys, route, mod, layer, ln_g, ln_b, t_rows)


def _prep_w_in(w):
    offs = np.concatenate([[0], np.cumsum(ALL_SPLITS)])
    piece = lambda i: w[:, int(offs[i]):int(offs[i + 1])]
    w_main = jnp.concatenate([piece(i) for i in _PIECE_ORDER], axis=1).astype(BF16)
    w_lr = jnp.pad(piece(_LR_PIECE), ((0, 0), (0, LANES - 2 * GLA_RANK))).astype(BF16)
    return w_main, w_lr


def kernel(x, c, ctx, c_ctx, w_mod, b_mod, w_in, na_rpb, gla_w_gate, gla_b_gate, gla_norm, diff_lambda, diff_norm,
           w_branch, w_out, ln1_g, ln1_b, ln2_g, ln2_b, w_group, b_group, w_router, b_router, w_gu, w_down):
    x_all = jnp.concatenate([x.reshape(T_LAT, D_MODEL), ctx.reshape(T_CTX, D_MODEL)], axis=0)
    c8 = jnp.concatenate([c, c_ctx[None], jnp.zeros((SUBLANES - BATCH - 1, D_MODEL), F32)], axis=0)
    mod = _modulation(c8, w_mod, b_mod).reshape(DEPTH * SUBLANES, 1, 6 * D_MODEL)
    rope = _rope_tables()
    w_gu_flat = w_gu.reshape(DEPTH * N_EXPERTS, D_MODEL, 2 * EXPERT_DIM)
    w_down_flat = w_down.reshape(DEPTH * N_EXPERTS, EXPERT_DIM, D_MODEL)
    s_zero = jnp.zeros((BATCH, GLA_HEADS // 2, 4, GLA_VAL_DIM, LANES), F32)

    for l in range(DEPTH):
        last = l == DEPTH - 1
        t_rows = T_LAT if last else T_ALL
        lam_init = 0.8 - 0.6 * math.exp(-0.3 * l)
        lq1, lk1, lq2, lk2 = diff_lambda[l].astype(F32)
        lam = jnp.exp(jnp.sum(lq1 * lk1)) - jnp.exp(jnp.sum(lq2 * lk2)) + lam_init
        lam_row = jnp.full((1, LANES), lam, F32)

        w_main, w_lr = _prep_w_in(w_in[l])
        p, lr = _projection(x_all, mod, l, w_main, w_lr)

        gla_w = _gla_gate_weights(gla_w_gate[l], gla_b_gate[l])
        gla_gain = gla_norm[l].reshape(1, GLA_VAL_DIM)
        diff_gain = diff_norm[l].reshape(1, DIFF_V_DIM)
        y_gla_ctx, states = _gla(p, lr, *gla_w, gla_gain, s_zero, ctx=True)
        y_gla, _ = _gla(p, lr, *gla_w, gla_gain, states, ctx=False)
        ys_lat = (_na_lat(p, _na_bias_tiles(na_rpb[l])), y_gla, _fnet(p, ctx=False),
                  _diff_lat(p, rope, lam_row, diff_gain, lam_init))
        ys_ctx = None
        if not last:
            ys_ctx = (_na_ctx(p), y_gla_ctx, _fnet(p, ctx=True), _diff_ctx(p, lam_row, diff_gain, lam_init))

        m = _merge(p, ys_lat, ys_ctx, w_branch[l].astype(BF16))
        x1 = _outproj(m, x_all, mod, l, w_out[l].astype(BF16), ln1_g[l].reshape(1, -1), ln1_b[l].reshape(1, -1),
                      t_rows)
        router_w = _router_weights(w_group[l], b_group[l], w_router[l], b_router[l])
        x_all = _moe(x1, mod, l, router_w, w_gu_flat, w_down_flat, ln2_g[l].reshape(1, -1),
                     ln2_b[l].reshape(1, -1), t_rows)
    return x_all.reshape(BATCH, SEQ, D_MODEL)
```

```python
import functools
import math

import numpy as np
import jax
import jax.numpy as jnp
from jax import lax
from jax.experimental import pallas as pl
from jax.experimental.pallas import tpu as pltpu

F32 = jnp.float32
BF16 = jnp.bfloat16

D_MODEL = 2048
BATCH = 4
SEQ = 2048
DEPTH = 2
GRID_W = 64
CTX_LEN = 256
N_BRANCH = 4
MIX_W = D_MODEL // N_BRANCH
NA_HEADS = 4
NA_HEAD_DIM = MIX_W // NA_HEADS
NA_WIN_ROWS = 8
NA_WIN_COLS = 16
GLA_HEADS = 4
GLA_VAL_DIM = MIX_W // GLA_HEADS
GLA_KEY_DIM = GLA_VAL_DIM // 2
GLA_RANK = 16
GLA_TAU = 16.0
GLA_CHUNK = 64
FNET_GROUPS = 4
FNET_GROUP_DIM = MIX_W // FNET_GROUPS
DIFF_HEADS = 4
DIFF_V_DIM = MIX_W // DIFF_HEADS
DIFF_QK_DIM = DIFF_V_DIM // 2
ROPE_BASE = 10000.0
N_GROUPS = 4
EXPERTS_PER_GROUP = 8
N_EXPERTS = N_GROUPS * EXPERTS_PER_GROUP
EXPERT_DIM = D_MODEL // 4
DEEPNORM_ALPHA = (2 * DEPTH) ** 0.25
LN_EPS = 1e-6
RMS_EPS = 1e-5
NEG_INF = -1e30

GLA_QK_W = GLA_HEADS * GLA_KEY_DIM
DIFF_QK_W = DIFF_HEADS * 2 * DIFF_QK_DIM
KV_SPLITS = (MIX_W, MIX_W, GLA_QK_W, MIX_W, 2 * GLA_RANK, DIFF_QK_W, MIX_W)
Q_SPLITS = (MIX_W, GLA_QK_W, MIX_W, MIX_W, DIFF_QK_W)
GATE_W = N_BRANCH * D_MODEL
ALL_SPLITS = KV_SPLITS + Q_SPLITS + (GATE_W,)

LANES = 128
SUBLANES = 8
VMEM_LIMIT = 56 * 1024 * 1024

T_LAT = BATCH * SEQ
T_CTX = BATCH * CTX_LEN
T_ALL = T_LAT + T_CTX
ROW_SLABS = D_MODEL // LANES

CB_GATES = 0
CB_NA_K = 64
CB_NA_V = 68
CB_GLA_K = 72
CB_GLA_V = 74
CB_DIFF_K = 78
CB_DIFF_V = 82
CB_NA_Q = 86
CB_GLA_Q = 90
CB_GLA_R = 92
CB_FNET_U = 96
CB_DIFF_Q = 100
P_COLS = 104 * LANES
_PIECE_ORDER = (12, 0, 1, 2, 3, 5, 6, 7, 8, 9, 10, 11)
_LR_PIECE = 4

PROJ_TM = 1024
PROJ_TN = 1024
ROW_TM = 256
EXPERT_TM = 256
NA_QROWS = 8
NA_KROWS = 16
DIFF_QB = 512
DIFF_SUB = 256
CTX_BLK0 = T_LAT // CTX_LEN


def _cparams(sem, vmem=VMEM_LIMIT):
    return pltpu.CompilerParams(dimension_semantics=sem, vmem_limit_bytes=vmem)


def _dot(a, b):
    return jnp.dot(a, b, preferred_element_type=F32)


def _dot_nt(a, b):
    return lax.dot_general(a, b, (((1,), (1,)), ((), ())), preferred_element_type=F32)


def _dot_tn(a, b):
    return lax.dot_general(a, b, (((0,), (0,)), ((), ())), preferred_element_type=F32)


def _sigmoid(x):
    return 1.0 / (1.0 + jnp.exp(-x))


def _ln_rows(x):
    mu = jnp.mean(x, -1, keepdims=True)
    xc = x - mu
    var = jnp.mean(xc * xc, -1, keepdims=True)
    return xc * lax.rsqrt(var + LN_EPS)


def _rms_rows(x, gain):
    return x * lax.rsqrt(jnp.mean(x * x, -1, keepdims=True) + RMS_EPS) * gain


def _mod_row(tile_rows):
    per_batch = SEQ // tile_rows
    return lambda i: jnp.minimum(i // per_batch, BATCH)


def _mod_body(c_ref, w_ref, b_ref, o_ref):
    c = c_ref[...]
    s = (c * _sigmoid(c)).astype(BF16)
    o_ref[...] = _dot(s, w_ref[...].astype(BF16)) + b_ref[...]


def _modulation(c8, w_mod, b_mod):
    depth, d, cols = w_mod.shape
    tn = 1024
    return pl.pallas_call(
        _mod_body,
        grid=(depth, cols // tn),
        in_specs=[pl.BlockSpec((SUBLANES, d), lambda l, j: (0, 0)),
                  pl.BlockSpec((None, d, tn), lambda l, j: (l, 0, j)),
                  pl.BlockSpec((None, 1, tn), lambda l, j: (l, 0, j))],
        out_specs=pl.BlockSpec((None, SUBLANES, tn), lambda l, j: (l, 0, j)),
        out_shape=jax.ShapeDtypeStruct((depth, SUBLANES, cols), F32),
        compiler_params=_cparams(("arbitrary", "arbitrary")),
    )(c8, w_mod, b_mod.reshape(depth, 1, cols))


PROJ_LAT_TILES = T_LAT // PROJ_TM


def _proj_body(*refs, split_input, skip_ctx_gates):
    if split_input:
        x_ref, xc_ref, sh_ref, sc_ref, w_ref, wlr_ref, p_ref, lr_ref, h_scr = refs
    else:
        x_ref, sh_ref, sc_ref, w_ref, wlr_ref, p_ref, lr_ref, h_scr = refs
    i = pl.program_id(0)
    j = pl.program_id(1)

    @pl.when(j == 0)
    def _():
        def chunk(c, carry):
            r = pl.ds(pl.multiple_of(c * ROW_TM, ROW_TM), ROW_TM)
            xv = x_ref[r, :]
            if split_input:
                xv = jnp.where(i < PROJ_LAT_TILES, xv, xc_ref[r, :])
            h = _ln_rows(xv) * (1.0 + sc_ref[...]) + sh_ref[...]
            hb = h.astype(BF16)
            h_scr[r, :] = hb
            lr_ref[r, :] = _dot(hb, wlr_ref[...])
            return carry
        lax.fori_loop(0, PROJ_TM // ROW_TM, chunk, 0)

    def project():
        p_ref[...] = _dot(h_scr[...], w_ref[...]).astype(BF16)

    if skip_ctx_gates:
        unused = (i >= PROJ_LAT_TILES) & (j < GATE_W // PROJ_TN)
        pl.when(jnp.logical_not(unused))(project)

        @pl.when(unused)
        def _():
            p_ref[...] = jnp.zeros_like(p_ref)
    else:
        project()


def _projection(x_lat, x_ctx, mod, layer, w_main, w_lr, *, skip_ctx_gates):
    split_input = x_ctx is not None
    brow = _mod_row(PROJ_TM)
    mrow = lambda i: layer * SUBLANES + brow(i)
    x_specs = [pl.BlockSpec((PROJ_TM, D_MODEL), lambda i, j: (jnp.minimum(i, PROJ_LAT_TILES - 1), 0))
               if split_input else pl.BlockSpec((PROJ_TM, D_MODEL), lambda i, j: (i, 0))]
    xs = [x_lat]
    if split_input:
        assert x_ctx.shape[0] == PROJ_TM
        x_specs.append(pl.BlockSpec((PROJ_TM, D_MODEL), lambda i, j: (0, 0), pipeline_mode=pl.Buffered(1)))
        xs.append(x_ctx)
    return pl.pallas_call(
        functools.partial(_proj_body, split_input=split_input, skip_ctx_gates=skip_ctx_gates),
        grid=(T_ALL // PROJ_TM, P_COLS // PROJ_TN),
        in_specs=x_specs + [pl.BlockSpec((None, 1, D_MODEL), lambda i, j: (mrow(i), 0, 0)),
                            pl.BlockSpec((None, 1, D_MODEL), lambda i, j: (mrow(i), 0, 1)),
                            pl.BlockSpec((D_MODEL, PROJ_TN), lambda i, j: (0, j)),
                            pl.BlockSpec((D_MODEL, LANES), lambda i, j: (0, 0))],
        out_specs=[pl.BlockSpec((PROJ_TM, PROJ_TN), lambda i, j: (i, j)),
                   pl.BlockSpec((PROJ_TM, LANES), lambda i, j: (i, 0))],
        out_shape=[jax.ShapeDtypeStruct((T_ALL, P_COLS), BF16),
                   jax.ShapeDtypeStruct((T_ALL, LANES), F32)],
        scratch_shapes=[pltpu.VMEM((PROJ_TM, D_MODEL), BF16)],
        compiler_params=_cparams(("arbitrary", "arbitrary")),
    )(*xs, mod, mod, w_main, w_lr)


def _na_key_row_start(rb):
    return jnp.clip(rb * NA_QROWS - NA_WIN_ROWS // 2, 0, SEQ // GRID_W - NA_KROWS)


def _softmax_pv(s_list, v_list):
    m = s_list[0].max(-1, keepdims=True)
    for s in s_list[1:]:
        m = jnp.maximum(m, s.max(-1, keepdims=True))
    den = None
    o = None
    for s, v in zip(s_list, v_list):
        p = jnp.exp(s - m)
        ps = p.sum(-1, keepdims=True)
        den = ps if den is None else den + ps
        pv = _dot(p.astype(BF16), v)
        o = pv if o is None else o + pv
    return o / den


def _na_lat_body(q_ref, k_ref, v_ref, kc_ref, vc_ref, tile_ref, o_ref, bias_ref):
    rb = pl.program_id(1)
    krows = NA_KROWS * GRID_W
    grid_rows = SEQ // GRID_W

    @pl.when(pl.program_id(2) == 0)
    def _():
        left = lax.broadcasted_iota(jnp.int32, (GRID_W, LANES), 1) < GRID_W
        k0 = _na_key_row_start(rb)
        for qr in range(NA_QROWS):
            r = rb * NA_QROWS + qr
            r0 = jnp.clip(r - NA_WIN_ROWS // 2, 0, grid_rows - NA_WIN_ROWS)
            for kp in range(NA_KROWS // 2):
                idx = []
                for kk in range(2):
                    kr = k0 + 2 * kp + kk
                    inside = (kr >= r0) & (kr < r0 + NA_WIN_ROWS)
                    idx.append(jnp.where(inside, kr - r + NA_WIN_ROWS - 1, 2 * NA_WIN_ROWS - 1))
                bias_ref[qr * GRID_W:(qr + 1) * GRID_W, kp * LANES:(kp + 1) * LANES] = jnp.where(
                    left, tile_ref[idx[0]], tile_ref[idx[1]])

    start = pl.multiple_of(_na_key_row_start(rb) * GRID_W, 4 * GRID_W)
    k = k_ref[pl.ds(start, krows), :]
    v = v_ref[pl.ds(start, krows), :]
    kc = kc_ref[...]
    vc = vc_ref[...]
    scale = NA_HEAD_DIM ** -0.5
    half = NA_QROWS * GRID_W // 2
    for part in range(2):
        rows = slice(part * half, (part + 1) * half)
        q = q_ref[rows, :]
        s_lat = _dot_nt(q, k) * scale + bias_ref[rows, :]
        s_ctx = _dot_nt(q, kc) * scale
        o_ref[rows, :] = _softmax_pv([s_lat, s_ctx], [v, vc]).astype(BF16)


def _na_ctx_body(q_ref, kc_ref, vc_ref, o_ref):
    s = _dot_nt(q_ref[...], kc_ref[...]) * (NA_HEAD_DIM ** -0.5)
    o_ref[...] = _softmax_pv([s], [vc_ref[...]]).astype(BF16)


def _na_bias_tiles(rpb):
    n_dr = 2 * NA_WIN_ROWS - 1
    sel_c = np.zeros((GRID_W, GRID_W, 2 * NA_WIN_COLS - 1), np.float32)
    ok_c = np.zeros((GRID_W, GRID_W), bool)
    for qc in range(GRID_W):
        w0 = int(np.clip(qc - NA_WIN_COLS // 2, 0, GRID_W - NA_WIN_COLS))
        for kc in range(GRID_W):
            if w0 <= kc < w0 + NA_WIN_COLS:
                ok_c[qc, kc] = True
                sel_c[qc, kc, int(np.clip(kc - qc + NA_WIN_COLS - 1, 0, 2 * NA_WIN_COLS - 2))] = 1.0
    t = jnp.einsum('xyc,hac->haxy', jnp.asarray(sel_c), rpb.astype(F32), precision=lax.Precision.HIGHEST)
    t = jnp.where(jnp.asarray(ok_c)[None, None], t, NEG_INF)
    t = jnp.concatenate([t, jnp.full((NA_HEADS, 1, GRID_W, GRID_W), NEG_INF, F32)], axis=1)
    assert t.shape[1] == n_dr + 1
    return jnp.concatenate([t, t], axis=-1)


def _na_lat(p, tiles):
    qrows = NA_QROWS * GRID_W
    n_rb = SEQ // qrows
    return pl.pallas_call(
        _na_lat_body,
        grid=(NA_HEADS, n_rb, BATCH),
        in_specs=[pl.BlockSpec((qrows, LANES), lambda h, rb, b: (b * n_rb + rb, CB_NA_Q + h)),
                  pl.BlockSpec((SEQ, LANES), lambda h, rb, b: (b, CB_NA_K + h)),
                  pl.BlockSpec((SEQ, LANES), lambda h, rb, b: (b, CB_NA_V + h)),
                  pl.BlockSpec((CTX_LEN, LANES), lambda h, rb, b: (CTX_BLK0 + b, CB_NA_K + h)),
                  pl.BlockSpec((CTX_LEN, LANES), lambda h, rb, b: (CTX_BLK0 + b, CB_NA_V + h)),
                  pl.BlockSpec((None, 2 * NA_WIN_ROWS, GRID_W, LANES), lambda h, rb, b: (h, 0, 0, 0))],
        out_specs=pl.BlockSpec((qrows, LANES), lambda h, rb, b: (b * n_rb + rb, h)),
        out_shape=jax.ShapeDtypeStruct((T_LAT, MIX_W), BF16),
        scratch_shapes=[pltpu.VMEM((qrows, NA_KROWS * GRID_W), F32)],
        compiler_params=_cparams(("arbitrary",) * 3),
    )(p, p, p, p, p, tiles)


def _na_ctx(p):
    return pl.pallas_call(
        _na_ctx_body,
        grid=(BATCH, NA_HEADS),
        in_specs=[pl.BlockSpec((CTX_LEN, LANES), lambda b, h: (CTX_BLK0 + b, CB_NA_Q + h)),
                  pl.BlockSpec((CTX_LEN, LANES), lambda b, h: (CTX_BLK0 + b, CB_NA_K + h)),
                  pl.BlockSpec((CTX_LEN, LANES), lambda b, h: (CTX_BLK0 + b, CB_NA_V + h))],
        out_specs=pl.BlockSpec((CTX_LEN, LANES), lambda b, h: (b, h)),
        out_shape=jax.ShapeDtypeStruct((T_CTX, MIX_W), BF16),
        compiler_params=_cparams(("arbitrary",) * 2),
    )(p, p, p)


def _log_sigmoid(z):
    return jnp.minimum(z, 0.0) - jnp.log1p(jnp.exp(-jnp.abs(z)))


def _gla_body(q_ref, k_ref, v_ref, r_ref, lr_ref, wgf_ref, wgb_ref, bgf_ref, bgb_ref, gain_ref, s0_ref,
              y_ref, sfin_ref, la_f, la_b, o_scr, st_scr, *, n):
    c = GLA_CHUNK
    nc = n // c
    lr = lr_ref[...].astype(BF16)
    la_f[...] = _log_sigmoid(_dot(lr, wgf_ref[...]) + bgf_ref[...]) / GLA_TAU
    la_b[...] = _log_sigmoid(_dot(lr, wgb_ref[...]) + bgb_ref[...]) / GLA_TAU
    o_scr[...] = jnp.zeros_like(o_scr)
    st_scr[...] = s0_ref[...]

    lane = lax.broadcasted_iota(jnp.int32, (c, LANES), 1)
    head_mask = (lane < GLA_KEY_DIM, lane >= GLA_KEY_DIM)
    ri = lax.broadcasted_iota(jnp.int32, (c, c), 0)
    ci = lax.broadcasted_iota(jnp.int32, (c, c), 1)
    tri_mask = (ri >= ci, ci >= ri)
    tri_mat = tuple(jnp.where(m, 1.0, 0.0).astype(BF16) for m in tri_mask)
    scale = GLA_KEY_DIM ** -0.5

    def step(ch, carry):
        for d in range(2):
            cc = ch if d == 0 else nc - 1 - ch
            r = pl.ds(pl.multiple_of(cc * c, c), c)
            la = (la_f if d == 0 else la_b)[r, :]
            la_hi = la.astype(BF16)
            la_lo = (la - la_hi.astype(F32)).astype(BF16)
            cum = _dot(tri_mat[d], la_hi) + _dot(tri_mat[d], la_lo)
            tot = cum[c - 1:c, :] if d == 0 else cum[0:1, :]
            q = q_ref[r, :].astype(F32)
            k = k_ref[r, :].astype(F32)
            qf = q * jnp.exp(cum) * scale
            k_in = (k * jnp.exp(-cum)).astype(BF16)
            k_end = k * jnp.exp(tot - cum)
            g = jnp.exp(tot)
            for hh in range(2):
                cols = slice(hh * GLA_VAL_DIM, (hh + 1) * GLA_VAL_DIM)
                qh = jnp.where(head_mask[hh], qf, 0.0).astype(BF16)
                a = jnp.where(tri_mask[d], _dot_nt(qh, k_in), 0.0)
                vh = v_ref[r, cols]
                st = st_scr[hh * 2 + d]
                o = _dot(a.astype(BF16), vh) + _dot_nt(qh, st.astype(BF16))
                o_scr[r, cols] = o_scr[r, cols] + o
                kh = jnp.where(head_mask[hh], k_end, 0.0).astype(BF16)
                st_scr[hh * 2 + d] = g * st + _dot_tn(vh, kh)
        return carry

    lax.fori_loop(0, nc, step, 0, unroll=2)
    sfin_ref[...] = st_scr[...]
    for hh in range(2):
        cols = slice(hh * GLA_VAL_DIM, (hh + 1) * GLA_VAL_DIM)
        rr = r_ref[:, cols].astype(F32)
        y_ref[:, cols] = (_rms_rows(o_scr[:, cols], gain_ref[...]) * (rr * _sigmoid(rr))).astype(BF16)


def _gla(p, lr, wg_f, wg_b, bg_f, bg_b, gain, s0, *, ctx):
    n = CTX_LEN if ctx else SEQ
    blk0 = CTX_BLK0 if ctx else 0
    pairs = GLA_HEADS // 2
    st_spec = pl.BlockSpec((None, None, 4, GLA_VAL_DIM, LANES), lambda b, pr: (b, pr, 0, 0, 0))
    return pl.pallas_call(
        functools.partial(_gla_body, n=n),
        grid=(BATCH, pairs),
        in_specs=[pl.BlockSpec((n, LANES), lambda b, pr: (blk0 + b, CB_GLA_Q + pr)),
                  pl.BlockSpec((n, LANES), lambda b, pr: (blk0 + b, CB_GLA_K + pr)),
                  pl.BlockSpec((n, 2 * LANES), lambda b, pr: (blk0 + b, CB_GLA_V // 2 + pr)),
                  pl.BlockSpec((n, 2 * LANES), lambda b, pr: (blk0 + b, CB_GLA_R // 2 + pr)),
                  pl.BlockSpec((n, LANES), lambda b, pr: (blk0 + b, 0)),
                  pl.BlockSpec((LANES, LANES), lambda b, pr: (0, pr)),
                  pl.BlockSpec((LANES, LANES), lambda b, pr: (0, pr)),
                  pl.BlockSpec((1, LANES), lambda b, pr: (0, pr)),
                  pl.BlockSpec((1, LANES), lambda b, pr: (0, pr)),
                  pl.BlockSpec((1, GLA_VAL_DIM), lambda b, pr: (0, 0)),
                  st_spec],
        out_specs=[pl.BlockSpec((n, 2 * LANES), lambda b, pr: (b, pr)), st_spec],
        out_shape=[jax.ShapeDtypeStruct((BATCH * n, MIX_W), BF16),
                   jax.ShapeDtypeStruct((BATCH, pairs, 4, GLA_VAL_DIM, LANES), F32)],
        scratch_shapes=[pltpu.VMEM((n, LANES), F32), pltpu.VMEM((n, LANES), F32),
                        pltpu.VMEM((n, 2 * LANES), F32), pltpu.VMEM((4, GLA_VAL_DIM, LANES), F32)],
        compiler_params=_cparams(("arbitrary",) * 2),
    )(p, p, p, p, lr, wg_f, wg_b, bg_f, bg_b, gain, s0)


def _gla_gate_weights(w_gate, b_gate):
    wf = jnp.zeros((LANES, GLA_QK_W), F32).at[:GLA_RANK].set(w_gate[0])
    wb = jnp.zeros((LANES, GLA_QK_W), F32).at[GLA_RANK:2 * GLA_RANK].set(w_gate[1])
    return wf.astype(BF16), wb.astype(BF16), b_gate[0:1], b_gate[1:2]


def _dft_tables(n):
    def cs(m):
        idx = (np.arange(m)[:, None] * np.arange(m)[None, :]) % m
        ang = 2.0 * np.pi * idx / m
        return np.cos(ang) / np.sqrt(m), np.sin(ang) / np.sqrt(m)
    cn, sn = cs(n)
    cg, sg = cs(FNET_GROUP_DIM)
    as_bf16 = lambda a: jnp.asarray(a, F32).astype(BF16)
    return as_bf16(cn), as_bf16(sn), as_bf16(np.concatenate([cg, sg], axis=1))


def _fnet_body(u_ref, cn_ref, sn_ref, csg_ref, o_ref):
    t = _dot(u_ref[...], csg_ref[...]).astype(BF16)
    y = _dot(cn_ref[...], t[:, :FNET_GROUP_DIM]) - _dot(sn_ref[...], t[:, FNET_GROUP_DIM:])
    o_ref[...] = y.astype(BF16)


def _fnet(p, *, ctx):
    n = CTX_LEN if ctx else SEQ
    blk0 = CTX_BLK0 if ctx else 0
    cn, sn, csg = _dft_tables(n)
    return pl.pallas_call(
        _fnet_body,
        grid=(BATCH, FNET_GROUPS),
        in_specs=[pl.BlockSpec((n, LANES), lambda b, g: (blk0 + b, CB_FNET_U + g)),
                  pl.BlockSpec((n, n), lambda b, g: (0, 0)),
                  pl.BlockSpec((n, n), lambda b, g: (0, 0)),
                  pl.BlockSpec((FNET_GROUP_DIM, 2 * FNET_GROUP_DIM), lambda b, g: (0, 0))],
        out_specs=pl.BlockSpec((n, LANES), lambda b, g: (b, g)),
        out_shape=jax.ShapeDtypeStruct((BATCH * n, MIX_W), BF16),
        compiler_params=_cparams(("arbitrary",) * 2),
    )(p, cn, sn, csg)


def _rope_tables():
    half = DIFF_QK_DIM // 2
    inv = ROPE_BASE ** (-jnp.arange(0, half, 2, dtype=F32) / half)
    t = jnp.arange(SEQ)
    ang_r = (t // GRID_W).astype(F32)[:, None] * inv
    ang_c = (t % GRID_W).astype(F32)[:, None] * inv
    cos = jnp.concatenate([jnp.cos(ang_r)] * 2 + [jnp.cos(ang_c)] * 2, axis=1)
    sin = jnp.concatenate([jnp.sin(ang_r)] * 2 + [jnp.sin(ang_c)] * 2, axis=1)
    rot = np.zeros((LANES, LANES), np.float32)
    q4 = half // 2
    for base in range(0, LANES, half):
        for j in range(q4):
            rot[base + j + q4, base + j] = -1.0
            rot[base + j, base + j + q4] = 1.0
    return jnp.tile(cos, (1, 2)), jnp.tile(sin, (1, 2)), jnp.asarray(rot, BF16)


def _rope(x, cos, sin, rot):
    return x.astype(F32) * cos + _dot(x, rot) * sin


def _diff_out(q, k_list, v_list, lam, gain, out_scale):
    q = q * (DIFF_QK_DIM ** -0.5)
    lane = lax.broadcasted_iota(jnp.int32, q.shape, 1)
    o = None
    for sub, coef in ((lane < DIFF_QK_DIM, None), (lane >= DIFF_QK_DIM, lam)):
        qs = jnp.where(sub, q, 0.0).astype(BF16)
        os = _softmax_pv([_dot_nt(qs, k) for k in k_list], v_list)
        o = os if coef is None else o - coef * os
    return (_rms_rows(o, gain) * out_scale).astype(BF16)


def _diff_lat_body(q_ref, k_ref, v_ref, kc_ref, vc_ref, cosq_ref, sinq_ref, cosk_ref, sink_ref, rot_ref,
                   lam_ref, gain_ref, o_ref, k_scr, *, out_scale):
    @pl.when(pl.program_id(2) == 0)
    def _():
        k_scr[...] = _rope(k_ref[...], cosk_ref[...], sink_ref[...], rot_ref[...]).astype(BF16)

    for part in range(DIFF_QB // DIFF_SUB):
        rows = slice(part * DIFF_SUB, (part + 1) * DIFF_SUB)
        q = _rope(q_ref[rows, :], cosq_ref[rows, :], sinq_ref[rows, :], rot_ref[...])
        o_ref[rows, :] = _diff_out(q, [kc_ref[...], k_scr[...]], [vc_ref[...], v_ref[...]], lam_ref[:, 0:1],
                                   gain_ref[...], out_scale)


def _diff_ctx_body(q_ref, kc_ref, vc_ref, lam_ref, gain_ref, o_ref, *, out_scale):
    o_ref[...] = _diff_out(q_ref[...].astype(F32), [kc_ref[...]], [vc_ref[...]], lam_ref[:, 0:1],
                           gain_ref[...], out_scale)


def _diff_lat(p, rope, lam_row, gain, lam_init):
    cos, sin, rot = rope
    nqb = SEQ // DIFF_QB
    const = lambda b, h, qb: (0, 0)
    return pl.pallas_call(
        functools.partial(_diff_lat_body, out_scale=1.0 - lam_init),
        grid=(BATCH, DIFF_HEADS, nqb),
        in_specs=[pl.BlockSpec((DIFF_QB, LANES), lambda b, h, qb: (b * nqb + qb, CB_DIFF_Q + h)),
                  pl.BlockSpec((SEQ, LANES), lambda b, h, qb: (b, CB_DIFF_K + h)),
                  pl.BlockSpec((SEQ, LANES), lambda b, h, qb: (b, CB_DIFF_V + h)),
                  pl.BlockSpec((CTX_LEN, LANES), lambda b, h, qb: (CTX_BLK0 + b, CB_DIFF_K + h)),
                  pl.BlockSpec((CTX_LEN, LANES), lambda b, h, qb: (CTX_BLK0 + b, CB_DIFF_V + h)),
                  pl.BlockSpec((DIFF_QB, LANES), lambda b, h, qb: (qb, 0)),
                  pl.BlockSpec((DIFF_QB, LANES), lambda b, h, qb: (qb, 0)),
                  pl.BlockSpec((SEQ, LANES), const),
                  pl.BlockSpec((SEQ, LANES), const),
                  pl.BlockSpec((LANES, LANES), const),
                  pl.BlockSpec((1, LANES), const),
                  pl.BlockSpec((1, DIFF_V_DIM), const)],
        out_specs=pl.BlockSpec((DIFF_QB, LANES), lambda b, h, qb: (b * nqb + qb, h)),
        out_shape=jax.ShapeDtypeStruct((T_LAT, MIX_W), BF16),
        scratch_shapes=[pltpu.VMEM((SEQ, LANES), BF16)],
        compiler_params=_cparams(("arbitrary",) * 3),
    )(p, p, p, p, p, cos, sin, cos, sin, rot, lam_row, gain)


def _diff_ctx(p, lam_row, gain, lam_init):
    const = lambda b, h: (0, 0)
    return pl.pallas_call(
        functools.partial(_diff_ctx_body, out_scale=1.0 - lam_init),
        grid=(BATCH, DIFF_HEADS),
        in_specs=[pl.BlockSpec((CTX_LEN, LANES), lambda b, h: (CTX_BLK0 + b, CB_DIFF_Q + h)),
                  pl.BlockSpec((CTX_LEN, LANES), lambda b, h: (CTX_BLK0 + b, CB_DIFF_K + h)),
                  pl.BlockSpec((CTX_LEN, LANES), lambda b, h: (CTX_BLK0 + b, CB_DIFF_V + h)),
                  pl.BlockSpec((1, LANES), const),
                  pl.BlockSpec((1, DIFF_V_DIM), const)],
        out_specs=pl.BlockSpec((CTX_LEN, LANES), lambda b, h: (b, h)),
        out_shape=jax.ShapeDtypeStruct((T_CTX, MIX_W), BF16),
        compiler_params=_cparams(("arbitrary",) * 2),
    )(p, p, p, lam_row, gain)


N_LAT_TILES = T_LAT // ROW_TM


def _merge_body(*refs, with_ctx):
    gates = refs[0:N_BRANCH]
    y_lat = refs[N_BRANCH:2 * N_BRANCH]
    y_ctx = refs[2 * N_BRANCH:3 * N_BRANCH] if with_ctx else None
    wb_ref, m_ref = refs[-2], refs[-1]
    is_lat = pl.program_id(0) < N_LAT_TILES
    acc = None
    for i in range(N_BRANCH):
        y = y_lat[i][...]
        if with_ctx:
            y = jnp.where(is_lat, y, y_ctx[i][...])
        t = _sigmoid(gates[i][...].astype(F32)) * _dot(y, wb_ref[i])
        acc = t if acc is None else acc + t
    m_ref[...] = acc.astype(BF16)


def _merge(p, ys_lat, ys_ctx, w_branch):
    with_ctx = ys_ctx is not None
    t_rows = T_ALL if with_ctx else T_LAT
    gate_specs = [pl.BlockSpec((ROW_TM, D_MODEL), functools.partial(lambda i, g: (i, g), g=g))
                  for g in range(N_BRANCH)]
    y_specs = [pl.BlockSpec((ROW_TM, MIX_W), lambda i: (jnp.minimum(i, N_LAT_TILES - 1), 0))
               for _ in range(N_BRANCH)]
    ys = list(ys_lat)
    if with_ctx:
        y_specs += [pl.BlockSpec((ROW_TM, MIX_W), lambda i: (jnp.maximum(i - N_LAT_TILES, 0), 0))
                    for _ in range(N_BRANCH)]
        ys += list(ys_ctx)
    return pl.pallas_call(
        functools.partial(_merge_body, with_ctx=with_ctx),
        grid=(t_rows // ROW_TM,),
        in_specs=gate_specs + y_specs + [pl.BlockSpec((N_BRANCH, MIX_W, D_MODEL), lambda i: (0, 0, 0))],
        out_specs=pl.BlockSpec((ROW_TM, D_MODEL), lambda i: (i, 0)),
        out_shape=jax.ShapeDtypeStruct((t_rows, D_MODEL), BF16),
        compiler_params=_cparams(("arbitrary",)),
    )(p, p, p, p, *ys, w_branch)


def _residual_ln(x, gate, f, g, b):
    return _ln_rows(DEEPNORM_ALPHA * x + gate * f) * g + b


def _outproj_body(*refs, split_input):
    if split_input:
        m_ref, x_ref, xc_ref, gate_ref, w_ref, g_ref, b_ref, o_ref = refs
        x = jnp.where(pl.program_id(0) < N_LAT_TILES, x_ref[...], xc_ref[...])
    else:
        m_ref, x_ref, gate_ref, w_ref, g_ref, b_ref, o_ref = refs
        x = x_ref[...]
    f = _dot(m_ref[...], w_ref[...])
    o_ref[...] = _residual_ln(x, gate_ref[...], f, g_ref[...], b_ref[...])


def _outproj(m, x_lat, x_ctx, mod, layer, w_out, ln_g, ln_b, t_rows):
    split_input = x_ctx is not None
    brow = _mod_row(ROW_TM)
    vec = pl.BlockSpec((1, D_MODEL), lambda i: (0, 0))
    rows = pl.BlockSpec((ROW_TM, D_MODEL), lambda i: (i, 0))
    x_specs, xs = [rows], [x_lat]
    if split_input:
        x_specs = [pl.BlockSpec((ROW_TM, D_MODEL), lambda i: (jnp.minimum(i, N_LAT_TILES - 1), 0)),
                   pl.BlockSpec((ROW_TM, D_MODEL), lambda i: (jnp.maximum(i - N_LAT_TILES, 0), 0))]
        xs = [x_lat, x_ctx]
    return pl.pallas_call(
        functools.partial(_outproj_body, split_input=split_input),
        grid=(t_rows // ROW_TM,),
        in_specs=[rows] + x_specs
        + [pl.BlockSpec((None, 1, D_MODEL), lambda i: (layer * SUBLANES + brow(i), 0, 2)),
           pl.BlockSpec((D_MODEL, D_MODEL), lambda i: (0, 0)), vec, vec],
        out_specs=rows,
        out_shape=jax.ShapeDtypeStruct((t_rows, D_MODEL), F32),
        compiler_params=_cparams(("arbitrary",)),
    )(m, *xs, mod, w_out, ln_g, ln_b)


RT_E1, RT_E2, RT_W1, RT_W2, RT_RANK1, RT_RANK2 = range(6)


def _first_lane_of_max(vals, lane_f):
    m = vals.max(-1, keepdims=True)
    idx = jnp.where(vals == m, lane_f, float(LANES)).min(-1, keepdims=True)
    return m, idx


def _router_body(x_ref, sh_ref, sc_ref, wr_ref, br_ref, h_ref, route_ref, cnt_ref, run_scr):
    @pl.when(pl.program_id(0) == 0)
    def _():
        run_scr[...] = jnp.zeros_like(run_scr)

    tm = ROW_TM
    h = _ln_rows(x_ref[...]) * (1.0 + sc_ref[...]) + sh_ref[...]
    hb = h.astype(BF16)
    for s in range(ROW_SLABS):
        h_ref[pl.ds(s, tm, stride=ROW_SLABS), :] = hb[:, s * LANES:(s + 1) * LANES].astype(F32)
    logits = _dot(hb, wr_ref[...]) + br_ref[...]
    lane = lax.broadcasted_iota(jnp.int32, (tm, LANES), 1)
    lane_f = lane.astype(F32)
    is_group = lane < N_GROUPS
    g_logits = jnp.where(is_group, logits, NEG_INF)
    g_max, g_sel = _first_lane_of_max(g_logits, lane_f)
    g_w = 1.0 / jnp.where(is_group, jnp.exp(logits - g_max), 0.0).sum(-1, keepdims=True)
    lo = N_GROUPS + EXPERTS_PER_GROUP * g_sel
    e_logits = jnp.where((lane_f >= lo) & (lane_f < lo + EXPERTS_PER_GROUP), logits, NEG_INF)
    v1, i1 = _first_lane_of_max(e_logits, lane_f)
    v2, i2 = _first_lane_of_max(jnp.where(lane_f == i1, NEG_INF, e_logits), lane_f)
    t = jnp.exp(v2 - v1)
    w1 = g_w / (1.0 + t)
    w2 = g_w * t / (1.0 + t)
    e1 = i1 - N_GROUPS
    e2 = i2 - N_GROUPS

    oh1 = lane_f == e1
    oh2 = lane_f == e2
    both = jnp.where(oh1 | oh2, 1.0, 0.0)
    ri = lax.broadcasted_iota(jnp.int32, (tm, tm), 0)
    ci = lax.broadcasted_iota(jnp.int32, (tm, tm), 1)
    before = jnp.where(ci < ri, 1.0, 0.0).astype(BF16)
    excl = _dot(before, both.astype(BF16)) + run_scr[0:1, :]
    rank1 = jnp.where(oh1, excl, 0.0).sum(-1, keepdims=True)
    rank2 = jnp.where(oh2, excl, 0.0).sum(-1, keepdims=True)
    run_scr[...] = run_scr[...] + both.sum(0, keepdims=True)
    cnt_ref[...] = run_scr[...]

    rec = jnp.zeros((tm, LANES), F32)
    for idx, val in ((RT_E1, e1), (RT_E2, e2), (RT_W1, w1), (RT_W2, w2), (RT_RANK1, rank1), (RT_RANK2, rank2)):
        rec = jnp.where(lane == idx, val, rec)
    route_ref[...] = rec


def _router(x1, mod, layer, w_r, b_r, t_rows):
    brow = _mod_row(ROW_TM)
    mrow = lambda i: layer * SUBLANES + brow(i)
    return pl.pallas_call(
        _router_body,
        grid=(t_rows // ROW_TM,),
        in_specs=[pl.BlockSpec((ROW_TM, D_MODEL), lambda i: (i, 0)),
                  pl.BlockSpec((None, 1, D_MODEL), lambda i: (mrow(i), 0, 3)),
                  pl.BlockSpec((None, 1, D_MODEL), lambda i: (mrow(i), 0, 4)),
                  pl.BlockSpec((D_MODEL, LANES), lambda i: (0, 0)),
                  pl.BlockSpec((1, LANES), lambda i: (0, 0))],
        out_specs=[pl.BlockSpec((ROW_TM * ROW_SLABS, LANES), lambda i: (i, 0)),
                   pl.BlockSpec((ROW_TM, LANES), lambda i: (i, 0)),
                   pl.BlockSpec((SUBLANES, LANES), lambda i: (0, 0))],
        out_shape=[jax.ShapeDtypeStruct((t_rows * ROW_SLABS, LANES), F32),
                   jax.ShapeDtypeStruct((t_rows, LANES), F32),
                   jax.ShapeDtypeStruct((SUBLANES, LANES), F32)],
        scratch_shapes=[pltpu.VMEM((SUBLANES, LANES), F32)],
        compiler_params=_cparams(("arbitrary",)),
    )(x1, mod, mod, w_r, b_r)


def _router_weights(w_group, b_group, w_router, b_router):
    wr = jnp.concatenate([w_group, jnp.transpose(w_router, (1, 0, 2)).reshape(D_MODEL, N_EXPERTS)], axis=1)
    br = jnp.concatenate([b_group, b_router.reshape(N_EXPERTS)])
    pad = LANES - wr.shape[1]
    return jnp.pad(wr, ((0, 0), (0, pad))).astype(BF16), jnp.pad(br, (0, pad)).reshape(1, LANES)


def _token_rows(ref, tok):
    return ref.at[pl.ds(pl.multiple_of(tok * ROW_SLABS, ROW_SLABS), ROW_SLABS), :]


DMA_UNROLL = 8


def _dispatch_body(pos_ref, h_ref, xs_in_ref, xs_ref, sem):
    del xs_in_ref
    tt = ROW_TM
    base = pl.program_id(0) * tt

    def issue(j, carry):
        for k in range(2):
            pltpu.make_async_copy(_token_rows(h_ref, j), _token_rows(xs_ref, pos_ref[2 * (base + j) + k]),
                                  sem).start()
        return carry

    def drain(j, carry):
        pltpu.make_async_copy(_token_rows(h_ref, 0), _token_rows(xs_ref, 0), sem).wait()
        return carry

    lax.fori_loop(0, tt, issue, 0, unroll=DMA_UNROLL)
    lax.fori_loop(0, 2 * tt, drain, 0, unroll=DMA_UNROLL)


def _dispatch(pos, h, n_slots):
    t_rows = h.shape[0] // ROW_SLABS
    xs0 = jnp.zeros((n_slots * ROW_SLABS, LANES), F32)
    return pl.pallas_call(
        _dispatch_body,
        grid_spec=pltpu.PrefetchScalarGridSpec(
            num_scalar_prefetch=1,
            grid=(t_rows // ROW_TM,),
            in_specs=[pl.BlockSpec((ROW_TM * ROW_SLABS, LANES), lambda i, pos: (i, 0)),
                      pl.BlockSpec(memory_space=pl.ANY)],
            out_specs=pl.BlockSpec(memory_space=pl.ANY),
            scratch_shapes=[pltpu.SemaphoreType.DMA(())]),
        out_shape=jax.ShapeDtypeStruct(xs0.shape, F32),
        input_output_aliases={2: 0},
        compiler_params=_cparams(("arbitrary",)),
    )(pos, h, xs0)


def _expert_body(te_ref, nu_ref, xs_ref, wgu_ref, wd_ref, ys_ref, wgu_b, wd_b):
    i = pl.program_id(0)
    tm = EXPERT_TM
    used = i < nu_ref[0]

    @pl.when(used)
    def _():
        @pl.when((i == 0) | (te_ref[i] != te_ref[jnp.maximum(i - 1, 0)]))
        def _():
            def cast(c, carry):
                r = pl.ds(pl.multiple_of(c * ROW_TM, ROW_TM), ROW_TM)
                wgu_b[r, :] = wgu_ref[r, :].astype(BF16)
                return carry
            lax.fori_loop(0, D_MODEL // ROW_TM, cast, 0)
            wd_b[...] = wd_ref[...].astype(BF16)

        x = jnp.concatenate([xs_ref[pl.ds(s, tm, stride=ROW_SLABS), :].astype(BF16) for s in range(ROW_SLABS)],
                            axis=1)
        hgu = _dot(x, wgu_b[...])
        hg = hgu[:, :EXPERT_DIM]
        hu = hgu[:, EXPERT_DIM:]
        act = (hg * _sigmoid(hg) * hu).astype(BF16)
        y = _dot(act, wd_b[...])
        for s in range(ROW_SLABS):
            ys_ref[pl.ds(s, tm, stride=ROW_SLABS), :] = y[:, s * LANES:(s + 1) * LANES]

    @pl.when(jnp.logical_not(used))
    def _():
        ys_ref[...] = jnp.zeros_like(ys_ref)


def _experts(tile_expert, n_used, xs, w_gu, w_down, layer):
    n_tiles = xs.shape[0] // (EXPERT_TM * ROW_SLABS)
    blk = EXPERT_TM * ROW_SLABS
    wsel = lambda i, te, nu: (layer * N_EXPERTS + te[i], 0, 0)
    return pl.pallas_call(
        _expert_body,
        grid_spec=pltpu.PrefetchScalarGridSpec(
            num_scalar_prefetch=2,
            grid=(n_tiles,),
            in_specs=[pl.BlockSpec((blk, LANES), lambda i, te, nu: (jnp.minimum(i, nu[0] - 1), 0)),
                      pl.BlockSpec((None, D_MODEL, 2 * EXPERT_DIM), wsel),
                      pl.BlockSpec((None, EXPERT_DIM, D_MODEL), wsel)],
            out_specs=pl.BlockSpec((blk, LANES), lambda i, te, nu: (i, 0)),
            scratch_shapes=[pltpu.VMEM((D_MODEL, 2 * EXPERT_DIM), BF16),
                            pltpu.VMEM((EXPERT_DIM, D_MODEL), BF16)]),
        out_shape=jax.ShapeDtypeStruct(xs.shape, F32),
        compiler_params=_cparams(("arbitrary",)),
    )(tile_expert, n_used, xs, w_gu, w_down)


GATHER_PITCH = ROW_SLABS + SUBLANES


def _final_body(pos_ref, x_ref, ys_ref, route_ref, gate_ref, g_ref, b_ref, out_ref, buf, sem):
    tm = ROW_TM
    i = pl.program_id(0)
    slot = lax.rem(i, 2)

    def gather_copy(tile, sl, t, k):
        dst = buf.at[sl, pl.ds(pl.multiple_of((k * tm + t) * GATHER_PITCH, SUBLANES), ROW_SLABS), :]
        return pltpu.make_async_copy(_token_rows(ys_ref, pos_ref[(tile * tm + t) * 2 + k]), dst, sem.at[sl])

    def start_tile(tile, sl):
        def issue(t, carry):
            for k in range(2):
                gather_copy(tile, sl, t, k).start()
            return carry
        lax.fori_loop(0, tm, issue, 0, unroll=DMA_UNROLL // 2)

    @pl.when(i == 0)
    def _():
        start_tile(0, 0)

    @pl.when(i + 1 < pl.num_programs(0))
    def _():
        start_tile(i + 1, 1 - slot)

    def drain(t, carry):
        for k in range(2):
            gather_copy(i, slot, 0, k).wait()
        return carry
    lax.fori_loop(0, tm, drain, 0, unroll=DMA_UNROLL // 2)

    rt = route_ref[...]
    w1 = rt[:, RT_W1:RT_W1 + 1]
    w2 = rt[:, RT_W2:RT_W2 + 1]
    f = jnp.concatenate(
        [w1 * buf[slot, pl.ds(s, tm, stride=GATHER_PITCH), :]
         + w2 * buf[slot, pl.ds(tm * GATHER_PITCH + s, tm, stride=GATHER_PITCH), :] for s in range(ROW_SLABS)],
        axis=1)
    out_ref[...] = _residual_ln(x_ref[...], gate_ref[...], f, g_ref[...], b_ref[...])


def _final(pos, x1, ys, route, mod, layer, ln_g, ln_b, t_rows):
    brow = _mod_row(ROW_TM)
    vec = pl.BlockSpec((1, D_MODEL), lambda i, pos: (0, 0))
    return pl.pallas_call(
        _final_body,
        grid_spec=pltpu.PrefetchScalarGridSpec(
            num_scalar_prefetch=1,
            grid=(t_rows // ROW_TM,),
            in_specs=[pl.BlockSpec((ROW_TM, D_MODEL), lambda i, pos: (i, 0)),
                      pl.BlockSpec(memory_space=pl.ANY),
                      pl.BlockSpec((ROW_TM, LANES), lambda i, pos: (i, 0)),
                      pl.BlockSpec((None, 1, D_MODEL), lambda i, pos: (layer * SUBLANES + brow(i), 0, 5)),
                      vec, vec],
            out_specs=pl.BlockSpec((ROW_TM, D_MODEL), lambda i, pos: (i, 0)),
            scratch_shapes=[pltpu.VMEM((2, 2 * ROW_TM * GATHER_PITCH, LANES), F32),
                            pltpu.SemaphoreType.DMA((2,))]),
        out_shape=jax.ShapeDtypeStruct((t_rows, D_MODEL), F32),
        compiler_params=_cparams(("arbitrary",)),
    )(pos, x1, ys, route, mod, ln_g, ln_b)


def _moe(x1, mod, layer, router_w, w_gu, w_down, ln_g, ln_b, t_rows):
    h, route, cnt = _router(x1, mod, layer, *router_w, t_rows)
    n_tiles = -(-(2 * t_rows + N_EXPERTS * (EXPERT_TM - 1)) // EXPERT_TM)
    counts = cnt[0, :N_EXPERTS].astype(jnp.int32)
    tiles_per = (counts + EXPERT_TM - 1) // EXPERT_TM
    tile_end = jnp.cumsum(tiles_per)
    tile_start = tile_end - tiles_per
    n_used = tile_end[-1:]
    e = route[:, RT_E1:RT_E2 + 1].astype(jnp.int32)
    rank = route[:, RT_RANK1:RT_RANK2 + 1].astype(jnp.int32)
    pos = (tile_start[e] * EXPERT_TM + rank).reshape(-1)
    ti = jnp.minimum(jnp.arange(n_tiles, dtype=jnp.int32), n_used[0] - 1)
    tile_expert = jnp.sum((ti[:, None] >= tile_end[None, :]).astype(jnp.int32), axis=1)
    xs = _dispatch(pos, h, n_tiles * EXPERT_TM)
    ys = _experts(tile_expert, n_used, xs, w_gu, w_down, layer)
    return _final(pos, x1, ys, route, mod, layer, ln_g, ln_b, t_rows)


def _prep_w_in(w):
    offs = np.concatenate([[0], np.cumsum(ALL_SPLITS)])
    piece = lambda i: w[:, int(offs[i]):int(offs[i + 1])]
    w_main = jnp.concatenate([piece(i) for i in _PIECE_ORDER], axis=1).astype(BF16)
    w_lr = jnp.pad(piece(_LR_PIECE), ((0, 0), (0, LANES - 2 * GLA_RANK))).astype(BF16)
    return w_main, w_lr


def kernel(x, c, ctx, c_ctx, w_mod, b_mod, w_in, na_rpb, gla_w_gate, gla_b_gate, gla_norm, diff_lambda, diff_norm,
           w_branch, w_out, ln1_g, ln1_b, ln2_g, ln2_b, w_group, b_group, w_router, b_router, w_gu, w_down):
    x_lat, x_ctx = x.reshape(T_LAT, D_MODEL), ctx.reshape(T_CTX, D_MODEL)
    c8 =jnp.concatenate([c, c_ctx[None], jnp.zeros((SUBLANES - BATCH - 1, D_MODEL), F32)], axis=0)
    mod = _modulation(c8, w_mod, b_mod).reshape(DEPTH * SUBLANES, 1, 6 * D_MODEL)
    rope = _rope_tables()
    w_gu_flat = w_gu.reshape(DEPTH * N_EXPERTS, D_MODEL, 2 * EXPERT_DIM)
    w_down_flat = w_down.reshape(DEPTH * N_EXPERTS, EXPERT_DIM, D_MODEL)
    s_zero = jnp.zeros((BATCH, GLA_HEADS // 2, 4, GLA_VAL_DIM, LANES), F32)

    for l in range(DEPTH):
        last = l == DEPTH - 1
        t_rows = T_LAT if last else T_ALL
        lam_init = 0.8 - 0.6 * math.exp(-0.3 * l)
        lq1, lk1, lq2, lk2 = diff_lambda[l].astype(F32)
        lam = jnp.exp(jnp.sum(lq1 * lk1)) - jnp.exp(jnp.sum(lq2 * lk2)) + lam_init
        lam_row = jnp.full((1, LANES), lam, F32)

        w_main, w_lr = _prep_w_in(w_in[l])
        p, lr = _projection(x_lat, x_ctx, mod, l, w_main, w_lr, skip_ctx_gates=last)

        gla_w = _gla_gate_weights(gla_w_gate[l], gla_b_gate[l])
        gla_gain = gla_norm[l].reshape(1, GLA_VAL_DIM)
        diff_gain = diff_norm[l].reshape(1, DIFF_V_DIM)
        y_gla_ctx, states = _gla(p, lr, *gla_w, gla_gain, s_zero, ctx=True)
        y_gla, _ = _gla(p, lr, *gla_w, gla_gain, states, ctx=False)
        ys_lat = (_na_lat(p, _na_bias_tiles(na_rpb[l])), y_gla, _fnet(p, ctx=False),
                  _diff_lat(p, rope, lam_row, diff_gain, lam_init))
        ys_ctx = None
        if not last:
            ys_ctx = (_na_ctx(p), y_gla_ctx, _fnet(p, ctx=True), _diff_ctx(p, lam_row, diff_gain, lam_init))

        m = _merge(p, ys_lat, ys_ctx, w_branch[l].astype(BF16))
        x1 = _outproj(m, x_lat, x_ctx, mod, l, w_out[l].astype(BF16), ln1_g[l].reshape(1, -1),
                      ln1_b[l].reshape(1, -1), t_rows)
        router_w = _router_weights(w_group[l], b_group[l], w_router[l], b_router[l])
        x_lat = _moe(x1, mod, l, router_w, w_gu_flat, w_down_flat, ln2_g[l].reshape(1, -1),
                     ln2_b[l].reshape(1, -1), t_rows)
        x_ctx = None
    return x_lat.reshape(BATCH, SEQ, D_MODEL)
```

```python
import functools
import math

import numpy as np
import jax
import jax.numpy as jnp
from jax import lax
from jax.experimental import pallas as pl
from jax.experimental.pallas import tpu as pltpu

F32 = jnp.float32
BF16 = jnp.bfloat16

D_MODEL = 2048
BATCH = 4
SEQ = 2048
DEPTH = 2
GRID_W = 64
CTX_LEN = 256
N_BRANCH = 4
MIX_W = D_MODEL // N_BRANCH
NA_HEADS = 4
NA_HEAD_DIM = MIX_W // NA_HEADS
NA_WIN_ROWS = 8
NA_WIN_COLS = 16
GLA_HEADS = 4
GLA_VAL_DIM = MIX_W // GLA_HEADS
GLA_KEY_DIM = GLA_VAL_DIM // 2
GLA_RANK = 16
GLA_TAU = 16.0
GLA_CHUNK = 64
FNET_GROUPS = 4
FNET_GROUP_DIM = MIX_W // FNET_GROUPS
DIFF_HEADS = 4
DIFF_V_DIM = MIX_W // DIFF_HEADS
DIFF_QK_DIM = DIFF_V_DIM // 2
ROPE_BASE = 10000.0
N_GROUPS = 4
EXPERTS_PER_GROUP = 8
N_EXPERTS = N_GROUPS * EXPERTS_PER_GROUP
EXPERT_DIM = D_MODEL // 4
DEEPNORM_ALPHA = (2 * DEPTH) ** 0.25
LN_EPS = 1e-6
RMS_EPS = 1e-5
NEG_INF = -1e30

GLA_QK_W = GLA_HEADS * GLA_KEY_DIM
DIFF_QK_W = DIFF_HEADS * 2 * DIFF_QK_DIM
KV_SPLITS = (MIX_W, MIX_W, GLA_QK_W, MIX_W, 2 * GLA_RANK, DIFF_QK_W, MIX_W)
Q_SPLITS = (MIX_W, GLA_QK_W, MIX_W, MIX_W, DIFF_QK_W)
GATE_W = N_BRANCH * D_MODEL
ALL_SPLITS = KV_SPLITS + Q_SPLITS + (GATE_W,)

LANES = 128
SUBLANES = 8
VMEM_LIMIT = 56 * 1024 * 1024

T_LAT = BATCH * SEQ
T_CTX = BATCH * CTX_LEN
T_ALL = T_LAT + T_CTX
ROW_SLABS = D_MODEL // LANES

CB_GATES = 0
CB_NA_K = 64
CB_NA_V = 68
CB_GLA_K = 72
CB_GLA_V = 74
CB_DIFF_K = 78
CB_DIFF_V = 82
CB_NA_Q = 86
CB_GLA_Q = 90
CB_GLA_R = 92
CB_FNET_U = 96
CB_DIFF_Q = 100
P_COLS = 104 * LANES
_PIECE_ORDER = (12, 0, 1, 2, 3, 5, 6, 7, 8, 9, 10, 11)
_LR_PIECE = 4

PROJ_TM = 1024
PROJ_TN = 1024
ROW_TM = 256
EXPERT_TM = 256
NA_QROWS = 8
NA_KROWS = 16
DIFF_QB = 512
DIFF_SUB = 256
CTX_BLK0 = T_LAT // CTX_LEN


def _cparams(sem, vmem=VMEM_LIMIT):
    return pltpu.CompilerParams(dimension_semantics=sem, vmem_limit_bytes=vmem)


def _dot(a, b):
    return jnp.dot(a, b, preferred_element_type=F32)


def _dot_nt(a, b):
    return lax.dot_general(a, b, (((1,), (1,)), ((), ())), preferred_element_type=F32)


def _dot_tn(a, b):
    return lax.dot_general(a, b, (((0,), (0,)), ((), ())), preferred_element_type=F32)


def _sigmoid(x):
    return 1.0 / (1.0 + jnp.exp(-x))


def _ln_rows(x):
    mu = jnp.mean(x, -1, keepdims=True)
    xc = x - mu
    var = jnp.mean(xc * xc, -1, keepdims=True)
    return xc * lax.rsqrt(var + LN_EPS)


def _rms_rows(x, gain):
    return x * lax.rsqrt(jnp.mean(x * x, -1, keepdims=True) + RMS_EPS) * gain


def _mod_row(tile_rows):
    per_batch = SEQ // tile_rows
    return lambda i: jnp.minimum(i // per_batch, BATCH)


MOD_SPLIT = 4


def _mod_body(c_ref, *refs):
    w_refs, b_ref, o_ref = refs[:MOD_SPLIT], refs[MOD_SPLIT], refs[MOD_SPLIT + 1]
    c = c_ref[...]
    s = (c * _sigmoid(c)).astype(BF16)
    tn = w_refs[0].shape[-1]
    for part, w_ref in enumerate(w_refs):
        cols = slice(part * tn, (part + 1) * tn)
        o_ref[:, cols] = _dot(s, w_ref[...].astype(BF16)) + b_ref[:, cols]


def _modulation(c8, w_mod, b_mod):
    depth, d, cols = w_mod.shape
    tn = 512
    w_specs = [pl.BlockSpec((None, d, tn), functools.partial(lambda l, j, part: (l, 0, j * MOD_SPLIT + part), part=part))
               for part in range(MOD_SPLIT)]
    return pl.pallas_call(
        _mod_body,
        grid=(depth, cols // (tn * MOD_SPLIT)),
        in_specs=[pl.BlockSpec((SUBLANES, d), lambda l, j: (0, 0))] + w_specs
        + [pl.BlockSpec((None, 1, tn * MOD_SPLIT), lambda l, j: (l, 0, j))],
        out_specs=pl.BlockSpec((None, SUBLANES, tn * MOD_SPLIT), lambda l, j: (l, 0, j)),
        out_shape=jax.ShapeDtypeStruct((depth, SUBLANES, cols), F32),
        compiler_params=_cparams(("arbitrary", "arbitrary")),
    )(c8, *([w_mod] * MOD_SPLIT), b_mod.reshape(depth, 1, cols))


PROJ_LAT_TILES = T_LAT // PROJ_TM


def _proj_body(*refs, split_input, skip_ctx_gates):
    if split_input:
        x_ref, xc_ref, sh_ref, sc_ref, w_ref, wlr_ref, p_ref, lr_ref, h_scr = refs
    else:
        x_ref, sh_ref, sc_ref, w_ref, wlr_ref, p_ref, lr_ref, h_scr = refs
    i = pl.program_id(0)
    j = pl.program_id(1)

    @pl.when(j == 0)
    def _():
        def chunk(c, carry):
            r = pl.ds(pl.multiple_of(c * ROW_TM, ROW_TM), ROW_TM)
            xv = x_ref[r, :]
            if split_input:
                xv = jnp.where(i < PROJ_LAT_TILES, xv, xc_ref[r, :])
            h = _ln_rows(xv) * (1.0 + sc_ref[...]) + sh_ref[...]
            hb = h.astype(BF16)
            h_scr[r, :] = hb
            lr_ref[r, :] = _dot(hb, wlr_ref[...])
            return carry
        lax.fori_loop(0, PROJ_TM // ROW_TM, chunk, 0)

    def project():
        p_ref[...] = _dot(h_scr[...], w_ref[...]).astype(BF16)

    if skip_ctx_gates:
        unused = (i >= PROJ_LAT_TILES) & (j < GATE_W // PROJ_TN)
        pl.when(jnp.logical_not(unused))(project)

        @pl.when(unused)
        def _():
            p_ref[...] = jnp.zeros_like(p_ref)
    else:
        project()


def _projection(x_lat, x_ctx, mod, layer, w_main, w_lr, *, skip_ctx_gates):
    split_input = x_ctx is not None
    brow = _mod_row(PROJ_TM)
    mrow = lambda i: layer * SUBLANES + brow(i)
    x_specs = [pl.BlockSpec((PROJ_TM, D_MODEL), lambda i, j: (jnp.minimum(i, PROJ_LAT_TILES - 1), 0))
               if split_input else pl.BlockSpec((PROJ_TM, D_MODEL), lambda i, j: (i, 0))]
    xs = [x_lat]
    if split_input:
        assert x_ctx.shape[0] == PROJ_TM
        x_specs.append(pl.BlockSpec((PROJ_TM, D_MODEL), lambda i, j: (0, 0), pipeline_mode=pl.Buffered(1)))
        xs.append(x_ctx)
    return pl.pallas_call(
        functools.partial(_proj_body, split_input=split_input, skip_ctx_gates=skip_ctx_gates),
        grid=(T_ALL // PROJ_TM, P_COLS // PROJ_TN),
        in_specs=x_specs + [pl.BlockSpec((None, 1, D_MODEL), lambda i, j: (mrow(i), 0, 0)),
                            pl.BlockSpec((None, 1, D_MODEL), lambda i, j: (mrow(i), 0, 1)),
                            pl.BlockSpec((D_MODEL, PROJ_TN), lambda i, j: (0, j)),
                            pl.BlockSpec((D_MODEL, LANES), lambda i, j: (0, 0))],
        out_specs=[pl.BlockSpec((PROJ_TM, PROJ_TN), lambda i, j: (i, j)),
                   pl.BlockSpec((PROJ_TM, LANES), lambda i, j: (i, 0))],
        out_shape=[jax.ShapeDtypeStruct((T_ALL, P_COLS), BF16),
                   jax.ShapeDtypeStruct((T_ALL, LANES), F32)],
        scratch_shapes=[pltpu.VMEM((PROJ_TM, D_MODEL), BF16)],
        compiler_params=_cparams(("arbitrary", "arbitrary")),
    )(*xs, mod, mod, w_main, w_lr)


def _na_key_row_start(rb):
    return jnp.clip(rb * NA_QROWS - NA_WIN_ROWS // 2, 0, SEQ // GRID_W - NA_KROWS)


def _softmax_pv(s_list, v_list):
    m = s_list[0].max(-1, keepdims=True)
    for s in s_list[1:]:
        m = jnp.maximum(m, s.max(-1, keepdims=True))
    den = None
    o = None
    for s, v in zip(s_list, v_list):
        p = jnp.exp(s - m)
        ps = p.sum(-1, keepdims=True)
        den = ps if den is None else den + ps
        pv = _dot(p.astype(BF16), v)
        o = pv if o is None else o + pv
    return o / den


def _na_lat_body(q_ref, k_ref, v_ref, kc_ref, vc_ref, tile_ref, o_ref, bias_ref):
    rb = pl.program_id(1)
    krows = NA_KROWS * GRID_W
    grid_rows = SEQ // GRID_W

    @pl.when(pl.program_id(2) == 0)
    def _():
        left = lax.broadcasted_iota(jnp.int32, (GRID_W, LANES), 1) < GRID_W
        k0 = _na_key_row_start(rb)
        for qr in range(NA_QROWS):
            r = rb * NA_QROWS + qr
            r0 = jnp.clip(r - NA_WIN_ROWS // 2, 0, grid_rows - NA_WIN_ROWS)
            for kp in range(NA_KROWS // 2):
                idx = []
                for kk in range(2):
                    kr = k0 + 2 * kp + kk
                    inside = (kr >= r0) & (kr < r0 + NA_WIN_ROWS)
                    idx.append(jnp.where(inside, kr - r + NA_WIN_ROWS - 1, 2 * NA_WIN_ROWS - 1))
                bias_ref[qr * GRID_W:(qr + 1) * GRID_W, kp * LANES:(kp + 1) * LANES] = jnp.where(
                    left, tile_ref[idx[0]], tile_ref[idx[1]])

    start = pl.multiple_of(_na_key_row_start(rb) * GRID_W, 4 * GRID_W)
    k = k_ref[pl.ds(start, krows), :]
    v = v_ref[pl.ds(start, krows), :]
    kc = kc_ref[...]
    vc = vc_ref[...]
    scale = NA_HEAD_DIM ** -0.5
    half = NA_QROWS * GRID_W // 2
    for part in range(2):
        rows = slice(part * half, (part + 1) * half)
        q = q_ref[rows, :]
        s_lat = _dot_nt(q, k) * scale + bias_ref[rows, :]
        s_ctx = _dot_nt(q, kc) * scale
        o_ref[rows, :] = _softmax_pv([s_lat, s_ctx], [v, vc]).astype(BF16)


def _na_ctx_body(q_ref, kc_ref, vc_ref, o_ref):
    s = _dot_nt(q_ref[...], kc_ref[...]) * (NA_HEAD_DIM ** -0.5)
    o_ref[...] = _softmax_pv([s], [vc_ref[...]]).astype(BF16)


def _na_bias_tiles(rpb):
    n_dr = 2 * NA_WIN_ROWS - 1
    sel_c = np.zeros((GRID_W, GRID_W, 2 * NA_WIN_COLS - 1), np.float32)
    ok_c = np.zeros((GRID_W, GRID_W), bool)
    for qc in range(GRID_W):
        w0 = int(np.clip(qc - NA_WIN_COLS // 2, 0, GRID_W - NA_WIN_COLS))
        for kc in range(GRID_W):
            if w0 <= kc < w0 + NA_WIN_COLS:
                ok_c[qc, kc] = True
                sel_c[qc, kc, int(np.clip(kc - qc + NA_WIN_COLS - 1, 0, 2 * NA_WIN_COLS - 2))] = 1.0
    t = jnp.einsum('xyc,hac->haxy', jnp.asarray(sel_c), rpb.astype(F32), precision=lax.Precision.HIGHEST)
    t = jnp.where(jnp.asarray(ok_c)[None, None], t, NEG_INF)
    t = jnp.concatenate([t, jnp.full((NA_HEADS, 1, GRID_W, GRID_W), NEG_INF, F32)], axis=1)
    assert t.shape[1] == n_dr + 1
    return jnp.concatenate([t, t], axis=-1)


def _na_lat(p, tiles):
    qrows = NA_QROWS * GRID_W
    n_rb = SEQ // qrows
    return pl.pallas_call(
        _na_lat_body,
        grid=(NA_HEADS, n_rb, BATCH),
        in_specs=[pl.BlockSpec((qrows, LANES), lambda h, rb, b: (b * n_rb + rb, CB_NA_Q + h)),
                  pl.BlockSpec((SEQ, LANES), lambda h, rb, b: (b, CB_NA_K + h)),
                  pl.BlockSpec((SEQ, LANES), lambda h, rb, b: (b, CB_NA_V + h)),
                  pl.BlockSpec((CTX_LEN, LANES), lambda h, rb, b: (CTX_BLK0 + b, CB_NA_K + h)),
                  pl.BlockSpec((CTX_LEN, LANES), lambda h, rb, b: (CTX_BLK0 + b, CB_NA_V + h)),
                  pl.BlockSpec((None, 2 * NA_WIN_ROWS, GRID_W, LANES), lambda h, rb, b: (h, 0, 0, 0))],
        out_specs=pl.BlockSpec((qrows, LANES), lambda h, rb, b: (b * n_rb + rb, h)),
        out_shape=jax.ShapeDtypeStruct((T_LAT, MIX_W), BF16),
        scratch_shapes=[pltpu.VMEM((qrows, NA_KROWS * GRID_W), F32)],
        compiler_params=_cparams(("arbitrary",) * 3),
    )(p, p, p, p, p, tiles)


def _na_ctx(p):
    return pl.pallas_call(
        _na_ctx_body,
        grid=(BATCH, NA_HEADS),
        in_specs=[pl.BlockSpec((CTX_LEN, LANES), lambda b, h: (CTX_BLK0 + b, CB_NA_Q + h)),
                  pl.BlockSpec((CTX_LEN, LANES), lambda b, h: (CTX_BLK0 + b, CB_NA_K + h)),
                  pl.BlockSpec((CTX_LEN, LANES), lambda b, h: (CTX_BLK0 + b, CB_NA_V + h))],
        out_specs=pl.BlockSpec((CTX_LEN, LANES), lambda b, h: (b, h)),
        out_shape=jax.ShapeDtypeStruct((T_CTX, MIX_W), BF16),
        compiler_params=_cparams(("arbitrary",) * 2),
    )(p, p, p)


def _log_sigmoid(z):
    return jnp.minimum(z, 0.0) - jnp.log1p(jnp.exp(-jnp.abs(z)))


GLA_GROUP = 256


def _gla_body(q_ref, k_ref, v_ref, r_ref, lr_ref, wgf_ref, wgb_ref, bgf_ref, bgb_ref, gain_ref, s0_ref,
              y_ref, sfin_ref, la_scr, o_scr, qh_scr, ut_scr, sp_scr, g_scr, st_scr, *, n):
    c = GLA_CHUNK
    nc = n // c
    grp = GLA_GROUP
    cpg = grp // c
    lr = lr_ref[...].astype(BF16)
    la_scr[0] = _log_sigmoid(_dot(lr, wgf_ref[...]) + bgf_ref[...]) / GLA_TAU
    la_scr[1] = _log_sigmoid(_dot(lr, wgb_ref[...]) + bgb_ref[...]) / GLA_TAU
    o_scr[...] = jnp.zeros_like(o_scr)

    lane = lax.broadcasted_iota(jnp.int32, (grp, LANES), 1)
    head_mask = (lane < GLA_KEY_DIM, lane >= GLA_KEY_DIM)
    ri = lax.broadcasted_iota(jnp.int32, (grp, grp), 0)
    ci = lax.broadcasted_iota(jnp.int32, (grp, grp), 1)
    same_chunk = lax.shift_right_logical(ri, 6) == lax.shift_right_logical(ci, 6)
    assert c == 1 << 6
    tri_mask = (same_chunk & (ri >= ci), same_chunk & (ci >= ri))
    tri_mat = tuple(jnp.where(m, 1.0, 0.0).astype(BF16) for m in tri_mask)
    chunk_ones = jnp.where(same_chunk, 1.0, 0.0).astype(BF16)
    scale = GLA_KEY_DIM ** -0.5

    def group(gi, carry):
        r = pl.ds(pl.multiple_of(gi * grp, grp), grp)
        q = q_ref[r, :].astype(F32)
        k = k_ref[r, :].astype(F32)
        for d in range(2):
            la = la_scr[d, r, :]
            la_hi = la.astype(BF16)
            la_lo = (la - la_hi.astype(F32)).astype(BF16)
            cum = _dot(tri_mat[d], la_hi) + _dot(tri_mat[d], la_lo)
            tot = _dot(chunk_ones, la_hi) + _dot(chunk_ones, la_lo)
            qf = q * jnp.exp(cum) * scale
            k_in = (k * jnp.exp(-cum)).astype(BF16)
            k_end = k * jnp.exp(tot - cum)
            g = jnp.exp(tot)
            for ch in range(cpg):
                g_scr[d, pl.ds(gi * cpg + ch, 1), :] = g[ch * c:ch * c + 1, :]
            for hh in range(2):
                chain = hh * 2 + d
                cols = slice(hh * GLA_VAL_DIM, (hh + 1) * GLA_VAL_DIM)
                qh = jnp.where(head_mask[hh], qf, 0.0).astype(BF16)
                qh_scr[chain, r, :] = qh
                a = jnp.where(tri_mask[d], _dot_nt(qh, k_in), 0.0)
                vh = v_ref[r, cols]
                o_scr[r, cols] = o_scr[r, cols] + _dot(a.astype(BF16), vh)
                kh = jnp.where(head_mask[hh], k_end, 0.0).astype(BF16)
                for ch in range(cpg):
                    rows = slice(ch * c, (ch + 1) * c)
                    ut_scr[chain, gi * cpg + ch] = _dot_tn(vh[rows], kh[rows])
        return carry

    lax.fori_loop(0, n // grp, group, 0)

    st_scr[...] = s0_ref[...]

    def scan(i, carry):
        for d in range(2):
            cc = i if d == 0 else nc - 1 - i
            g = g_scr[d, pl.ds(cc, 1), :]
            for hh in range(2):
                chain = hh * 2 + d
                st = st_scr[chain]
                sp_scr[chain, cc] = st.astype(BF16)
                st_scr[chain] = g * st + ut_scr[chain, cc]
        return carry

    lax.fori_loop(0, nc, scan, 0)
    sfin_ref[...] = st_scr[...]

    def inter(cc, carry):
        r = pl.ds(pl.multiple_of(cc * c, c), c)
        for hh in range(2):
            cols = slice(hh * GLA_VAL_DIM, (hh + 1) * GLA_VAL_DIM)
            o = (_dot_nt(qh_scr[hh * 2, r, :], sp_scr[hh * 2, cc])
                 + _dot_nt(qh_scr[hh * 2 + 1, r, :], sp_scr[hh * 2 + 1, cc]))
            o_scr[r, cols] = o_scr[r, cols] + o
        return carry

    lax.fori_loop(0, nc, inter, 0, unroll=4)
    for hh in range(2):
        cols = slice(hh * GLA_VAL_DIM, (hh + 1) * GLA_VAL_DIM)
        rr = r_ref[:, cols].astype(F32)
        y_ref[:, cols] = (_rms_rows(o_scr[:, cols], gain_ref[...]) * (rr * _sigmoid(rr))).astype(BF16)


def _gla(p, lr, wg_f, wg_b, bg_f, bg_b, gain, s0, *, ctx):
    n = CTX_LEN if ctx else SEQ
    blk0 = CTX_BLK0 if ctx else 0
    pairs = GLA_HEADS // 2
    st_spec = pl.BlockSpec((None, None, 4, GLA_VAL_DIM, LANES), lambda b, pr: (b, pr, 0, 0, 0))
    return pl.pallas_call(
        functools.partial(_gla_body, n=n),
        grid=(BATCH, pairs),
        in_specs=[pl.BlockSpec((n, LANES), lambda b, pr: (blk0 + b, CB_GLA_Q + pr)),
                  pl.BlockSpec((n, LANES), lambda b, pr: (blk0 + b, CB_GLA_K + pr)),
                  pl.BlockSpec((n, 2 * LANES), lambda b, pr: (blk0 + b, CB_GLA_V // 2 + pr)),
                  pl.BlockSpec((n, 2 * LANES), lambda b, pr: (blk0 + b, CB_GLA_R // 2 + pr)),
                  pl.BlockSpec((n, LANES), lambda b, pr: (blk0 + b, 0)),
                  pl.BlockSpec((LANES, LANES), lambda b, pr: (0, pr)),
                  pl.BlockSpec((LANES, LANES), lambda b, pr: (0, pr)),
                  pl.BlockSpec((1, LANES), lambda b, pr: (0, pr)),
                  pl.BlockSpec((1, LANES), lambda b, pr: (0, pr)),
                  pl.BlockSpec((1, GLA_VAL_DIM), lambda b, pr: (0, 0)),
                  st_spec],
        out_specs=[pl.BlockSpec((n, 2 * LANES), lambda b, pr: (b, pr)), st_spec],
        out_shape=[jax.ShapeDtypeStruct((BATCH * n, MIX_W), BF16),
                   jax.ShapeDtypeStruct((BATCH, pairs, 4, GLA_VAL_DIM, LANES), F32)],
        scratch_shapes=[pltpu.VMEM((2, n, LANES), F32),
                        pltpu.VMEM((n, 2 * LANES), F32),
                        pltpu.VMEM((4, n, LANES), BF16),
                        pltpu.VMEM((4, n // GLA_CHUNK, GLA_VAL_DIM, LANES), F32),
                        pltpu.VMEM((4, n // GLA_CHUNK, GLA_VAL_DIM, LANES), BF16),
                        pltpu.VMEM((2, max(n // GLA_CHUNK, SUBLANES), LANES), F32),
                        pltpu.VMEM((4, GLA_VAL_DIM, LANES), F32)],
        compiler_params=_cparams(("arbitrary",) * 2),
    )(p, p, p, p, lr, wg_f, wg_b, bg_f, bg_b, gain, s0)


def _gla_gate_weights(w_gate, b_gate):
    wf = jnp.zeros((LANES, GLA_QK_W), F32).at[:GLA_RANK].set(w_gate[0])
    wb = jnp.zeros((LANES, GLA_QK_W), F32).at[GLA_RANK:2 * GLA_RANK].set(w_gate[1])
    return wf.astype(BF16), wb.astype(BF16), b_gate[0:1], b_gate[1:2]


def _dft_tables(n):
    def cs(m):
        idx = (np.arange(m)[:, None] * np.arange(m)[None, :]) % m
        ang = 2.0 * np.pi * idx / m
        return np.cos(ang) / np.sqrt(m), np.sin(ang) / np.sqrt(m)
    cn, sn = cs(n)
    cg, sg = cs(FNET_GROUP_DIM)
    as_bf16 = lambda a: jnp.asarray(a, F32).astype(BF16)
    return as_bf16(cn), as_bf16(sn), as_bf16(np.concatenate([cg, sg], axis=1))


def _fnet_body(u_ref, cn_ref, sn_ref, csg_ref, o_ref):
    t = _dot(u_ref[...], csg_ref[...]).astype(BF16)
    y = _dot(cn_ref[...], t[:, :FNET_GROUP_DIM]) - _dot(sn_ref[...], t[:, FNET_GROUP_DIM:])
    o_ref[...] = y.astype(BF16)


def _fnet(p, *, ctx):
    n = CTX_LEN if ctx else SEQ
    blk0 = CTX_BLK0 if ctx else 0
    cn, sn, csg = _dft_tables(n)
    return pl.pallas_call(
        _fnet_body,
        grid=(BATCH, FNET_GROUPS),
        in_specs=[pl.BlockSpec((n, LANES), lambda b, g: (blk0 + b, CB_FNET_U + g)),
                  pl.BlockSpec((n, n), lambda b, g: (0, 0)),
                  pl.BlockSpec((n, n), lambda b, g: (0, 0)),
                  pl.BlockSpec((FNET_GROUP_DIM, 2 * FNET_GROUP_DIM), lambda b, g: (0, 0))],
        out_specs=pl.BlockSpec((n, LANES), lambda b, g: (b, g)),
        out_shape=jax.ShapeDtypeStruct((BATCH * n, MIX_W), BF16),
        compiler_params=_cparams(("arbitrary",) * 2),
    )(p, cn, sn, csg)


def _rope_tables():
    half = DIFF_QK_DIM // 2
    inv = ROPE_BASE ** (-jnp.arange(0, half, 2, dtype=F32) / half)
    t = jnp.arange(SEQ)
    ang_r = (t // GRID_W).astype(F32)[:, None] * inv
    ang_c = (t % GRID_W).astype(F32)[:, None] * inv
    cos = jnp.concatenate([jnp.cos(ang_r)] * 2 + [jnp.cos(ang_c)] * 2, axis=1)
    sin = jnp.concatenate([jnp.sin(ang_r)] * 2 + [jnp.sin(ang_c)] * 2, axis=1)
    rot = np.zeros((LANES, LANES), np.float32)
    q4 = half // 2
    for base in range(0, LANES, half):
        for j in range(q4):
            rot[base + j + q4, base + j] = -1.0
            rot[base + j, base + j + q4] = 1.0
    return jnp.tile(cos, (1, 2)), jnp.tile(sin, (1, 2)), jnp.asarray(rot, BF16)


def _rope(x, cos, sin, rot):
    return x.astype(F32) * cos + _dot(x, rot) * sin


def _diff_out(q, k_list, v_list, lam, gain, out_scale):
    q = q * (DIFF_QK_DIM ** -0.5)
    lane = lax.broadcasted_iota(jnp.int32, q.shape, 1)
    o = None
    for sub, coef in ((lane < DIFF_QK_DIM, None), (lane >= DIFF_QK_DIM, lam)):
        qs = jnp.where(sub, q, 0.0).astype(BF16)
        os = _softmax_pv([_dot_nt(qs, k) for k in k_list], v_list)
        o = os if coef is None else o - coef * os
    return (_rms_rows(o, gain) * out_scale).astype(BF16)


def _diff_lat_body(q_ref, k_ref, v_ref, kc_ref, vc_ref, cosq_ref, sinq_ref, cosk_ref, sink_ref, rot_ref,
                   lam_ref, gain_ref, o_ref, k_scr, *, out_scale):
    @pl.when(pl.program_id(2) == 0)
    def _():
        k_scr[...] = _rope(k_ref[...], cosk_ref[...], sink_ref[...], rot_ref[...]).astype(BF16)

    for part in range(DIFF_QB // DIFF_SUB):
        rows = slice(part * DIFF_SUB, (part + 1) * DIFF_SUB)
        q = _rope(q_ref[rows, :], cosq_ref[rows, :], sinq_ref[rows, :], rot_ref[...])
        o_ref[rows, :] = _diff_out(q, [kc_ref[...], k_scr[...]], [vc_ref[...], v_ref[...]], lam_ref[:, 0:1],
                                   gain_ref[...], out_scale)


def _diff_ctx_body(q_ref, kc_ref, vc_ref, lam_ref, gain_ref, o_ref, *, out_scale):
    o_ref[...] = _diff_out(q_ref[...].astype(F32), [kc_ref[...]], [vc_ref[...]], lam_ref[:, 0:1],
                           gain_ref[...], out_scale)


def _diff_lat(p, rope, lam_row, gain, lam_init):
    cos, sin, rot = rope
    nqb = SEQ // DIFF_QB
    const = lambda b, h, qb: (0, 0)
    return pl.pallas_call(
        functools.partial(_diff_lat_body, out_scale=1.0 - lam_init),
        grid=(BATCH, DIFF_HEADS, nqb),
        in_specs=[pl.BlockSpec((DIFF_QB, LANES), lambda b, h, qb: (b * nqb + qb, CB_DIFF_Q + h)),
                  pl.BlockSpec((SEQ, LANES), lambda b, h, qb: (b, CB_DIFF_K + h)),
                  pl.BlockSpec((SEQ, LANES), lambda b, h, qb: (b, CB_DIFF_V + h)),
                  pl.BlockSpec((CTX_LEN, LANES), lambda b, h, qb: (CTX_BLK0 + b, CB_DIFF_K + h)),
                  pl.BlockSpec((CTX_LEN, LANES), lambda b, h, qb: (CTX_BLK0 + b, CB_DIFF_V + h)),
                  pl.BlockSpec((DIFF_QB, LANES), lambda b, h, qb: (qb, 0)),
                  pl.BlockSpec((DIFF_QB, LANES), lambda b, h, qb: (qb, 0)),
                  pl.BlockSpec((SEQ, LANES), const),
                  pl.BlockSpec((SEQ, LANES), const),
                  pl.BlockSpec((LANES, LANES), const),
                  pl.BlockSpec((1, LANES), const),
                  pl.BlockSpec((1, DIFF_V_DIM), const)],
        out_specs=pl.BlockSpec((DIFF_QB, LANES), lambda b, h, qb: (b * nqb + qb, h)),
        out_shape=jax.ShapeDtypeStruct((T_LAT, MIX_W), BF16),
        scratch_shapes=[pltpu.VMEM((SEQ, LANES), BF16)],
        compiler_params=_cparams(("arbitrary",) * 3),
    )(p, p, p, p, p, cos, sin, cos, sin, rot, lam_row, gain)


def _diff_ctx(p, lam_row, gain, lam_init):
    const = lambda b, h: (0, 0)
    return pl.pallas_call(
        functools.partial(_diff_ctx_body, out_scale=1.0 - lam_init),
        grid=(BATCH, DIFF_HEADS),
        in_specs=[pl.BlockSpec((CTX_LEN, LANES), lambda b, h: (CTX_BLK0 + b, CB_DIFF_Q + h)),
                  pl.BlockSpec((CTX_LEN, LANES), lambda b, h: (CTX_BLK0 + b, CB_DIFF_K + h)),
                  pl.BlockSpec((CTX_LEN, LANES), lambda b, h: (CTX_BLK0 + b, CB_DIFF_V + h)),
                  pl.BlockSpec((1, LANES), const),
                  pl.BlockSpec((1, DIFF_V_DIM), const)],
        out_specs=pl.BlockSpec((CTX_LEN, LANES), lambda b, h: (b, h)),
        out_shape=jax.ShapeDtypeStruct((T_CTX, MIX_W), BF16),
        compiler_params=_cparams(("arbitrary",) * 2),
    )(p, p, p, lam_row, gain)


N_LAT_TILES = T_LAT // ROW_TM


def _merge_body(*refs, with_ctx):
    gates = refs[0:N_BRANCH]
    y_lat = refs[N_BRANCH:2 * N_BRANCH]
    y_ctx = refs[2 * N_BRANCH:3 * N_BRANCH] if with_ctx else None
    wb_ref, m_ref = refs[-2], refs[-1]
    is_lat = pl.program_id(0) < N_LAT_TILES
    acc = None
    for i in range(N_BRANCH):
        y = y_lat[i][...]
        if with_ctx:
            y = jnp.where(is_lat, y, y_ctx[i][...])
        t = _sigmoid(gates[i][...].astype(F32)) * _dot(y, wb_ref[i])
        acc = t if acc is None else acc + t
    m_ref[...] = acc.astype(BF16)


def _merge(p, ys_lat, ys_ctx, w_branch):
    with_ctx = ys_ctx is not None
    t_rows = T_ALL if with_ctx else T_LAT
    gate_specs = [pl.BlockSpec((ROW_TM, D_MODEL), functools.partial(lambda i, g: (i, g), g=g))
                  for g in range(N_BRANCH)]
    y_specs = [pl.BlockSpec((ROW_TM, MIX_W), lambda i: (jnp.minimum(i, N_LAT_TILES - 1), 0))
               for _ in range(N_BRANCH)]
    ys = list(ys_lat)
    if with_ctx:
        y_specs += [pl.BlockSpec((ROW_TM, MIX_W), lambda i: (jnp.maximum(i - N_LAT_TILES, 0), 0))
                    for _ in range(N_BRANCH)]
        ys += list(ys_ctx)
    return pl.pallas_call(
        functools.partial(_merge_body, with_ctx=with_ctx),
        grid=(t_rows // ROW_TM,),
        in_specs=gate_specs + y_specs + [pl.BlockSpec((N_BRANCH, MIX_W, D_MODEL), lambda i: (0, 0, 0))],
        out_specs=pl.BlockSpec((ROW_TM, D_MODEL), lambda i: (i, 0)),
        out_shape=jax.ShapeDtypeStruct((t_rows, D_MODEL), BF16),
        compiler_params=_cparams(("arbitrary",)),
    )(p, p, p, p, *ys, w_branch)


def _residual_ln(x, gate, f, g, b):
    return _ln_rows(DEEPNORM_ALPHA * x + gate * f) * g + b


def _outproj_body(*refs, split_input):
    if split_input:
        m_ref, x_ref, xc_ref, gate_ref, w_ref, g_ref, b_ref, o_ref = refs
        x = jnp.where(pl.program_id(0) < N_LAT_TILES, x_ref[...], xc_ref[...])
    else:
        m_ref, x_ref, gate_ref, w_ref, g_ref, b_ref, o_ref = refs
        x = x_ref[...]
    f = _dot(m_ref[...], w_ref[...])
    o_ref[...] = _residual_ln(x, gate_ref[...], f, g_ref[...], b_ref[...])


def _outproj(m, x_lat, x_ctx, mod, layer, w_out, ln_g, ln_b, t_rows):
    split_input = x_ctx is not None
    brow = _mod_row(ROW_TM)
    vec = pl.BlockSpec((1, D_MODEL), lambda i: (0, 0))
    rows = pl.BlockSpec((ROW_TM, D_MODEL), lambda i: (i, 0))
    x_specs, xs = [rows], [x_lat]
    if split_input:
        x_specs = [pl.BlockSpec((ROW_TM, D_MODEL), lambda i: (jnp.minimum(i, N_LAT_TILES - 1), 0)),
                   pl.BlockSpec((ROW_TM, D_MODEL), lambda i: (jnp.maximum(i - N_LAT_TILES, 0), 0))]
        xs = [x_lat, x_ctx]
    return pl.pallas_call(
        functools.partial(_outproj_body, split_input=split_input),
        grid=(t_rows // ROW_TM,),
        in_specs=[rows] + x_specs
        + [pl.BlockSpec((None, 1, D_MODEL), lambda i: (layer * SUBLANES + brow(i), 0, 2)),
           pl.BlockSpec((D_MODEL, D_MODEL), lambda i: (0, 0)), vec, vec],
        out_specs=rows,
        out_shape=jax.ShapeDtypeStruct((t_rows, D_MODEL), F32),
        compiler_params=_cparams(("arbitrary",)),
    )(m, *xs, mod, w_out, ln_g, ln_b)


RT_E1, RT_E2, RT_W1, RT_W2, RT_RANK1, RT_RANK2 = range(6)


def _first_lane_of_max(vals, lane_f):
    m = vals.max(-1, keepdims=True)
    idx = jnp.where(vals == m, lane_f, float(LANES)).min(-1, keepdims=True)
    return m, idx


def _router_body(x_ref, sh_ref, sc_ref, wr_ref, br_ref, h_ref, route_ref, cnt_ref, run_scr):
    @pl.when(pl.program_id(0) == 0)
    def _():
        run_scr[...] = jnp.zeros_like(run_scr)

    tm = ROW_TM
    h = _ln_rows(x_ref[...]) * (1.0 + sc_ref[...]) + sh_ref[...]
    hb = h.astype(BF16)
    for s in range(ROW_SLABS):
        h_ref[pl.ds(s, tm, stride=ROW_SLABS), :] = hb[:, s * LANES:(s + 1) * LANES].astype(F32)
    logits = _dot(hb, wr_ref[...]) + br_ref[...]
    lane = lax.broadcasted_iota(jnp.int32, (tm, LANES), 1)
    lane_f = lane.astype(F32)
    is_group = lane < N_GROUPS
    g_logits = jnp.where(is_group, logits, NEG_INF)
    g_max, g_sel = _first_lane_of_max(g_logits, lane_f)
    g_w = 1.0 / jnp.where(is_group, jnp.exp(logits - g_max), 0.0).sum(-1, keepdims=True)
    lo = N_GROUPS + EXPERTS_PER_GROUP * g_sel
    e_logits = jnp.where((lane_f >= lo) & (lane_f < lo + EXPERTS_PER_GROUP), logits, NEG_INF)
    v1, i1 = _first_lane_of_max(e_logits, lane_f)
    v2, i2 = _first_lane_of_max(jnp.where(lane_f == i1, NEG_INF, e_logits), lane_f)
    t = jnp.exp(v2 - v1)
    w1 = g_w / (1.0 + t)
    w2 = g_w * t / (1.0 + t)
    e1 = i1 - N_GROUPS
    e2 = i2 - N_GROUPS

    oh1 = lane_f == e1
    oh2 = lane_f == e2
    both = jnp.where(oh1 | oh2, 1.0, 0.0)
    ri = lax.broadcasted_iota(jnp.int32, (tm, tm), 0)
    ci = lax.broadcasted_iota(jnp.int32, (tm, tm), 1)
    before = jnp.where(ci < ri, 1.0, 0.0).astype(BF16)
    excl = _dot(before, both.astype(BF16)) + run_scr[0:1, :]
    rank1 = jnp.where(oh1, excl, 0.0).sum(-1, keepdims=True)
    rank2 = jnp.where(oh2, excl, 0.0).sum(-1, keepdims=True)
    run_scr[...] = run_scr[...] + both.sum(0, keepdims=True)
    cnt_ref[...] = run_scr[...]

    rec = jnp.zeros((tm, LANES), F32)
    for idx, val in ((RT_E1, e1), (RT_E2, e2), (RT_W1, w1), (RT_W2, w2), (RT_RANK1, rank1), (RT_RANK2, rank2)):
        rec = jnp.where(lane == idx, val, rec)
    route_ref[...] = rec


def _router(x1, mod, layer, w_r, b_r, t_rows):
    brow = _mod_row(ROW_TM)
    mrow = lambda i: layer * SUBLANES + brow(i)
    return pl.pallas_call(
        _router_body,
        grid=(t_rows // ROW_TM,),
        in_specs=[pl.BlockSpec((ROW_TM, D_MODEL), lambda i: (i, 0)),
                  pl.BlockSpec((None, 1, D_MODEL), lambda i: (mrow(i), 0, 3)),
                  pl.BlockSpec((None, 1, D_MODEL), lambda i: (mrow(i), 0, 4)),
                  pl.BlockSpec((D_MODEL, LANES), lambda i: (0, 0)),
                  pl.BlockSpec((1, LANES), lambda i: (0, 0))],
        out_specs=[pl.BlockSpec((ROW_TM * ROW_SLABS, LANES), lambda i: (i, 0)),
                   pl.BlockSpec((ROW_TM, LANES), lambda i: (i, 0)),
                   pl.BlockSpec((SUBLANES, LANES), lambda i: (0, 0))],
        out_shape=[jax.ShapeDtypeStruct((t_rows * ROW_SLABS, LANES), F32),
                   jax.ShapeDtypeStruct((t_rows, LANES), F32),
                   jax.ShapeDtypeStruct((SUBLANES, LANES), F32)],
        scratch_shapes=[pltpu.VMEM((SUBLANES, LANES), F32)],
        compiler_params=_cparams(("arbitrary",)),
    )(x1, mod, mod, w_r, b_r)


def _router_weights(w_group, b_group, w_router, b_router):
    wr = jnp.concatenate([w_group, jnp.transpose(w_router, (1, 0, 2)).reshape(D_MODEL, N_EXPERTS)], axis=1)
    br = jnp.concatenate([b_group, b_router.reshape(N_EXPERTS)])
    pad = LANES - wr.shape[1]
    return jnp.pad(wr, ((0, 0), (0, pad))).astype(BF16), jnp.pad(br, (0, pad)).reshape(1, LANES)


def _token_rows(ref, tok):
    return ref.at[pl.ds(pl.multiple_of(tok * ROW_SLABS, ROW_SLABS), ROW_SLABS), :]


DMA_UNROLL = 8


def _dispatch_body(pos_ref, h_ref, xs_in_ref, xs_ref, sem):
    del xs_in_ref
    tt = ROW_TM
    base = pl.program_id(0) * tt

    def issue(j, carry):
        for k in range(2):
            pltpu.make_async_copy(_token_rows(h_ref, j), _token_rows(xs_ref, pos_ref[2 * (base + j) + k]),
                                  sem).start()
        return carry

    def drain(j, carry):
        pltpu.make_async_copy(_token_rows(h_ref, 0), _token_rows(xs_ref, 0), sem).wait()
        return carry

    lax.fori_loop(0, tt, issue, 0, unroll=DMA_UNROLL)
    lax.fori_loop(0, 2 * tt, drain, 0, unroll=DMA_UNROLL)


def _dispatch(pos, h, n_slots):
    t_rows = h.shape[0] // ROW_SLABS
    xs0 = jnp.zeros((n_slots * ROW_SLABS, LANES), F32)
    return pl.pallas_call(
        _dispatch_body,
        grid_spec=pltpu.PrefetchScalarGridSpec(
            num_scalar_prefetch=1,
            grid=(t_rows // ROW_TM,),
            in_specs=[pl.BlockSpec((ROW_TM * ROW_SLABS, LANES), lambda i, pos: (i, 0)),
                      pl.BlockSpec(memory_space=pl.ANY)],
            out_specs=pl.BlockSpec(memory_space=pl.ANY),
            scratch_shapes=[pltpu.SemaphoreType.DMA(())]),
        out_shape=jax.ShapeDtypeStruct(xs0.shape, F32),
        input_output_aliases={2: 0},
        compiler_params=_cparams(("arbitrary",)),
    )(pos, h, xs0)


def _expert_body(te_ref, nu_ref, xs_ref, wgu_ref, wd_ref, ys_ref, wgu_b, wd_b):
    i = pl.program_id(0)
    tm = EXPERT_TM
    used = i < nu_ref[0]

    @pl.when(used)
    def _():
        @pl.when((i == 0) | (te_ref[i] != te_ref[jnp.maximum(i - 1, 0)]))
        def _():
            def cast(c, carry):
                r = pl.ds(pl.multiple_of(c * ROW_TM, ROW_TM), ROW_TM)
                wgu_b[r, :] = wgu_ref[r, :].astype(BF16)
                return carry
            lax.fori_loop(0, D_MODEL // ROW_TM, cast, 0)
            wd_b[...] = wd_ref[...].astype(BF16)

        x = jnp.concatenate([xs_ref[pl.ds(s, tm, stride=ROW_SLABS), :].astype(BF16) for s in range(ROW_SLABS)],
                            axis=1)
        hgu = _dot(x, wgu_b[...])
        hg = hgu[:, :EXPERT_DIM]
        hu = hgu[:, EXPERT_DIM:]
        act = (hg * _sigmoid(hg) * hu).astype(BF16)
        y = _dot(act, wd_b[...])
        for s in range(ROW_SLABS):
            ys_ref[pl.ds(s, tm, stride=ROW_SLABS), :] = y[:, s * LANES:(s + 1) * LANES]

    @pl.when(jnp.logical_not(used))
    def _():
        ys_ref[...] = jnp.zeros_like(ys_ref)


def _experts(tile_expert, n_used, xs, w_gu, w_down, layer):
    n_tiles = xs.shape[0] // (EXPERT_TM * ROW_SLABS)
    blk = EXPERT_TM * ROW_SLABS
    wsel = lambda i, te, nu: (layer * N_EXPERTS + te[i], 0, 0)
    return pl.pallas_call(
        _expert_body,
        grid_spec=pltpu.PrefetchScalarGridSpec(
            num_scalar_prefetch=2,
            grid=(n_tiles,),
            in_specs=[pl.BlockSpec((blk, LANES), lambda i, te, nu: (jnp.minimum(i, nu[0] - 1), 0)),
                      pl.BlockSpec((None, D_MODEL, 2 * EXPERT_DIM), wsel),
                      pl.BlockSpec((None, EXPERT_DIM, D_MODEL), wsel)],
            out_specs=pl.BlockSpec((blk, LANES), lambda i, te, nu: (i, 0)),
            scratch_shapes=[pltpu.VMEM((D_MODEL, 2 * EXPERT_DIM), BF16),
                            pltpu.VMEM((EXPERT_DIM, D_MODEL), BF16)]),
        out_shape=jax.ShapeDtypeStruct(xs.shape, F32),
        compiler_params=_cparams(("arbitrary",)),
    )(tile_expert, n_used, xs, w_gu, w_down)


GATHER_PITCH = ROW_SLABS + SUBLANES


def _final_body(pos_ref, x_ref, ys_ref, route_ref, gate_ref, g_ref, b_ref, out_ref, buf, sem):
    tm = ROW_TM
    i = pl.program_id(0)
    slot = lax.rem(i, 2)

    def gather_copy(tile, sl, t, k):
        dst = buf.at[sl, pl.ds(pl.multiple_of((k * tm + t) * GATHER_PITCH, SUBLANES), ROW_SLABS), :]
        return pltpu.make_async_copy(_token_rows(ys_ref, pos_ref[(tile * tm + t) * 2 + k]), dst, sem.at[sl])

    def start_tile(tile, sl):
        def issue(t, carry):
            for k in range(2):
                gather_copy(tile, sl, t, k).start()
            return carry
        lax.fori_loop(0, tm, issue, 0, unroll=DMA_UNROLL // 2)

    @pl.when(i == 0)
    def _():
        start_tile(0, 0)

    @pl.when(i + 1 < pl.num_programs(0))
    def _():
        start_tile(i + 1, 1 - slot)

    def drain(t, carry):
        for k in range(2):
            gather_copy(i, slot, 0, k).wait()
        return carry
    lax.fori_loop(0, tm, drain, 0, unroll=DMA_UNROLL // 2)

    rt = route_ref[...]
    w1 = rt[:, RT_W1:RT_W1 + 1]
    w2 = rt[:, RT_W2:RT_W2 + 1]
    f = jnp.concatenate(
        [w1 * buf[slot, pl.ds(s, tm, stride=GATHER_PITCH), :]
         + w2 * buf[slot, pl.ds(tm * GATHER_PITCH + s, tm, stride=GATHER_PITCH), :] for s in range(ROW_SLABS)],
        axis=1)
    out_ref[...] = _residual_ln(x_ref[...], gate_ref[...], f, g_ref[...], b_ref[...])


def _final(pos, x1, ys, route, mod, layer, ln_g, ln_b, t_rows):
    brow = _mod_row(ROW_TM)
    vec = pl.BlockSpec((1, D_MODEL), lambda i, pos: (0, 0))
    return pl.pallas_call(
        _final_body,
        grid_spec=pltpu.PrefetchScalarGridSpec(
            num_scalar_prefetch=1,
            grid=(t_rows // ROW_TM,),
            in_specs=[pl.BlockSpec((ROW_TM, D_MODEL), lambda i, pos: (i, 0)),
                      pl.BlockSpec(memory_space=pl.ANY),
                      pl.BlockSpec((ROW_TM, LANES), lambda i, pos: (i, 0)),
                      pl.BlockSpec((None, 1, D_MODEL), lambda i, pos: (layer * SUBLANES + brow(i), 0, 5)),
                      vec, vec],
            out_specs=pl.BlockSpec((ROW_TM, D_MODEL), lambda i, pos: (i, 0)),
            scratch_shapes=[pltpu.VMEM((2, 2 * ROW_TM * GATHER_PITCH, LANES), F32),
                            pltpu.SemaphoreType.DMA((2,))]),
        out_shape=jax.ShapeDtypeStruct((t_rows, D_MODEL), F32),
        compiler_params=_cparams(("arbitrary",)),
    )(pos, x1, ys, route, mod, ln_g, ln_b)


def _moe(x1, mod, layer, router_w, w_gu, w_down, ln_g, ln_b, t_rows):
    h, route, cnt = _router(x1, mod, layer, *router_w, t_rows)
    n_tiles = -(-(2 * t_rows + N_EXPERTS * (EXPERT_TM - 1)) // EXPERT_TM)
    counts = cnt[0, :N_EXPERTS].astype(jnp.int32)
    tiles_per = (counts + EXPERT_TM - 1) // EXPERT_TM
    tile_end = jnp.cumsum(tiles_per)
    tile_start = tile_end - tiles_per
    n_used = tile_end[-1:]
    e = route[:, RT_E1:RT_E2 + 1].astype(jnp.int32)
    rank = route[:, RT_RANK1:RT_RANK2 + 1].astype(jnp.int32)
    pos = (tile_start[e] * EXPERT_TM + rank).reshape(-1)
    ti = jnp.minimum(jnp.arange(n_tiles, dtype=jnp.int32), n_used[0] - 1)
    tile_expert = jnp.sum((ti[:, None] >= tile_end[None, :]).astype(jnp.int32), axis=1)
    xs = _dispatch(pos, h, n_tiles * EXPERT_TM)
    ys = _experts(tile_expert, n_used, xs, w_gu, w_down, layer)
    return _final(pos, x1, ys, route, mod, layer, ln_g, ln_b, t_rows)


def _prep_w_in(w):
    offs = np.concatenate([[0], np.cumsum(ALL_SPLITS)])
    piece = lambda i: w[:, int(offs[i]):int(offs[i + 1])]
    w_main = jnp.concatenate([piece(i) for i in _PIECE_ORDER], axis=1).astype(BF16)
    w_lr = jnp.pad(piece(_LR_PIECE), ((0, 0), (0, LANES - 2 * GLA_RANK))).astype(BF16)
    return w_main, w_lr


def kernel(x, c, ctx, c_ctx, w_mod, b_mod, w_in, na_rpb, gla_w_gate, gla_b_gate, gla_norm, diff_lambda, diff_norm,
           w_branch, w_out, ln1_g, ln1_b, ln2_g, ln2_b, w_group, b_group, w_router, b_router, w_gu, w_down):
    x_lat, x_ctx = x.reshape(T_LAT, D_MODEL), ctx.reshape(T_CTX, D_MODEL)
    c8 =jnp.concatenate([c, c_ctx[None], jnp.zeros((SUBLANES - BATCH - 1, D_MODEL), F32)], axis=0)
    mod = _modulation(c8, w_mod, b_mod).reshape(DEPTH * SUBLANES, 1, 6 * D_MODEL)
    rope = _rope_tables()
    w_gu_flat = w_gu.reshape(DEPTH * N_EXPERTS, D_MODEL, 2 * EXPERT_DIM)
    w_down_flat = w_down.reshape(DEPTH * N_EXPERTS, EXPERT_DIM, D_MODEL)
    s_zero = jnp.zeros((BATCH, GLA_HEADS // 2, 4, GLA_VAL_DIM, LANES), F32)

    for l in range(DEPTH):
        last = l == DEPTH - 1
        t_rows = T_LAT if last else T_ALL
        lam_init = 0.8 - 0.6 * math.exp(-0.3 * l)
        lq1, lk1, lq2, lk2 = diff_lambda[l].astype(F32)
        lam = jnp.exp(jnp.sum(lq1 * lk1)) - jnp.exp(jnp.sum(lq2 * lk2)) + lam_init
        lam_row = jnp.full((1, LANES), lam, F32)

        w_main, w_lr = _prep_w_in(w_in[l])
        p, lr = _projection(x_lat, x_ctx, mod, l, w_main, w_lr, skip_ctx_gates=last)

        gla_w = _gla_gate_weights(gla_w_gate[l], gla_b_gate[l])
        gla_gain = gla_norm[l].reshape(1, GLA_VAL_DIM)
        diff_gain = diff_norm[l].reshape(1, DIFF_V_DIM)
        y_gla_ctx, states = _gla(p, lr, *gla_w, gla_gain, s_zero, ctx=True)
        y_gla, _ = _gla(p, lr, *gla_w, gla_gain, states, ctx=False)
        ys_lat = (_na_lat(p, _na_bias_tiles(na_rpb[l])), y_gla, _fnet(p, ctx=False),
                  _diff_lat(p, rope, lam_row, diff_gain, lam_init))
        ys_ctx = None
        if not last:
            ys_ctx = (_na_ctx(p), y_gla_ctx, _fnet(p, ctx=True), _diff_ctx(p, lam_row, diff_gain, lam_init))

        m = _merge(p, ys_lat, ys_ctx, w_branch[l].astype(BF16))
        x1 = _outproj(m, x_lat, x_ctx, mod, l, w_out[l].astype(BF16), ln1_g[l].reshape(1, -1),
                      ln1_b[l].reshape(1, -1), t_rows)
        router_w = _router_weights(w_group[l], b_group[l], w_router[l], b_router[l])
        x_lat = _moe(x1, mod, l, router_w, w_gu_flat, w_down_flat, ln2_g[l].reshape(1, -1),
                     ln2_b[l].reshape(1, -1), t_rows)
        x_ctx = None
    return x_lat.reshape(BATCH, SEQ, D_MODEL)
```

```python
import functools
import math

import numpy as np
import jax
import jax.numpy as jnp
from jax import lax
from jax.experimental import pallas as pl
from jax.experimental.pallas import tpu as pltpu

F32 = jnp.float32
BF16 = jnp.bfloat16

D_MODEL = 2048
BATCH = 4
SEQ = 2048
DEPTH = 2
GRID_W = 64
CTX_LEN = 256
N_BRANCH = 4
MIX_W = D_MODEL // N_BRANCH
NA_HEADS = 4
NA_HEAD_DIM = MIX_W // NA_HEADS
NA_WIN_ROWS = 8
NA_WIN_COLS = 16
GLA_HEADS = 4
GLA_VAL_DIM = MIX_W // GLA_HEADS
GLA_KEY_DIM = GLA_VAL_DIM // 2
GLA_RANK = 16
GLA_TAU = 16.0
GLA_CHUNK = 64
FNET_GROUPS = 4
FNET_GROUP_DIM = MIX_W // FNET_GROUPS
DIFF_HEADS = 4
DIFF_V_DIM = MIX_W // DIFF_HEADS
DIFF_QK_DIM = DIFF_V_DIM // 2
ROPE_BASE = 10000.0
N_GROUPS = 4
EXPERTS_PER_GROUP = 8
N_EXPERTS = N_GROUPS * EXPERTS_PER_GROUP
EXPERT_DIM = D_MODEL // 4
DEEPNORM_ALPHA = (2 * DEPTH) ** 0.25
LN_EPS = 1e-6
RMS_EPS = 1e-5
NEG_INF = -1e30

GLA_QK_W = GLA_HEADS * GLA_KEY_DIM
DIFF_QK_W = DIFF_HEADS * 2 * DIFF_QK_DIM
KV_SPLITS = (MIX_W, MIX_W, GLA_QK_W, MIX_W, 2 * GLA_RANK, DIFF_QK_W, MIX_W)
Q_SPLITS = (MIX_W, GLA_QK_W, MIX_W, MIX_W, DIFF_QK_W)
GATE_W = N_BRANCH * D_MODEL
ALL_SPLITS = KV_SPLITS + Q_SPLITS + (GATE_W,)

LANES = 128
SUBLANES = 8
VMEM_LIMIT = 56 * 1024 * 1024

T_LAT = BATCH * SEQ
T_CTX = BATCH * CTX_LEN
T_ALL = T_LAT + T_CTX
ROW_SLABS = D_MODEL // LANES

CB_GATES = 0
CB_NA_K = 64
CB_NA_V = 68
CB_GLA_K = 72
CB_GLA_V = 74
CB_DIFF_K = 78
CB_DIFF_V = 82
CB_NA_Q = 86
CB_GLA_Q = 90
CB_GLA_R = 92
CB_FNET_U = 96
CB_DIFF_Q = 100
P_COLS = 104 * LANES
_PIECE_ORDER = (12, 0, 1, 2, 3, 5, 6, 7, 8, 9, 10, 11)
_LR_PIECE = 4

PROJ_TM = 1024
PROJ_TN = 1024
ROW_TM = 256
EXPERT_TM = 256
NA_QROWS = 8
NA_KROWS = 16
DIFF_QB = 512
DIFF_SUB = 256
CTX_BLK0 = T_LAT // CTX_LEN


def _cparams(sem, vmem=VMEM_LIMIT):
    return pltpu.CompilerParams(dimension_semantics=sem, vmem_limit_bytes=vmem)


def _dot(a, b):
    return jnp.dot(a, b, preferred_element_type=F32)


def _dot_nt(a, b):
    return lax.dot_general(a, b, (((1,), (1,)), ((), ())), preferred_element_type=F32)


def _dot_tn(a, b):
    return lax.dot_general(a, b, (((0,), (0,)), ((), ())), preferred_element_type=F32)


def _sigmoid(x):
    return 1.0 / (1.0 + jnp.exp(-x))


def _ln_rows(x):
    mu = jnp.mean(x, -1, keepdims=True)
    xc = x - mu
    var = jnp.mean(xc * xc, -1, keepdims=True)
    return xc * lax.rsqrt(var + LN_EPS)


def _rms_rows(x, gain):
    return x * lax.rsqrt(jnp.mean(x * x, -1, keepdims=True) + RMS_EPS) * gain


def _mod_row(tile_rows):
    per_batch = SEQ // tile_rows
    return lambda i: jnp.minimum(i // per_batch, BATCH)


MOD_TK = 256


def _mod_body(c_ref, w_ref, b_ref, o_ref):
    @pl.when(pl.program_id(1) == 0)
    def _():
        o_ref[...] = jnp.broadcast_to(b_ref[...], o_ref.shape)

    c = c_ref[...]
    s = (c * _sigmoid(c)).astype(BF16)
    o_ref[...] += _dot(s, w_ref[...].astype(BF16))


def _modulation(c8, w_mod, b_mod):
    depth, d, cols = w_mod.shape
    return pl.pallas_call(
        _mod_body,
        grid=(depth, d // MOD_TK),
        in_specs=[pl.BlockSpec((SUBLANES, MOD_TK), lambda l, k: (0, k)),
                  pl.BlockSpec((None, MOD_TK, cols), lambda l, k: (l, k, 0)),
                  pl.BlockSpec((None, 1, cols), lambda l, k: (l, 0, 0))],
        out_specs=pl.BlockSpec((None, SUBLANES, cols), lambda l, k: (l, 0, 0)),
        out_shape=jax.ShapeDtypeStruct((depth, SUBLANES, cols), F32),
        compiler_params=_cparams(("arbitrary", "arbitrary")),
    )(c8, w_mod, b_mod.reshape(depth, 1, cols))


PROJ_LAT_TILES = T_LAT // PROJ_TM


def _proj_body(*refs, split_input, skip_ctx_gates):
    if split_input:
        x_ref, xc_ref, sh_ref, sc_ref, w_ref, wlr_ref, p_ref, lr_ref, h_scr = refs
    else:
        x_ref, sh_ref, sc_ref, w_ref, wlr_ref, p_ref, lr_ref, h_scr = refs
    i = pl.program_id(0)
    j = pl.program_id(1)

    @pl.when(j == 0)
    def _():
        def chunk(c, carry):
            r = pl.ds(pl.multiple_of(c * ROW_TM, ROW_TM), ROW_TM)
            xv = x_ref[r, :]
            if split_input:
                xv = jnp.where(i < PROJ_LAT_TILES, xv, xc_ref[r, :])
            h = _ln_rows(xv) * (1.0 + sc_ref[...]) + sh_ref[...]
            hb = h.astype(BF16)
            h_scr[r, :] = hb
            lr_ref[r, :] = _dot(hb, wlr_ref[...])
            return carry
        lax.fori_loop(0, PROJ_TM // ROW_TM, chunk, 0)

    def project():
        p_ref[...] = _dot(h_scr[...], w_ref[...]).astype(BF16)

    if skip_ctx_gates:
        unused = (i >= PROJ_LAT_TILES) & (j < GATE_W // PROJ_TN)
        pl.when(jnp.logical_not(unused))(project)

        @pl.when(unused)
        def _():
            p_ref[...] = jnp.zeros_like(p_ref)
    else:
        project()


def _projection(x_lat, x_ctx, mod, layer, w_main, w_lr, *, skip_ctx_gates):
    split_input = x_ctx is not None
    brow = _mod_row(PROJ_TM)
    mrow = lambda i: layer * SUBLANES + brow(i)
    x_specs = [pl.BlockSpec((PROJ_TM, D_MODEL), lambda i, j: (jnp.minimum(i, PROJ_LAT_TILES - 1), 0))
               if split_input else pl.BlockSpec((PROJ_TM, D_MODEL), lambda i, j: (i, 0))]
    xs = [x_lat]
    if split_input:
        assert x_ctx.shape[0] == PROJ_TM
        x_specs.append(pl.BlockSpec((PROJ_TM, D_MODEL), lambda i, j: (0, 0), pipeline_mode=pl.Buffered(1)))
        xs.append(x_ctx)
    return pl.pallas_call(
        functools.partial(_proj_body, split_input=split_input, skip_ctx_gates=skip_ctx_gates),
        grid=(T_ALL // PROJ_TM, P_COLS // PROJ_TN),
        in_specs=x_specs + [pl.BlockSpec((None, 1, D_MODEL), lambda i, j: (mrow(i), 0, 0)),
                            pl.BlockSpec((None, 1, D_MODEL), lambda i, j: (mrow(i), 0, 1)),
                            pl.BlockSpec((D_MODEL, PROJ_TN), lambda i, j: (0, j)),
                            pl.BlockSpec((D_MODEL, LANES), lambda i, j: (0, 0))],
        out_specs=[pl.BlockSpec((PROJ_TM, PROJ_TN), lambda i, j: (i, j)),
                   pl.BlockSpec((PROJ_TM, LANES), lambda i, j: (i, 0))],
        out_shape=[jax.ShapeDtypeStruct((T_ALL, P_COLS), BF16),
                   jax.ShapeDtypeStruct((T_ALL, LANES), F32)],
        scratch_shapes=[pltpu.VMEM((PROJ_TM, D_MODEL), BF16)],
        compiler_params=_cparams(("arbitrary", "arbitrary")),
    )(*xs, mod, mod, w_main, w_lr)


def _na_key_row_start(rb):
    return jnp.clip(rb * NA_QROWS - NA_WIN_ROWS // 2, 0, SEQ // GRID_W - NA_KROWS)


def _softmax_pv(s_list, v_list):
    m = s_list[0].max(-1, keepdims=True)
    for s in s_list[1:]:
        m = jnp.maximum(m, s.max(-1, keepdims=True))
    den = None
    o = None
    for s, v in zip(s_list, v_list):
        p = jnp.exp(s - m)
        ps = p.sum(-1, keepdims=True)
        den = ps if den is None else den + ps
        pv = _dot(p.astype(BF16), v)
        o = pv if o is None else o + pv
    return o / den


def _na_lat_body(q_ref, k_ref, v_ref, kc_ref, vc_ref, tile_ref, o_ref, bias_ref):
    rb = pl.program_id(1)
    krows = NA_KROWS * GRID_W
    grid_rows = SEQ // GRID_W

    @pl.when(pl.program_id(2) == 0)
    def _():
        left = lax.broadcasted_iota(jnp.int32, (GRID_W, LANES), 1) < GRID_W
        k0 = _na_key_row_start(rb)
        for qr in range(NA_QROWS):
            r = rb * NA_QROWS + qr
            r0 = jnp.clip(r - NA_WIN_ROWS // 2, 0, grid_rows - NA_WIN_ROWS)
            for kp in range(NA_KROWS // 2):
                idx = []
                for kk in range(2):
                    kr = k0 + 2 * kp + kk
                    inside = (kr >= r0) & (kr < r0 + NA_WIN_ROWS)
                    idx.append(jnp.where(inside, kr - r + NA_WIN_ROWS - 1, 2 * NA_WIN_ROWS - 1))
                bias_ref[qr * GRID_W:(qr + 1) * GRID_W, kp * LANES:(kp + 1) * LANES] = jnp.where(
                    left, tile_ref[idx[0]], tile_ref[idx[1]])

    start = pl.multiple_of(_na_key_row_start(rb) * GRID_W, 4 * GRID_W)
    k = k_ref[pl.ds(start, krows), :]
    v = v_ref[pl.ds(start, krows), :]
    kc = kc_ref[...]
    vc = vc_ref[...]
    scale = NA_HEAD_DIM ** -0.5
    half = NA_QROWS * GRID_W // 2
    for part in range(2):
        rows = slice(part * half, (part + 1) * half)
        q = q_ref[rows, :]
        s_lat = _dot_nt(q, k) * scale + bias_ref[rows, :]
        s_ctx = _dot_nt(q, kc) * scale
        o_ref[rows, :] = _softmax_pv([s_lat, s_ctx], [v, vc]).astype(BF16)


def _na_ctx_body(q_ref, kc_ref, vc_ref, o_ref):
    s = _dot_nt(q_ref[...], kc_ref[...]) * (NA_HEAD_DIM ** -0.5)
    o_ref[...] = _softmax_pv([s], [vc_ref[...]]).astype(BF16)


def _na_bias_tiles(rpb):
    n_dr = 2 * NA_WIN_ROWS - 1
    sel_c = np.zeros((GRID_W, GRID_W, 2 * NA_WIN_COLS - 1), np.float32)
    ok_c = np.zeros((GRID_W, GRID_W), bool)
    for qc in range(GRID_W):
        w0 = int(np.clip(qc - NA_WIN_COLS // 2, 0, GRID_W - NA_WIN_COLS))
        for kc in range(GRID_W):
            if w0 <= kc < w0 + NA_WIN_COLS:
                ok_c[qc, kc] = True
                sel_c[qc, kc, int(np.clip(kc - qc + NA_WIN_COLS - 1, 0, 2 * NA_WIN_COLS - 2))] = 1.0
    t = jnp.einsum('xyc,hac->haxy', jnp.asarray(sel_c), rpb.astype(F32), precision=lax.Precision.HIGHEST)
    t = jnp.where(jnp.asarray(ok_c)[None, None], t, NEG_INF)
    t = jnp.concatenate([t, jnp.full((NA_HEADS, 1, GRID_W, GRID_W), NEG_INF, F32)], axis=1)
    assert t.shape[1] == n_dr + 1
    return jnp.concatenate([t, t], axis=-1)


def _na_lat(p, tiles):
    qrows = NA_QROWS * GRID_W
    n_rb = SEQ // qrows
    return pl.pallas_call(
        _na_lat_body,
        grid=(NA_HEADS, n_rb, BATCH),
        in_specs=[pl.BlockSpec((qrows, LANES), lambda h, rb, b: (b * n_rb + rb, CB_NA_Q + h)),
                  pl.BlockSpec((SEQ, LANES), lambda h, rb, b: (b, CB_NA_K + h)),
                  pl.BlockSpec((SEQ, LANES), lambda h, rb, b: (b, CB_NA_V + h)),
                  pl.BlockSpec((CTX_LEN, LANES), lambda h, rb, b: (CTX_BLK0 + b, CB_NA_K + h)),
                  pl.BlockSpec((CTX_LEN, LANES), lambda h, rb, b: (CTX_BLK0 + b, CB_NA_V + h)),
                  pl.BlockSpec((None, 2 * NA_WIN_ROWS, GRID_W, LANES), lambda h, rb, b: (h, 0, 0, 0))],
        out_specs=pl.BlockSpec((qrows, LANES), lambda h, rb, b: (b * n_rb + rb, h)),
        out_shape=jax.ShapeDtypeStruct((T_LAT, MIX_W), BF16),
        scratch_shapes=[pltpu.VMEM((qrows, NA_KROWS * GRID_W), F32)],
        compiler_params=_cparams(("arbitrary",) * 3),
    )(p, p, p, p, p, tiles)


def _na_ctx(p):
    return pl.pallas_call(
        _na_ctx_body,
        grid=(BATCH, NA_HEADS),
        in_specs=[pl.BlockSpec((CTX_LEN, LANES), lambda b, h: (CTX_BLK0 + b, CB_NA_Q + h)),
                  pl.BlockSpec((CTX_LEN, LANES), lambda b, h: (CTX_BLK0 + b, CB_NA_K + h)),
                  pl.BlockSpec((CTX_LEN, LANES), lambda b, h: (CTX_BLK0 + b, CB_NA_V + h))],
        out_specs=pl.BlockSpec((CTX_LEN, LANES), lambda b, h: (b, h)),
        out_shape=jax.ShapeDtypeStruct((T_CTX, MIX_W), BF16),
        compiler_params=_cparams(("arbitrary",) * 2),
    )(p, p, p)


def _log_sigmoid(z):
    return jnp.minimum(z, 0.0) - jnp.log1p(jnp.exp(-jnp.abs(z)))


GLA_GROUP = 256


def _gla_body(q_ref, k_ref, v_ref, r_ref, lr_ref, wgf_ref, wgb_ref, bgf_ref, bgb_ref, gain_ref, s0_ref,
              y_ref, sfin_ref, la_scr, o_scr, qh_scr, ut_scr, sp_scr, g_scr, st_scr, *, n):
    c = GLA_CHUNK
    nc = n // c
    grp = GLA_GROUP
    cpg = grp // c
    lr = lr_ref[...].astype(BF16)
    la_scr[0] = _log_sigmoid(_dot(lr, wgf_ref[...]) + bgf_ref[...]) / GLA_TAU
    la_scr[1] = _log_sigmoid(_dot(lr, wgb_ref[...]) + bgb_ref[...]) / GLA_TAU
    o_scr[...] = jnp.zeros_like(o_scr)

    lane = lax.broadcasted_iota(jnp.int32, (grp, LANES), 1)
    head_mask = (lane < GLA_KEY_DIM, lane >= GLA_KEY_DIM)
    ri = lax.broadcasted_iota(jnp.int32, (grp, grp), 0)
    ci = lax.broadcasted_iota(jnp.int32, (grp, grp), 1)
    same_chunk = lax.shift_right_logical(ri, 6) == lax.shift_right_logical(ci, 6)
    assert c == 1 << 6
    tri_mask = (same_chunk & (ri >= ci), same_chunk & (ci >= ri))
    tri_mat = tuple(jnp.where(m, 1.0, 0.0).astype(BF16) for m in tri_mask)
    chunk_ones = jnp.where(same_chunk, 1.0, 0.0).astype(BF16)
    scale = GLA_KEY_DIM ** -0.5

    def group(gi, carry):
        r = pl.ds(pl.multiple_of(gi * grp, grp), grp)
        q = q_ref[r, :].astype(F32)
        k = k_ref[r, :].astype(F32)
        for d in range(2):
            la = la_scr[d, r, :]
            la_hi = la.astype(BF16)
            la_lo = (la - la_hi.astype(F32)).astype(BF16)
            cum = _dot(tri_mat[d], la_hi) + _dot(tri_mat[d], la_lo)
            tot = _dot(chunk_ones, la_hi) + _dot(chunk_ones, la_lo)
            qf = q * jnp.exp(cum) * scale
            k_in = (k * jnp.exp(-cum)).astype(BF16)
            k_end = k * jnp.exp(tot - cum)
            g = jnp.exp(tot)
            for ch in range(cpg):
                g_scr[d, pl.ds(gi * cpg + ch, 1), :] = g[ch * c:ch * c + 1, :]
            for hh in range(2):
                chain = hh * 2 + d
                cols = slice(hh * GLA_VAL_DIM, (hh + 1) * GLA_VAL_DIM)
                qh = jnp.where(head_mask[hh], qf, 0.0).astype(BF16)
                qh_scr[chain, r, :] = qh
                a = jnp.where(tri_mask[d], _dot_nt(qh, k_in), 0.0)
                vh = v_ref[r, cols]
                o_scr[r, cols] = o_scr[r, cols] + _dot(a.astype(BF16), vh)
                kh = jnp.where(head_mask[hh], k_end, 0.0).astype(BF16)
                for ch in range(cpg):
                    rows = slice(ch * c, (ch + 1) * c)
                    ut_scr[chain, gi * cpg + ch] = _dot_tn(vh[rows], kh[rows])
        return carry

    lax.fori_loop(0, n // grp, group, 0)

    st_scr[...] = s0_ref[...]

    def scan(i, carry):
        for d in range(2):
            cc = i if d == 0 else nc - 1 - i
            g = g_scr[d, pl.ds(cc, 1), :]
            for hh in range(2):
                chain = hh * 2 + d
                st = st_scr[chain]
                sp_scr[chain, cc] = st.astype(BF16)
                st_scr[chain] = g * st + ut_scr[chain, cc]
        return carry

    lax.fori_loop(0, nc, scan, 0)
    sfin_ref[...] = st_scr[...]

    def inter(cc, carry):
        r = pl.ds(pl.multiple_of(cc * c, c), c)
        for hh in range(2):
            cols = slice(hh * GLA_VAL_DIM, (hh + 1) * GLA_VAL_DIM)
            o = (_dot_nt(qh_scr[hh * 2, r, :], sp_scr[hh * 2, cc])
                 + _dot_nt(qh_scr[hh * 2 + 1, r, :], sp_scr[hh * 2 + 1, cc]))
            o_scr[r, cols] = o_scr[r, cols] + o
        return carry

    lax.fori_loop(0, nc, inter, 0, unroll=4)
    for hh in range(2):
        cols = slice(hh * GLA_VAL_DIM, (hh + 1) * GLA_VAL_DIM)
        rr = r_ref[:, cols].astype(F32)
        y_ref[:, cols] = (_rms_rows(o_scr[:, cols], gain_ref[...]) * (rr * _sigmoid(rr))).astype(BF16)


def _gla(p, lr, wg_f, wg_b, bg_f, bg_b, gain, s0, *, ctx):
    n = CTX_LEN if ctx else SEQ
    blk0 = CTX_BLK0 if ctx else 0
    pairs = GLA_HEADS // 2
    st_spec = pl.BlockSpec((None, None, 4, GLA_VAL_DIM, LANES), lambda b, pr: (b, pr, 0, 0, 0))
    return pl.pallas_call(
        functools.partial(_gla_body, n=n),
        grid=(BATCH, pairs),
        in_specs=[pl.BlockSpec((n, LANES), lambda b, pr: (blk0 + b, CB_GLA_Q + pr)),
                  pl.BlockSpec((n, LANES), lambda b, pr: (blk0 + b, CB_GLA_K + pr)),
                  pl.BlockSpec((n, 2 * LANES), lambda b, pr: (blk0 + b, CB_GLA_V // 2 + pr)),
                  pl.BlockSpec((n, 2 * LANES), lambda b, pr: (blk0 + b, CB_GLA_R // 2 + pr)),
                  pl.BlockSpec((n, LANES), lambda b, pr: (blk0 + b, 0)),
                  pl.BlockSpec((LANES, LANES), lambda b, pr: (0, pr)),
                  pl.BlockSpec((LANES, LANES), lambda b, pr: (0, pr)),
                  pl.BlockSpec((1, LANES), lambda b, pr: (0, pr)),
                  pl.BlockSpec((1, LANES), lambda b, pr: (0, pr)),
                  pl.BlockSpec((1, GLA_VAL_DIM), lambda b, pr: (0, 0)),
                  st_spec],
        out_specs=[pl.BlockSpec((n, 2 * LANES), lambda b, pr: (b, pr)), st_spec],
        out_shape=[jax.ShapeDtypeStruct((BATCH * n, MIX_W), BF16),
                   jax.ShapeDtypeStruct((BATCH, pairs, 4, GLA_VAL_DIM, LANES), F32)],
        scratch_shapes=[pltpu.VMEM((2, n, LANES), F32),
                        pltpu.VMEM((n, 2 * LANES), F32),
                        pltpu.VMEM((4, n, LANES), BF16),
                        pltpu.VMEM((4, n // GLA_CHUNK, GLA_VAL_DIM, LANES), F32),
                        pltpu.VMEM((4, n // GLA_CHUNK, GLA_VAL_DIM, LANES), BF16),
                        pltpu.VMEM((2, max(n // GLA_CHUNK, SUBLANES), LANES), F32),
                        pltpu.VMEM((4, GLA_VAL_DIM, LANES), F32)],
        compiler_params=_cparams(("arbitrary",) * 2),
    )(p, p, p, p, lr, wg_f, wg_b, bg_f, bg_b, gain, s0)


def _gla_gate_weights(w_gate, b_gate):
    wf = jnp.zeros((LANES, GLA_QK_W), F32).at[:GLA_RANK].set(w_gate[0])
    wb = jnp.zeros((LANES, GLA_QK_W), F32).at[GLA_RANK:2 * GLA_RANK].set(w_gate[1])
    return wf.astype(BF16), wb.astype(BF16), b_gate[0:1], b_gate[1:2]


def _dft_tables(n):
    def cs(m):
        idx = (np.arange(m)[:, None] * np.arange(m)[None, :]) % m
        ang = 2.0 * np.pi * idx / m
        return np.cos(ang) / np.sqrt(m), np.sin(ang) / np.sqrt(m)
    cn, sn = cs(n)
    cg, sg = cs(FNET_GROUP_DIM)
    as_bf16 = lambda a: jnp.asarray(a, F32).astype(BF16)
    return as_bf16(cn), as_bf16(sn), as_bf16(np.concatenate([cg, sg], axis=1))


def _fnet_body(u_ref, cn_ref, sn_ref, csg_ref, o_ref):
    t = _dot(u_ref[...], csg_ref[...]).astype(BF16)
    y = _dot(cn_ref[...], t[:, :FNET_GROUP_DIM]) - _dot(sn_ref[...], t[:, FNET_GROUP_DIM:])
    o_ref[...] = y.astype(BF16)


def _fnet(p, *, ctx):
    n = CTX_LEN if ctx else SEQ
    blk0 = CTX_BLK0 if ctx else 0
    cn, sn, csg = _dft_tables(n)
    return pl.pallas_call(
        _fnet_body,
        grid=(BATCH, FNET_GROUPS),
        in_specs=[pl.BlockSpec((n, LANES), lambda b, g: (blk0 + b, CB_FNET_U + g)),
                  pl.BlockSpec((n, n), lambda b, g: (0, 0)),
                  pl.BlockSpec((n, n), lambda b, g: (0, 0)),
                  pl.BlockSpec((FNET_GROUP_DIM, 2 * FNET_GROUP_DIM), lambda b, g: (0, 0))],
        out_specs=pl.BlockSpec((n, LANES), lambda b, g: (b, g)),
        out_shape=jax.ShapeDtypeStruct((BATCH * n, MIX_W), BF16),
        compiler_params=_cparams(("arbitrary",) * 2),
    )(p, cn, sn, csg)


def _rope_tables():
    half = DIFF_QK_DIM // 2
    inv = ROPE_BASE ** (-jnp.arange(0, half, 2, dtype=F32) / half)
    t = jnp.arange(SEQ)
    ang_r = (t // GRID_W).astype(F32)[:, None] * inv
    ang_c = (t % GRID_W).astype(F32)[:, None] * inv
    cos = jnp.concatenate([jnp.cos(ang_r)] * 2 + [jnp.cos(ang_c)] * 2, axis=1)
    sin = jnp.concatenate([jnp.sin(ang_r)] * 2 + [jnp.sin(ang_c)] * 2, axis=1)
    rot = np.zeros((LANES, LANES), np.float32)
    q4 = half // 2
    for base in range(0, LANES, half):
        for j in range(q4):
            rot[base + j + q4, base + j] = -1.0
            rot[base + j, base + j + q4] = 1.0
    return jnp.tile(cos, (1, 2)), jnp.tile(sin, (1, 2)), jnp.asarray(rot, BF16)


def _rope(x, cos, sin, rot):
    return x.astype(F32) * cos + _dot(x, rot) * sin


def _diff_out(q, k_list, v_list, lam, gain, out_scale):
    q = q * (DIFF_QK_DIM ** -0.5)
    lane = lax.broadcasted_iota(jnp.int32, q.shape, 1)
    o = None
    for sub, coef in ((lane < DIFF_QK_DIM, None), (lane >= DIFF_QK_DIM, lam)):
        qs = jnp.where(sub, q, 0.0).astype(BF16)
        os = _softmax_pv([_dot_nt(qs, k) for k in k_list], v_list)
        o = os if coef is None else o - coef * os
    return (_rms_rows(o, gain) * out_scale).astype(BF16)


def _diff_lat_body(q_ref, k_ref, v_ref, kc_ref, vc_ref, cosq_ref, sinq_ref, cosk_ref, sink_ref, rot_ref,
                   lam_ref, gain_ref, o_ref, k_scr, *, out_scale):
    @pl.when(pl.program_id(2) == 0)
    def _():
        k_scr[...] = _rope(k_ref[...], cosk_ref[...], sink_ref[...], rot_ref[...]).astype(BF16)

    for part in range(DIFF_QB // DIFF_SUB):
        rows = slice(part * DIFF_SUB, (part + 1) * DIFF_SUB)
        q = _rope(q_ref[rows, :], cosq_ref[rows, :], sinq_ref[rows, :], rot_ref[...])
        o_ref[rows, :] = _diff_out(q, [kc_ref[...], k_scr[...]], [vc_ref[...], v_ref[...]], lam_ref[:, 0:1],
                                   gain_ref[...], out_scale)


def _diff_ctx_body(q_ref, kc_ref, vc_ref, lam_ref, gain_ref, o_ref, *, out_scale):
    o_ref[...] = _diff_out(q_ref[...].astype(F32), [kc_ref[...]], [vc_ref[...]], lam_ref[:, 0:1],
                           gain_ref[...], out_scale)


def _diff_lat(p, rope, lam_row, gain, lam_init):
    cos, sin, rot = rope
    nqb = SEQ // DIFF_QB
    const = lambda b, h, qb: (0, 0)
    return pl.pallas_call(
        functools.partial(_diff_lat_body, out_scale=1.0 - lam_init),
        grid=(BATCH, DIFF_HEADS, nqb),
        in_specs=[pl.BlockSpec((DIFF_QB, LANES), lambda b, h, qb: (b * nqb + qb, CB_DIFF_Q + h)),
                  pl.BlockSpec((SEQ, LANES), lambda b, h, qb: (b, CB_DIFF_K + h)),
                  pl.BlockSpec((SEQ, LANES), lambda b, h, qb: (b, CB_DIFF_V + h)),
                  pl.BlockSpec((CTX_LEN, LANES), lambda b, h, qb: (CTX_BLK0 + b, CB_DIFF_K + h)),
                  pl.BlockSpec((CTX_LEN, LANES), lambda b, h, qb: (CTX_BLK0 + b, CB_DIFF_V + h)),
                  pl.BlockSpec((DIFF_QB, LANES), lambda b, h, qb: (qb, 0)),
                  pl.BlockSpec((DIFF_QB, LANES), lambda b, h, qb: (qb, 0)),
                  pl.BlockSpec((SEQ, LANES), const),
                  pl.BlockSpec((SEQ, LANES), const),
                  pl.BlockSpec((LANES, LANES), const),
                  pl.BlockSpec((1, LANES), const),
                  pl.BlockSpec((1, DIFF_V_DIM), const)],
        out_specs=pl.BlockSpec((DIFF_QB, LANES), lambda b, h, qb: (b * nqb + qb, h)),
        out_shape=jax.ShapeDtypeStruct((T_LAT, MIX_W), BF16),
        scratch_shapes=[pltpu.VMEM((SEQ, LANES), BF16)],
        compiler_params=_cparams(("arbitrary",) * 3),
    )(p, p, p, p, p, cos, sin, cos, sin, rot, lam_row, gain)


def _diff_ctx(p, lam_row, gain, lam_init):
    const = lambda b, h: (0, 0)
    return pl.pallas_call(
        functools.partial(_diff_ctx_body, out_scale=1.0 - lam_init),
        grid=(BATCH, DIFF_HEADS),
        in_specs=[pl.BlockSpec((CTX_LEN, LANES), lambda b, h: (CTX_BLK0 + b, CB_DIFF_Q + h)),
                  pl.BlockSpec((CTX_LEN, LANES), lambda b, h: (CTX_BLK0 + b, CB_DIFF_K + h)),
                  pl.BlockSpec((CTX_LEN, LANES), lambda b, h: (CTX_BLK0 + b, CB_DIFF_V + h)),
                  pl.BlockSpec((1, LANES), const),
                  pl.BlockSpec((1, DIFF_V_DIM), const)],
        out_specs=pl.BlockSpec((CTX_LEN, LANES), lambda b, h: (b, h)),
        out_shape=jax.ShapeDtypeStruct((T_CTX, MIX_W), BF16),
        compiler_params=_cparams(("arbitrary",) * 2),
    )(p, p, p, lam_row, gain)


N_LAT_TILES = T_LAT // ROW_TM


def _merge_body(*refs, with_ctx):
    gates = refs[0:N_BRANCH]
    y_lat = refs[N_BRANCH:2 * N_BRANCH]
    y_ctx = refs[2 * N_BRANCH:3 * N_BRANCH] if with_ctx else None
    wb_ref, m_ref = refs[-2], refs[-1]
    is_lat = pl.program_id(0) < N_LAT_TILES
    acc = None
    for i in range(N_BRANCH):
        y = y_lat[i][...]
        if with_ctx:
            y = jnp.where(is_lat, y, y_ctx[i][...])
        t = _sigmoid(gates[i][...].astype(F32)) * _dot(y, wb_ref[i])
        acc = t if acc is None else acc + t
    m_ref[...] = acc.astype(BF16)


def _merge(p, ys_lat, ys_ctx, w_branch):
    with_ctx = ys_ctx is not None
    t_rows = T_ALL if with_ctx else T_LAT
    gate_specs = [pl.BlockSpec((ROW_TM, D_MODEL), functools.partial(lambda i, g: (i, g), g=g))
                  for g in range(N_BRANCH)]
    y_specs = [pl.BlockSpec((ROW_TM, MIX_W), lambda i: (jnp.minimum(i, N_LAT_TILES - 1), 0))
               for _ in range(N_BRANCH)]
    ys = list(ys_lat)
    if with_ctx:
        y_specs += [pl.BlockSpec((ROW_TM, MIX_W), lambda i: (jnp.maximum(i - N_LAT_TILES, 0), 0))
                    for _ in range(N_BRANCH)]
        ys += list(ys_ctx)
    return pl.pallas_call(
        functools.partial(_merge_body, with_ctx=with_ctx),
        grid=(t_rows // ROW_TM,),
        in_specs=gate_specs + y_specs + [pl.BlockSpec((N_BRANCH, MIX_W, D_MODEL), lambda i: (0, 0, 0))],
        out_specs=pl.BlockSpec((ROW_TM, D_MODEL), lambda i: (i, 0)),
        out_shape=jax.ShapeDtypeStruct((t_rows, D_MODEL), BF16),
        compiler_params=_cparams(("arbitrary",)),
    )(p, p, p, p, *ys, w_branch)


def _residual_ln(x, gate, f, g, b):
    return _ln_rows(DEEPNORM_ALPHA * x + gate * f) * g + b


def _outproj_body(*refs, split_input):
    if split_input:
        m_ref, x_ref, xc_ref, gate_ref, w_ref, g_ref, b_ref, o_ref = refs
        x = jnp.where(pl.program_id(0) < N_LAT_TILES, x_ref[...], xc_ref[...])
    else:
        m_ref, x_ref, gate_ref, w_ref, g_ref, b_ref, o_ref = refs
        x = x_ref[...]
    f = _dot(m_ref[...], w_ref[...])
    o_ref[...] = _residual_ln(x, gate_ref[...], f, g_ref[...], b_ref[...])


def _outproj(m, x_lat, x_ctx, mod, layer, w_out, ln_g, ln_b, t_rows):
    split_input = x_ctx is not None
    brow = _mod_row(ROW_TM)
    vec = pl.BlockSpec((1, D_MODEL), lambda i: (0, 0))
    rows = pl.BlockSpec((ROW_TM, D_MODEL), lambda i: (i, 0))
    x_specs, xs = [rows], [x_lat]
    if split_input:
        x_specs = [pl.BlockSpec((ROW_TM, D_MODEL), lambda i: (jnp.minimum(i, N_LAT_TILES - 1), 0)),
                   pl.BlockSpec((ROW_TM, D_MODEL), lambda i: (jnp.maximum(i - N_LAT_TILES, 0), 0))]
        xs = [x_lat, x_ctx]
    return pl.pallas_call(
        functools.partial(_outproj_body, split_input=split_input),
        grid=(t_rows // ROW_TM,),
        in_specs=[rows] + x_specs
        + [pl.BlockSpec((None, 1, D_MODEL), lambda i: (layer * SUBLANES + brow(i), 0, 2)),
           pl.BlockSpec((D_MODEL, D_MODEL), lambda i: (0, 0)), vec, vec],
        out_specs=rows,
        out_shape=jax.ShapeDtypeStruct((t_rows, D_MODEL), F32),
        compiler_params=_cparams(("arbitrary",)),
    )(m, *xs, mod, w_out, ln_g, ln_b)


RT_E1, RT_E2, RT_W1, RT_W2, RT_RANK1, RT_RANK2 = range(6)


def _first_lane_of_max(vals, lane_f):
    m = vals.max(-1, keepdims=True)
    idx = jnp.where(vals == m, lane_f, float(LANES)).min(-1, keepdims=True)
    return m, idx


def _router_body(x_ref, sh_ref, sc_ref, wr_ref, br_ref, h_ref, route_ref, cnt_ref, run_scr):
    @pl.when(pl.program_id(0) == 0)
    def _():
        run_scr[...] = jnp.zeros_like(run_scr)

    tm = ROW_TM
    h = _ln_rows(x_ref[...]) * (1.0 + sc_ref[...]) + sh_ref[...]
    hb = h.astype(BF16)
    for s in range(ROW_SLABS):
        h_ref[pl.ds(s, tm, stride=ROW_SLABS), :] = hb[:, s * LANES:(s + 1) * LANES].astype(F32)
    logits = _dot(hb, wr_ref[...]) + br_ref[...]
    lane = lax.broadcasted_iota(jnp.int32, (tm, LANES), 1)
    lane_f = lane.astype(F32)
    is_group = lane < N_GROUPS
    g_logits = jnp.where(is_group, logits, NEG_INF)
    g_max, g_sel = _first_lane_of_max(g_logits, lane_f)
    g_w = 1.0 / jnp.where(is_group, jnp.exp(logits - g_max), 0.0).sum(-1, keepdims=True)
    lo = N_GROUPS + EXPERTS_PER_GROUP * g_sel
    e_logits = jnp.where((lane_f >= lo) & (lane_f < lo + EXPERTS_PER_GROUP), logits, NEG_INF)
    v1, i1 = _first_lane_of_max(e_logits, lane_f)
    v2, i2 = _first_lane_of_max(jnp.where(lane_f == i1, NEG_INF, e_logits), lane_f)
    t = jnp.exp(v2 - v1)
    w1 = g_w / (1.0 + t)
    w2 = g_w * t / (1.0 + t)
    e1 = i1 - N_GROUPS
    e2 = i2 - N_GROUPS

    oh1 = lane_f == e1
    oh2 = lane_f == e2
    both = jnp.where(oh1 | oh2, 1.0, 0.0)
    ri = lax.broadcasted_iota(jnp.int32, (tm, tm), 0)
    ci = lax.broadcasted_iota(jnp.int32, (tm, tm), 1)
    before = jnp.where(ci < ri, 1.0, 0.0).astype(BF16)
    excl = _dot(before, both.astype(BF16)) + run_scr[0:1, :]
    rank1 = jnp.where(oh1, excl, 0.0).sum(-1, keepdims=True)
    rank2 = jnp.where(oh2, excl, 0.0).sum(-1, keepdims=True)
    run_scr[...] = run_scr[...] + both.sum(0, keepdims=True)
    cnt_ref[...] = run_scr[...]

    rec = jnp.zeros((tm, LANES), F32)
    for idx, val in ((RT_E1, e1), (RT_E2, e2), (RT_W1, w1), (RT_W2, w2), (RT_RANK1, rank1), (RT_RANK2, rank2)):
        rec = jnp.where(lane == idx, val, rec)
    route_ref[...] = rec


def _router(x1, mod, layer, w_r, b_r, t_rows):
    brow = _mod_row(ROW_TM)
    mrow = lambda i: layer * SUBLANES + brow(i)
    return pl.pallas_call(
        _router_body,
        grid=(t_rows // ROW_TM,),
        in_specs=[pl.BlockSpec((ROW_TM, D_MODEL), lambda i: (i, 0)),
                  pl.BlockSpec((None, 1, D_MODEL), lambda i: (mrow(i), 0, 3)),
                  pl.BlockSpec((None, 1, D_MODEL), lambda i: (mrow(i), 0, 4)),
                  pl.BlockSpec((D_MODEL, LANES), lambda i: (0, 0)),
                  pl.BlockSpec((1, LANES), lambda i: (0, 0))],
        out_specs=[pl.BlockSpec((ROW_TM * ROW_SLABS, LANES), lambda i: (i, 0)),
                   pl.BlockSpec((ROW_TM, LANES), lambda i: (i, 0)),
                   pl.BlockSpec((SUBLANES, LANES), lambda i: (0, 0))],
        out_shape=[jax.ShapeDtypeStruct((t_rows * ROW_SLABS, LANES), F32),
                   jax.ShapeDtypeStruct((t_rows, LANES), F32),
                   jax.ShapeDtypeStruct((SUBLANES, LANES), F32)],
        scratch_shapes=[pltpu.VMEM((SUBLANES, LANES), F32)],
        compiler_params=_cparams(("arbitrary",)),
    )(x1, mod, mod, w_r, b_r)


def _router_weights(w_group, b_group, w_router, b_router):
    wr = jnp.concatenate([w_group, jnp.transpose(w_router, (1, 0, 2)).reshape(D_MODEL, N_EXPERTS)], axis=1)
    br = jnp.concatenate([b_group, b_router.reshape(N_EXPERTS)])
    pad = LANES - wr.shape[1]
    return jnp.pad(wr, ((0, 0), (0, pad))).astype(BF16), jnp.pad(br, (0, pad)).reshape(1, LANES)


def _token_rows(ref, tok):
    return ref.at[pl.ds(pl.multiple_of(tok * ROW_SLABS, ROW_SLABS), ROW_SLABS), :]


DMA_UNROLL = 8


def _dispatch_body(pos_ref, h_ref, xs_in_ref, xs_ref, sem, *, t_rows):
    del xs_in_ref
    tt = ROW_TM
    base = pl.program_id(0) * tt

    def issue(j, carry):
        for k in range(2):
            pltpu.make_async_copy(_token_rows(h_ref, j), _token_rows(xs_ref, pos_ref[k * t_rows + base + j]),
                                  sem).start()
        return carry

    def drain(j, carry):
        pltpu.make_async_copy(_token_rows(h_ref, 0), _token_rows(xs_ref, 0), sem).wait()
        return carry

    lax.fori_loop(0, tt, issue, 0, unroll=DMA_UNROLL)
    lax.fori_loop(0, 2 * tt, drain, 0, unroll=DMA_UNROLL)


def _dispatch(pos, h, n_slots):
    t_rows = h.shape[0] // ROW_SLABS
    xs0 = jnp.zeros((n_slots * ROW_SLABS, LANES), F32)
    return pl.pallas_call(
        functools.partial(_dispatch_body, t_rows=t_rows),
        grid_spec=pltpu.PrefetchScalarGridSpec(
            num_scalar_prefetch=1,
            grid=(t_rows // ROW_TM,),
            in_specs=[pl.BlockSpec((ROW_TM * ROW_SLABS, LANES), lambda i, pos: (i, 0)),
                      pl.BlockSpec(memory_space=pl.ANY)],
            out_specs=pl.BlockSpec(memory_space=pl.ANY),
            scratch_shapes=[pltpu.SemaphoreType.DMA(())]),
        out_shape=jax.ShapeDtypeStruct(xs0.shape, F32),
        input_output_aliases={2: 0},
        compiler_params=_cparams(("arbitrary",)),
    )(pos, h, xs0)


def _expert_body(te_ref, nu_ref, nxt_ref, slot_ref, xs_ref, wgu_hbm, wd_hbm, ys_ref,
                 wgu_f, wd_f, wgu_b, wd_b, sem, *, layer):
    i = pl.program_id(0)
    tm = EXPERT_TM
    used = i < nu_ref[0]

    def fetch(expert, sl):
        e = layer * N_EXPERTS + expert
        return (pltpu.make_async_copy(wgu_hbm.at[e], wgu_f.at[sl], sem.at[0, sl]),
                pltpu.make_async_copy(wd_hbm.at[e], wd_f.at[sl], sem.at[1, sl]))

    @pl.when(used)
    def _():
        @pl.when((i == 0) | (te_ref[i] != te_ref[jnp.maximum(i - 1, 0)]))
        def _():
            sl = slot_ref[i]

            @pl.when(i == 0)
            def _():
                for cp in fetch(te_ref[0], 0):
                    cp.start()

            @pl.when(nxt_ref[i] >= 0)
            def _():
                for cp in fetch(nxt_ref[i], 1 - sl):
                    cp.start()

            for cp in fetch(te_ref[i], sl):
                cp.wait()

            def cast(c, carry):
                r = pl.ds(pl.multiple_of(c * ROW_TM, ROW_TM), ROW_TM)
                wgu_b[r, :] = wgu_f[sl, r, :].astype(BF16)
                return carry
            lax.fori_loop(0, D_MODEL // ROW_TM, cast, 0)
            wd_b[...] = wd_f[sl].astype(BF16)

        x = jnp.concatenate([xs_ref[pl.ds(s, tm, stride=ROW_SLABS), :].astype(BF16) for s in range(ROW_SLABS)],
                            axis=1)
        hgu = _dot(x, wgu_b[...])
        hg = hgu[:, :EXPERT_DIM]
        hu = hgu[:, EXPERT_DIM:]
        act = (hg * _sigmoid(hg) * hu).astype(BF16)
        y = _dot(act, wd_b[...])
        for s in range(ROW_SLABS):
            ys_ref[pl.ds(s, tm, stride=ROW_SLABS), :] = y[:, s * LANES:(s + 1) * LANES]

    @pl.when(jnp.logical_not(used))
    def _():
        ys_ref[...] = jnp.zeros_like(ys_ref)


def _experts(tile_expert, n_used, next_expert, tile_slot, xs, w_gu, w_down, layer):
    n_tiles = xs.shape[0] // (EXPERT_TM * ROW_SLABS)
    blk = EXPERT_TM * ROW_SLABS
    return pl.pallas_call(
        functools.partial(_expert_body, layer=layer),
        grid_spec=pltpu.PrefetchScalarGridSpec(
            num_scalar_prefetch=4,
            grid=(n_tiles,),
            in_specs=[pl.BlockSpec((blk, LANES), lambda i, te, nu, nx, sl: (jnp.minimum(i, nu[0] - 1), 0)),
                      pl.BlockSpec(memory_space=pl.ANY),
                      pl.BlockSpec(memory_space=pl.ANY)],
            out_specs=pl.BlockSpec((blk, LANES), lambda i, te, nu, nx, sl: (i, 0)),
            scratch_shapes=[pltpu.VMEM((2, D_MODEL, 2 * EXPERT_DIM), F32),
                            pltpu.VMEM((2, EXPERT_DIM, D_MODEL), F32),
                            pltpu.VMEM((D_MODEL, 2 * EXPERT_DIM), BF16),
                            pltpu.VMEM((EXPERT_DIM, D_MODEL), BF16),
                            pltpu.SemaphoreType.DMA((2, 2))]),
        out_shape=jax.ShapeDtypeStruct(xs.shape, F32),
        compiler_params=_cparams(("arbitrary",)),
    )(tile_expert, n_used, next_expert, tile_slot, xs, w_gu, w_down)


GATHER_PITCH = ROW_SLABS + SUBLANES


def _final_body(pos_ref, x_ref, ys_ref, route_ref, gate_ref, g_ref, b_ref, out_ref, buf, sem, *, t_rows):
    tm = ROW_TM
    i = pl.program_id(0)
    slot = lax.rem(i, 2)

    def gather_copy(tile, sl, t, k):
        dst = buf.at[sl, pl.ds(pl.multiple_of((k * tm + t) * GATHER_PITCH, SUBLANES), ROW_SLABS), :]
        return pltpu.make_async_copy(_token_rows(ys_ref, pos_ref[k * t_rows + tile * tm + t]), dst, sem.at[sl])

    def start_tile(tile, sl):
        def issue(t, carry):
            for k in range(2):
                gather_copy(tile, sl, t, k).start()
            return carry
        lax.fori_loop(0, tm, issue, 0, unroll=DMA_UNROLL // 2)

    @pl.when(i == 0)
    def _():
        start_tile(0, 0)

    @pl.when(i + 1 < pl.num_programs(0))
    def _():
        start_tile(i + 1, 1 - slot)

    def drain(t, carry):
        for k in range(2):
            gather_copy(i, slot, 0, k).wait()
        return carry
    lax.fori_loop(0, tm, drain, 0, unroll=DMA_UNROLL // 2)

    rt = route_ref[...]
    w1 = rt[:, RT_W1:RT_W1 + 1]
    w2 = rt[:, RT_W2:RT_W2 + 1]
    f = jnp.concatenate(
        [w1 * buf[slot, pl.ds(s, tm, stride=GATHER_PITCH), :]
         + w2 * buf[slot, pl.ds(tm * GATHER_PITCH + s, tm, stride=GATHER_PITCH), :] for s in range(ROW_SLABS)],
        axis=1)
    out_ref[...] = _residual_ln(x_ref[...], gate_ref[...], f, g_ref[...], b_ref[...])


def _final(pos, x1, ys, route, mod, layer, ln_g, ln_b, t_rows):
    brow = _mod_row(ROW_TM)
    vec = pl.BlockSpec((1, D_MODEL), lambda i, pos: (0, 0))
    return pl.pallas_call(
        functools.partial(_final_body, t_rows=t_rows),
        grid_spec=pltpu.PrefetchScalarGridSpec(
            num_scalar_prefetch=1,
            grid=(t_rows // ROW_TM,),
            in_specs=[pl.BlockSpec((ROW_TM, D_MODEL), lambda i, pos: (i, 0)),
                      pl.BlockSpec(memory_space=pl.ANY),
                      pl.BlockSpec((ROW_TM, LANES), lambda i, pos: (i, 0)),
                      pl.BlockSpec((None, 1, D_MODEL), lambda i, pos: (layer * SUBLANES + brow(i), 0, 5)),
                      vec, vec],
            out_specs=pl.BlockSpec((ROW_TM, D_MODEL), lambda i, pos: (i, 0)),
            scratch_shapes=[pltpu.VMEM((2, 2 * ROW_TM * GATHER_PITCH, LANES), F32),
                            pltpu.SemaphoreType.DMA((2,))]),
        out_shape=jax.ShapeDtypeStruct((t_rows, D_MODEL), F32),
        compiler_params=_cparams(("arbitrary",)),
    )(pos, x1, ys, route, mod, ln_g, ln_b)


def _moe(x1, mod, layer, router_w, w_gu, w_down, ln_g, ln_b, t_rows):
    h, route, cnt = _router(x1, mod, layer, *router_w, t_rows)
    n_tiles = -(-(2 * t_rows + N_EXPERTS * (EXPERT_TM - 1)) // EXPERT_TM)
    counts = cnt[0, :N_EXPERTS].astype(jnp.int32)
    tiles_per = (counts + EXPERT_TM - 1) // EXPERT_TM
    tile_end = jnp.cumsum(tiles_per)
    tile_start = tile_end - tiles_per
    n_used = tile_end[-1:]
    route_t = route.T
    e = route_t[RT_E1:RT_E2 + 1].astype(jnp.int32)
    rank = route_t[RT_RANK1:RT_RANK2 + 1].astype(jnp.int32)
    pos = (tile_start[e] * EXPERT_TM + rank).reshape(-1)
    ti = jnp.minimum(jnp.arange(n_tiles, dtype=jnp.int32), n_used[0] - 1)
    tile_expert = jnp.sum((ti[:, None] >= tile_end[None, :]).astype(jnp.int32), axis=1)
    ids = jnp.arange(N_EXPERTS, dtype=jnp.int32)
    owns = tiles_per > 0
    later = jnp.where(owns[None, :] & (ids[None, :] > ids[:, None]), ids[None, :], N_EXPERTS)
    nxt_e = jnp.min(later, axis=1)
    nxt_e = jnp.where(nxt_e == N_EXPERTS, -1, nxt_e)
    order_e = jnp.cumsum(owns.astype(jnp.int32)) - 1
    next_expert = nxt_e[tile_expert]
    tile_slot = order_e[tile_expert] % 2
    xs = _dispatch(pos, h, n_tiles * EXPERT_TM)
    ys = _experts(tile_expert, n_used, next_expert, tile_slot, xs, w_gu, w_down, layer)
    return _final(pos, x1, ys, route, mod, layer, ln_g, ln_b, t_rows)


def _prep_w_in(w):
    offs = np.concatenate([[0], np.cumsum(ALL_SPLITS)])
    piece = lambda i: w[:, int(offs[i]):int(offs[i + 1])]
    w_main = jnp.concatenate([piece(i) for i in _PIECE_ORDER], axis=1).astype(BF16)
    w_lr = jnp.pad(piece(_LR_PIECE), ((0, 0), (0, LANES - 2 * GLA_RANK))).astype(BF16)
    return w_main, w_lr


def kernel(x, c, ctx, c_ctx, w_mod, b_mod, w_in, na_rpb, gla_w_gate, gla_b_gate, gla_norm, diff_lambda, diff_norm,
           w_branch, w_out, ln1_g, ln1_b, ln2_g, ln2_b, w_group, b_group, w_router, b_router, w_gu, w_down):
    x_lat, x_ctx = x.reshape(T_LAT, D_MODEL), ctx.reshape(T_CTX, D_MODEL)
    c8 =jnp.concatenate([c, c_ctx[None], jnp.zeros((SUBLANES - BATCH - 1, D_MODEL), F32)], axis=0)
    mod = _modulation(c8, w_mod, b_mod).reshape(DEPTH * SUBLANES, 1, 6 * D_MODEL)
    rope = _rope_tables()
    w_gu_flat = w_gu.reshape(DEPTH * N_EXPERTS, D_MODEL, 2 * EXPERT_DIM)
    w_down_flat = w_down.reshape(DEPTH * N_EXPERTS, EXPERT_DIM, D_MODEL)
    s_zero = jnp.zeros((BATCH, GLA_HEADS // 2, 4, GLA_VAL_DIM, LANES), F32)

    for l in range(DEPTH):
        last = l == DEPTH - 1
        t_rows = T_LAT if last else T_ALL
        lam_init = 0.8 - 0.6 * math.exp(-0.3 * l)
        lq1, lk1, lq2, lk2 = diff_lambda[l].astype(F32)
        lam = jnp.exp(jnp.sum(lq1 * lk1)) - jnp.exp(jnp.sum(lq2 * lk2)) + lam_init
        lam_row = jnp.full((1, LANES), lam, F32)

        w_main, w_lr = _prep_w_in(w_in[l])
        p, lr = _projection(x_lat, x_ctx, mod, l, w_main, w_lr, skip_ctx_gates=last)

        gla_w = _gla_gate_weights(gla_w_gate[l], gla_b_gate[l])
        gla_gain = gla_norm[l].reshape(1, GLA_VAL_DIM)
        diff_gain = diff_norm[l].reshape(1, DIFF_V_DIM)
        y_gla_ctx, states = _gla(p, lr, *gla_w, gla_gain, s_zero, ctx=True)
        y_gla, _ = _gla(p, lr, *gla_w, gla_gain, states, ctx=False)
        ys_lat = (_na_lat(p, _na_bias_tiles(na_rpb[l])), y_gla, _fnet(p, ctx=False),
                  _diff_lat(p, rope, lam_row, diff_gain, lam_init))
        ys_ctx = None
        if not last:
            ys_ctx = (_na_ctx(p), y_gla_ctx, _fnet(p, ctx=True), _diff_ctx(p, lam_row, diff_gain, lam_init))

        m = _merge(p, ys_lat, ys_ctx, w_branch[l].astype(BF16))
        x1 = _outproj(m, x_lat, x_ctx, mod, l, w_out[l].astype(BF16), ln1_g[l].reshape(1, -1),
                      ln1_b[l].reshape(1, -1), t_rows)
        router_w = _router_weights(w_group[l], b_group[l], w_router[l], b_router[l])
        x_lat = _moe(x1, mod, l, router_w, w_gu_flat, w_down_flat, ln2_g[l].reshape(1, -1),
                     ln2_b[l].reshape(1, -1), t_rows)
        x_ctx = None
    return x_lat.reshape(BATCH, SEQ, D_MODEL)
```

```python
import functools
import math

import numpy as np
import jax
import jax.numpy as jnp
from jax import lax
from jax.experimental import pallas as pl
from jax.experimental.pallas import tpu as pltpu

F32 = jnp.float32
BF16 = jnp.bfloat16

D_MODEL = 2048
BATCH = 4
SEQ = 2048
DEPTH = 2
GRID_W = 64
CTX_LEN = 256
N_BRANCH = 4
MIX_W = D_MODEL // N_BRANCH
NA_HEADS = 4
NA_HEAD_DIM = MIX_W // NA_HEADS
NA_WIN_ROWS = 8
NA_WIN_COLS = 16
GLA_HEADS = 4
GLA_VAL_DIM = MIX_W // GLA_HEADS
GLA_KEY_DIM = GLA_VAL_DIM // 2
GLA_RANK = 16
GLA_TAU = 16.0
GLA_CHUNK = 64
FNET_GROUPS = 4
FNET_GROUP_DIM = MIX_W // FNET_GROUPS
DIFF_HEADS = 4
DIFF_V_DIM = MIX_W // DIFF_HEADS
DIFF_QK_DIM = DIFF_V_DIM // 2
ROPE_BASE = 10000.0
N_GROUPS = 4
EXPERTS_PER_GROUP = 8
N_EXPERTS = N_GROUPS * EXPERTS_PER_GROUP
EXPERT_DIM = D_MODEL // 4
DEEPNORM_ALPHA = (2 * DEPTH) ** 0.25
LN_EPS = 1e-6
RMS_EPS = 1e-5
NEG_INF = -1e30

GLA_QK_W = GLA_HEADS * GLA_KEY_DIM
DIFF_QK_W = DIFF_HEADS * 2 * DIFF_QK_DIM
KV_SPLITS = (MIX_W, MIX_W, GLA_QK_W, MIX_W, 2 * GLA_RANK, DIFF_QK_W, MIX_W)
Q_SPLITS = (MIX_W, GLA_QK_W, MIX_W, MIX_W, DIFF_QK_W)
GATE_W = N_BRANCH * D_MODEL
ALL_SPLITS = KV_SPLITS + Q_SPLITS + (GATE_W,)

LANES = 128
SUBLANES = 8
VMEM_LIMIT = 56 * 1024 * 1024

T_LAT = BATCH * SEQ
T_CTX = BATCH * CTX_LEN
T_ALL = T_LAT + T_CTX
ROW_SLABS = D_MODEL // LANES

CB_GATES = 0
CB_NA_K = 64
CB_NA_V = 68
CB_GLA_K = 72
CB_GLA_V = 74
CB_DIFF_K = 78
CB_DIFF_V = 82
CB_NA_Q = 86
CB_GLA_Q = 90
CB_GLA_R = 92
CB_FNET_U = 96
CB_DIFF_Q = 100
P_COLS = 104 * LANES
_PIECE_ORDER = (12, 0, 1, 2, 3, 5, 6, 7, 8, 9, 10, 11)
_LR_PIECE = 4

PROJ_TM = 1024
PROJ_TN = 1024
ROW_TM = 256
EXPERT_TM = 256
NA_QROWS = 8
NA_KROWS = 16
DIFF_QB = 512
DIFF_SUB = 256
CTX_BLK0 = T_LAT // CTX_LEN


def _cparams(sem, vmem=VMEM_LIMIT):
    return pltpu.CompilerParams(dimension_semantics=sem, vmem_limit_bytes=vmem)


def _dot(a, b):
    return jnp.dot(a, b, preferred_element_type=F32)


def _dot_nt(a, b):
    return lax.dot_general(a, b, (((1,), (1,)), ((), ())), preferred_element_type=F32)


def _dot_tn(a, b):
    return lax.dot_general(a, b, (((0,), (0,)), ((), ())), preferred_element_type=F32)


def _sigmoid(x):
    return 1.0 / (1.0 + jnp.exp(-x))


def _ln_rows(x):
    mu = jnp.mean(x, -1, keepdims=True)
    xc = x - mu
    var = jnp.mean(xc * xc, -1, keepdims=True)
    return xc * lax.rsqrt(var + LN_EPS)


def _rms_rows(x, gain):
    return x * lax.rsqrt(jnp.mean(x * x, -1, keepdims=True) + RMS_EPS) * gain


def _mod_row(tile_rows):
    per_batch = SEQ // tile_rows
    return lambda i: jnp.minimum(i // per_batch, BATCH)


MOD_TK = 256


def _mod_body(c_ref, w_ref, b_ref, o_ref):
    @pl.when(pl.program_id(1) == 0)
    def _():
        o_ref[...] = jnp.broadcast_to(b_ref[...], o_ref.shape)

    c = c_ref[...]
    s = (c * _sigmoid(c)).astype(BF16)
    o_ref[...] += _dot(s, w_ref[...].astype(BF16))


def _modulation(c8, w_mod, b_mod):
    depth, d, cols = w_mod.shape
    return pl.pallas_call(
        _mod_body,
        grid=(depth, d // MOD_TK),
        in_specs=[pl.BlockSpec((SUBLANES, MOD_TK), lambda l, k: (0, k)),
                  pl.BlockSpec((None, MOD_TK, cols), lambda l, k: (l, k, 0)),
                  pl.BlockSpec((None, 1, cols), lambda l, k: (l, 0, 0))],
        out_specs=pl.BlockSpec((None, SUBLANES, cols), lambda l, k: (l, 0, 0)),
        out_shape=jax.ShapeDtypeStruct((depth, SUBLANES, cols), F32),
        compiler_params=_cparams(("arbitrary", "arbitrary")),
    )(c8, w_mod, b_mod.reshape(depth, 1, cols))


PROJ_LAT_TILES = T_LAT // PROJ_TM


def _proj_body(*refs, split_input, skip_ctx_gates):
    if split_input:
        x_ref, xc_ref, sh_ref, sc_ref, w_ref, wlr_ref, p_ref, lr_ref, h_scr = refs
    else:
        x_ref, sh_ref, sc_ref, w_ref, wlr_ref, p_ref, lr_ref, h_scr = refs
    i = pl.program_id(0)
    j = pl.program_id(1)

    @pl.when(j == 0)
    def _():
        def chunk(c, carry):
            r = pl.ds(pl.multiple_of(c * ROW_TM, ROW_TM), ROW_TM)
            xv = x_ref[r, :]
            if split_input:
                xv = jnp.where(i < PROJ_LAT_TILES, xv, xc_ref[r, :])
            h = _ln_rows(xv) * (1.0 + sc_ref[...]) + sh_ref[...]
            hb = h.astype(BF16)
            h_scr[r, :] = hb
            lr_ref[r, :] = _dot(hb, wlr_ref[...])
            return carry
        lax.fori_loop(0, PROJ_TM // ROW_TM, chunk, 0)

    def project():
        p_ref[...] = _dot(h_scr[...], w_ref[...]).astype(BF16)

    if skip_ctx_gates:
        unused = (i >= PROJ_LAT_TILES) & (j < GATE_W // PROJ_TN)
        pl.when(jnp.logical_not(unused))(project)

        @pl.when(unused)
        def _():
            p_ref[...] = jnp.zeros_like(p_ref)
    else:
        project()


def _projection(x_lat, x_ctx, mod, layer, w_main, w_lr, *, skip_ctx_gates):
    split_input = x_ctx is not None
    brow = _mod_row(PROJ_TM)
    mrow = lambda i: layer * SUBLANES + brow(i)
    x_specs = [pl.BlockSpec((PROJ_TM, D_MODEL), lambda i, j: (jnp.minimum(i, PROJ_LAT_TILES - 1), 0))
               if split_input else pl.BlockSpec((PROJ_TM, D_MODEL), lambda i, j: (i, 0))]
    xs = [x_lat]
    if split_input:
        assert x_ctx.shape[0] == PROJ_TM
        x_specs.append(pl.BlockSpec((PROJ_TM, D_MODEL), lambda i, j: (0, 0), pipeline_mode=pl.Buffered(1)))
        xs.append(x_ctx)
    return pl.pallas_call(
        functools.partial(_proj_body, split_input=split_input, skip_ctx_gates=skip_ctx_gates),
        grid=(T_ALL // PROJ_TM, P_COLS // PROJ_TN),
        in_specs=x_specs + [pl.BlockSpec((None, 1, D_MODEL), lambda i, j: (mrow(i), 0, 0)),
                            pl.BlockSpec((None, 1, D_MODEL), lambda i, j: (mrow(i), 0, 1)),
                            pl.BlockSpec((D_MODEL, PROJ_TN), lambda i, j: (0, j)),
                            pl.BlockSpec((D_MODEL, LANES), lambda i, j: (0, 0))],
        out_specs=[pl.BlockSpec((PROJ_TM, PROJ_TN), lambda i, j: (i, j)),
                   pl.BlockSpec((PROJ_TM, LANES), lambda i, j: (i, 0))],
        out_shape=[jax.ShapeDtypeStruct((T_ALL, P_COLS), BF16),
                   jax.ShapeDtypeStruct((T_ALL, LANES), F32)],
        scratch_shapes=[pltpu.VMEM((PROJ_TM, D_MODEL), BF16)],
        compiler_params=_cparams(("arbitrary", "arbitrary")),
    )(*xs, mod, mod, w_main, w_lr)


def _na_key_row_start(rb):
    return jnp.clip(rb * NA_QROWS - NA_WIN_ROWS // 2, 0, SEQ // GRID_W - NA_KROWS)


def _softmax_pv(s_list, v_list):
    m = s_list[0].max(-1, keepdims=True)
    for s in s_list[1:]:
        m = jnp.maximum(m, s.max(-1, keepdims=True))
    den = None
    o = None
    for s, v in zip(s_list, v_list):
        p = jnp.exp(s - m)
        ps = p.sum(-1, keepdims=True)
        den = ps if den is None else den + ps
        pv = _dot(p.astype(BF16), v)
        o = pv if o is None else o + pv
    return o / den


def _na_lat_body(q_ref, k_ref, v_ref, kc_ref, vc_ref, tile_ref, o_ref, bias_ref):
    rb = pl.program_id(1)
    krows = NA_KROWS * GRID_W
    grid_rows = SEQ // GRID_W

    @pl.when(pl.program_id(2) == 0)
    def _():
        left = lax.broadcasted_iota(jnp.int32, (GRID_W, LANES), 1) < GRID_W
        k0 = _na_key_row_start(rb)
        for qr in range(NA_QROWS):
            r = rb * NA_QROWS + qr
            r0 = jnp.clip(r - NA_WIN_ROWS // 2, 0, grid_rows - NA_WIN_ROWS)
            for kp in range(NA_KROWS // 2):
                idx = []
                for kk in range(2):
                    kr = k0 + 2 * kp + kk
                    inside = (kr >= r0) & (kr < r0 + NA_WIN_ROWS)
                    idx.append(jnp.where(inside, kr - r + NA_WIN_ROWS - 1, 2 * NA_WIN_ROWS - 1))
                bias_ref[qr * GRID_W:(qr + 1) * GRID_W, kp * LANES:(kp + 1) * LANES] = jnp.where(
                    left, tile_ref[idx[0]], tile_ref[idx[1]])

    start = pl.multiple_of(_na_key_row_start(rb) * GRID_W, 4 * GRID_W)
    k = k_ref[pl.ds(start, krows), :]
    v = v_ref[pl.ds(start, krows), :]
    kc = kc_ref[...]
    vc = vc_ref[...]
    scale = NA_HEAD_DIM ** -0.5
    half = NA_QROWS * GRID_W // 2
    for part in range(2):
        rows = slice(part * half, (part + 1) * half)
        q = q_ref[rows, :]
        s_lat = _dot_nt(q, k) * scale + bias_ref[rows, :]
        s_ctx = _dot_nt(q, kc) * scale
        o_ref[rows, :] = _softmax_pv([s_lat, s_ctx], [v, vc]).astype(BF16)


def _na_ctx_body(q_ref, kc_ref, vc_ref, o_ref):
    s = _dot_nt(q_ref[...], kc_ref[...]) * (NA_HEAD_DIM ** -0.5)
    o_ref[...] = _softmax_pv([s], [vc_ref[...]]).astype(BF16)


def _na_bias_tiles(rpb):
    n_dr = 2 * NA_WIN_ROWS - 1
    sel_c = np.zeros((GRID_W, GRID_W, 2 * NA_WIN_COLS - 1), np.float32)
    ok_c = np.zeros((GRID_W, GRID_W), bool)
    for qc in range(GRID_W):
        w0 = int(np.clip(qc - NA_WIN_COLS // 2, 0, GRID_W - NA_WIN_COLS))
        for kc in range(GRID_W):
            if w0 <= kc < w0 + NA_WIN_COLS:
                ok_c[qc, kc] = True
                sel_c[qc, kc, int(np.clip(kc - qc + NA_WIN_COLS - 1, 0, 2 * NA_WIN_COLS - 2))] = 1.0
    t = jnp.einsum('xyc,hac->haxy', jnp.asarray(sel_c), rpb.astype(F32), precision=lax.Precision.HIGHEST)
    t = jnp.where(jnp.asarray(ok_c)[None, None], t, NEG_INF)
    t = jnp.concatenate([t, jnp.full((NA_HEADS, 1, GRID_W, GRID_W), NEG_INF, F32)], axis=1)
    assert t.shape[1] == n_dr + 1
    return jnp.concatenate([t, t], axis=-1)


def _na_lat(p, tiles):
    qrows = NA_QROWS * GRID_W
    n_rb = SEQ // qrows
    return pl.pallas_call(
        _na_lat_body,
        grid=(NA_HEADS, n_rb, BATCH),
        in_specs=[pl.BlockSpec((qrows, LANES), lambda h, rb, b: (b * n_rb + rb, CB_NA_Q + h)),
                  pl.BlockSpec((SEQ, LANES), lambda h, rb, b: (b, CB_NA_K + h)),
                  pl.BlockSpec((SEQ, LANES), lambda h, rb, b: (b, CB_NA_V + h)),
                  pl.BlockSpec((CTX_LEN, LANES), lambda h, rb, b: (CTX_BLK0 + b, CB_NA_K + h)),
                  pl.BlockSpec((CTX_LEN, LANES), lambda h, rb, b: (CTX_BLK0 + b, CB_NA_V + h)),
                  pl.BlockSpec((None, 2 * NA_WIN_ROWS, GRID_W, LANES), lambda h, rb, b: (h, 0, 0, 0))],
        out_specs=pl.BlockSpec((qrows, LANES), lambda h, rb, b: (b * n_rb + rb, h)),
        out_shape=jax.ShapeDtypeStruct((T_LAT, MIX_W), BF16),
        scratch_shapes=[pltpu.VMEM((qrows, NA_KROWS * GRID_W), F32)],
        compiler_params=_cparams(("arbitrary",) * 3),
    )(p, p, p, p, p, tiles)


def _na_ctx(p):
    return pl.pallas_call(
        _na_ctx_body,
        grid=(BATCH, NA_HEADS),
        in_specs=[pl.BlockSpec((CTX_LEN, LANES), lambda b, h: (CTX_BLK0 + b, CB_NA_Q + h)),
                  pl.BlockSpec((CTX_LEN, LANES), lambda b, h: (CTX_BLK0 + b, CB_NA_K + h)),
                  pl.BlockSpec((CTX_LEN, LANES), lambda b, h: (CTX_BLK0 + b, CB_NA_V + h))],
        out_specs=pl.BlockSpec((CTX_LEN, LANES), lambda b, h: (b, h)),
        out_shape=jax.ShapeDtypeStruct((T_CTX, MIX_W), BF16),
        compiler_params=_cparams(("arbitrary",) * 2),
    )(p, p, p)


def _log_sigmoid(z):
    return jnp.minimum(z, 0.0) - jnp.log1p(jnp.exp(-jnp.abs(z)))


GLA_GROUP = 256


def _gla_body(q_ref, k_ref, v_ref, r_ref, lr_ref, wgf_ref, wgb_ref, bgf_ref, bgb_ref, gain_ref, s0_ref,
              y_ref, sfin_ref, la_scr, o_scr, qh_scr, ut_scr, sp_scr, g_scr, st_scr, *, n):
    c = GLA_CHUNK
    nc = n // c
    grp = GLA_GROUP
    cpg = grp // c
    lr = lr_ref[...].astype(BF16)
    la_scr[0] = _log_sigmoid(_dot(lr, wgf_ref[...]) + bgf_ref[...]) / GLA_TAU
    la_scr[1] = _log_sigmoid(_dot(lr, wgb_ref[...]) + bgb_ref[...]) / GLA_TAU
    o_scr[...] = jnp.zeros_like(o_scr)

    lane = lax.broadcasted_iota(jnp.int32, (grp, LANES), 1)
    head_mask = (lane < GLA_KEY_DIM, lane >= GLA_KEY_DIM)
    ri = lax.broadcasted_iota(jnp.int32, (grp, grp), 0)
    ci = lax.broadcasted_iota(jnp.int32, (grp, grp), 1)
    same_chunk = lax.shift_right_logical(ri, 6) == lax.shift_right_logical(ci, 6)
    assert c == 1 << 6
    tri_mask = (same_chunk & (ri >= ci), same_chunk & (ci >= ri))
    tri_mat = tuple(jnp.where(m, 1.0, 0.0).astype(BF16) for m in tri_mask)
    chunk_ones = jnp.where(same_chunk, 1.0, 0.0).astype(BF16)
    scale = GLA_KEY_DIM ** -0.5

    def group(gi, carry):
        r = pl.ds(pl.multiple_of(gi * grp, grp), grp)
        q = q_ref[r, :].astype(F32)
        k = k_ref[r, :].astype(F32)
        for d in range(2):
            la = la_scr[d, r, :]
            la_hi = la.astype(BF16)
            la_lo = (la - la_hi.astype(F32)).astype(BF16)
            cum = _dot(tri_mat[d], la_hi) + _dot(tri_mat[d], la_lo)
            tot = _dot(chunk_ones, la_hi) + _dot(chunk_ones, la_lo)
            qf = q * jnp.exp(cum) * scale
            k_in = (k * jnp.exp(-cum)).astype(BF16)
            k_end = k * jnp.exp(tot - cum)
            g = jnp.exp(tot)
            for ch in range(cpg):
                g_scr[d, pl.ds(gi * cpg + ch, 1), :] = g[ch * c:ch * c + 1, :]
            for hh in range(2):
                chain = hh * 2 + d
                cols = slice(hh * GLA_VAL_DIM, (hh + 1) * GLA_VAL_DIM)
                qh = jnp.where(head_mask[hh], qf, 0.0).astype(BF16)
                qh_scr[chain, r, :] = qh
                a = jnp.where(tri_mask[d], _dot_nt(qh, k_in), 0.0)
                vh = v_ref[r, cols]
                o_scr[r, cols] = o_scr[r, cols] + _dot(a.astype(BF16), vh)
                kh = jnp.where(head_mask[hh], k_end, 0.0).astype(BF16)
                for ch in range(cpg):
                    rows = slice(ch * c, (ch + 1) * c)
                    ut_scr[chain, gi * cpg + ch] = _dot_tn(vh[rows], kh[rows])
        return carry

    lax.fori_loop(0, n // grp, group, 0)

    st_scr[...] = s0_ref[...]

    def scan(i, carry):
        for d in range(2):
            cc = i if d == 0 else nc - 1 - i
            g = g_scr[d, pl.ds(cc, 1), :]
            for hh in range(2):
                chain = hh * 2 + d
                st = st_scr[chain]
                sp_scr[chain, cc] = st.astype(BF16)
                st_scr[chain] = g * st + ut_scr[chain, cc]
        return carry

    lax.fori_loop(0, nc, scan, 0)
    sfin_ref[...] = st_scr[...]

    def inter(cc, carry):
        r = pl.ds(pl.multiple_of(cc * c, c), c)
        for hh in range(2):
            cols = slice(hh * GLA_VAL_DIM, (hh + 1) * GLA_VAL_DIM)
            o = (_dot_nt(qh_scr[hh * 2, r, :], sp_scr[hh * 2, cc])
                 + _dot_nt(qh_scr[hh * 2 + 1, r, :], sp_scr[hh * 2 + 1, cc]))
            o_scr[r, cols] = o_scr[r, cols] + o
        return carry

    lax.fori_loop(0, nc, inter, 0, unroll=4)
    for hh in range(2):
        cols = slice(hh * GLA_VAL_DIM, (hh + 1) * GLA_VAL_DIM)
        rr = r_ref[:, cols].astype(F32)
        y_ref[:, cols] = (_rms_rows(o_scr[:, cols], gain_ref[...]) * (rr * _sigmoid(rr))).astype(BF16)


def _gla(p, lr, wg_f, wg_b, bg_f, bg_b, gain, s0, *, ctx):
    n = CTX_LEN if ctx else SEQ
    blk0 = CTX_BLK0 if ctx else 0
    pairs = GLA_HEADS // 2
    st_spec = pl.BlockSpec((None, None, 4, GLA_VAL_DIM, LANES), lambda b, pr: (b, pr, 0, 0, 0))
    return pl.pallas_call(
        functools.partial(_gla_body, n=n),
        grid=(BATCH, pairs),
        in_specs=[pl.BlockSpec((n, LANES), lambda b, pr: (blk0 + b, CB_GLA_Q + pr)),
                  pl.BlockSpec((n, LANES), lambda b, pr: (blk0 + b, CB_GLA_K + pr)),
                  pl.BlockSpec((n, 2 * LANES), lambda b, pr: (blk0 + b, CB_GLA_V // 2 + pr)),
                  pl.BlockSpec((n, 2 * LANES), lambda b, pr: (blk0 + b, CB_GLA_R // 2 + pr)),
                  pl.BlockSpec((n, LANES), lambda b, pr: (blk0 + b, 0)),
                  pl.BlockSpec((LANES, LANES), lambda b, pr: (0, pr)),
                  pl.BlockSpec((LANES, LANES), lambda b, pr: (0, pr)),
                  pl.BlockSpec((1, LANES), lambda b, pr: (0, pr)),
                  pl.BlockSpec((1, LANES), lambda b, pr: (0, pr)),
                  pl.BlockSpec((1, GLA_VAL_DIM), lambda b, pr: (0, 0)),
                  st_spec],
        out_specs=[pl.BlockSpec((n, 2 * LANES), lambda b, pr: (b, pr)), st_spec],
        out_shape=[jax.ShapeDtypeStruct((BATCH * n, MIX_W), BF16),
                   jax.ShapeDtypeStruct((BATCH, pairs, 4, GLA_VAL_DIM, LANES), F32)],
        scratch_shapes=[pltpu.VMEM((2, n, LANES), F32),
                        pltpu.VMEM((n, 2 * LANES), F32),
                        pltpu.VMEM((4, n, LANES), BF16),
                        pltpu.VMEM((4, n // GLA_CHUNK, GLA_VAL_DIM, LANES), F32),
                        pltpu.VMEM((4, n // GLA_CHUNK, GLA_VAL_DIM, LANES), BF16),
                        pltpu.VMEM((2, max(n // GLA_CHUNK, SUBLANES), LANES), F32),
                        pltpu.VMEM((4, GLA_VAL_DIM, LANES), F32)],
        compiler_params=_cparams(("arbitrary",) * 2),
    )(p, p, p, p, lr, wg_f, wg_b, bg_f, bg_b, gain, s0)


def _gla_gate_weights(w_gate, b_gate):
    wf = jnp.zeros((LANES, GLA_QK_W), F32).at[:GLA_RANK].set(w_gate[0])
    wb = jnp.zeros((LANES, GLA_QK_W), F32).at[GLA_RANK:2 * GLA_RANK].set(w_gate[1])
    return wf.astype(BF16), wb.astype(BF16), b_gate[0:1], b_gate[1:2]


def _dft_tables(n):
    def cs(m):
        idx = (np.arange(m)[:, None] * np.arange(m)[None, :]) % m
        ang = 2.0 * np.pi * idx / m
        return np.cos(ang) / np.sqrt(m), np.sin(ang) / np.sqrt(m)
    cn, sn = cs(n)
    cg, sg = cs(FNET_GROUP_DIM)
    as_bf16 = lambda a: jnp.asarray(a, F32).astype(BF16)
    return as_bf16(cn), as_bf16(sn), as_bf16(np.concatenate([cg, sg], axis=1))


def _fnet_body(u_ref, cn_ref, sn_ref, csg_ref, o_ref):
    t = _dot(u_ref[...], csg_ref[...]).astype(BF16)
    y = _dot(cn_ref[...], t[:, :FNET_GROUP_DIM]) - _dot(sn_ref[...], t[:, FNET_GROUP_DIM:])
    o_ref[...] = y.astype(BF16)


def _fnet(p, *, ctx):
    n = CTX_LEN if ctx else SEQ
    blk0 = CTX_BLK0 if ctx else 0
    cn, sn, csg = _dft_tables(n)
    return pl.pallas_call(
        _fnet_body,
        grid=(BATCH, FNET_GROUPS),
        in_specs=[pl.BlockSpec((n, LANES), lambda b, g: (blk0 + b, CB_FNET_U + g)),
                  pl.BlockSpec((n, n), lambda b, g: (0, 0)),
                  pl.BlockSpec((n, n), lambda b, g: (0, 0)),
                  pl.BlockSpec((FNET_GROUP_DIM, 2 * FNET_GROUP_DIM), lambda b, g: (0, 0))],
        out_specs=pl.BlockSpec((n, LANES), lambda b, g: (b, g)),
        out_shape=jax.ShapeDtypeStruct((BATCH * n, MIX_W), BF16),
        compiler_params=_cparams(("arbitrary",) * 2),
    )(p, cn, sn, csg)


def _rope_tables():
    half = DIFF_QK_DIM // 2
    inv = ROPE_BASE ** (-jnp.arange(0, half, 2, dtype=F32) / half)
    t = jnp.arange(SEQ)
    ang_r = (t // GRID_W).astype(F32)[:, None] * inv
    ang_c = (t % GRID_W).astype(F32)[:, None] * inv
    cos = jnp.concatenate([jnp.cos(ang_r)] * 2 + [jnp.cos(ang_c)] * 2, axis=1)
    sin = jnp.concatenate([jnp.sin(ang_r)] * 2 + [jnp.sin(ang_c)] * 2, axis=1)
    rot = np.zeros((LANES, LANES), np.float32)
    q4 = half // 2
    for base in range(0, LANES, half):
        for j in range(q4):
            rot[base + j + q4, base + j] = -1.0
            rot[base + j, base + j + q4] = 1.0
    return jnp.tile(cos, (1, 2)), jnp.tile(sin, (1, 2)), jnp.asarray(rot, BF16)


def _rope(x, cos, sin, rot):
    return x.astype(F32) * cos + _dot(x, rot) * sin


def _diff_out(q, k_list, v_list, lam, gain, out_scale):
    q = q * (DIFF_QK_DIM ** -0.5)
    lane = lax.broadcasted_iota(jnp.int32, q.shape, 1)
    o = None
    for sub, coef in ((lane < DIFF_QK_DIM, None), (lane >= DIFF_QK_DIM, lam)):
        qs = jnp.where(sub, q, 0.0).astype(BF16)
        os = _softmax_pv([_dot_nt(qs, k) for k in k_list], v_list)
        o = os if coef is None else o - coef * os
    return (_rms_rows(o, gain) * out_scale).astype(BF16)


def _diff_lat_body(q_ref, k_ref, v_ref, kc_ref, vc_ref, cosq_ref, sinq_ref, cosk_ref, sink_ref, rot_ref,
                   lam_ref, gain_ref, o_ref, k_scr, *, out_scale):
    @pl.when(pl.program_id(2) == 0)
    def _():
        k_scr[...] = _rope(k_ref[...], cosk_ref[...], sink_ref[...], rot_ref[...]).astype(BF16)

    for part in range(DIFF_QB // DIFF_SUB):
        rows = slice(part * DIFF_SUB, (part + 1) * DIFF_SUB)
        q = _rope(q_ref[rows, :], cosq_ref[rows, :], sinq_ref[rows, :], rot_ref[...])
        o_ref[rows, :] = _diff_out(q, [kc_ref[...], k_scr[...]], [vc_ref[...], v_ref[...]], lam_ref[:, 0:1],
                                   gain_ref[...], out_scale)


def _diff_ctx_body(q_ref, kc_ref, vc_ref, lam_ref, gain_ref, o_ref, *, out_scale):
    o_ref[...] = _diff_out(q_ref[...].astype(F32), [kc_ref[...]], [vc_ref[...]], lam_ref[:, 0:1],
                           gain_ref[...], out_scale)


def _diff_lat(p, rope, lam_row, gain, lam_init):
    cos, sin, rot = rope
    nqb = SEQ // DIFF_QB
    const = lambda b, h, qb: (0, 0)
    return pl.pallas_call(
        functools.partial(_diff_lat_body, out_scale=1.0 - lam_init),
        grid=(BATCH, DIFF_HEADS, nqb),
        in_specs=[pl.BlockSpec((DIFF_QB, LANES), lambda b, h, qb: (b * nqb + qb, CB_DIFF_Q + h)),
                  pl.BlockSpec((SEQ, LANES), lambda b, h, qb: (b, CB_DIFF_K + h)),
                  pl.BlockSpec((SEQ, LANES), lambda b, h, qb: (b, CB_DIFF_V + h)),
                  pl.BlockSpec((CTX_LEN, LANES), lambda b, h, qb: (CTX_BLK0 + b, CB_DIFF_K + h)),
                  pl.BlockSpec((CTX_LEN, LANES), lambda b, h, qb: (CTX_BLK0 + b, CB_DIFF_V + h)),
                  pl.BlockSpec((DIFF_QB, LANES), lambda b, h, qb: (qb, 0)),
                  pl.BlockSpec((DIFF_QB, LANES), lambda b, h, qb: (qb, 0)),
                  pl.BlockSpec((SEQ, LANES), const),
                  pl.BlockSpec((SEQ, LANES), const),
                  pl.BlockSpec((LANES, LANES), const),
                  pl.BlockSpec((1, LANES), const),
                  pl.BlockSpec((1, DIFF_V_DIM), const)],
        out_specs=pl.BlockSpec((DIFF_QB, LANES), lambda b, h, qb: (b * nqb + qb, h)),
        out_shape=jax.ShapeDtypeStruct((T_LAT, MIX_W), BF16),
        scratch_shapes=[pltpu.VMEM((SEQ, LANES), BF16)],
        compiler_params=_cparams(("arbitrary",) * 3),
    )(p, p, p, p, p, cos, sin, cos, sin, rot, lam_row, gain)


def _diff_ctx(p, lam_row, gain, lam_init):
    const = lambda b, h: (0, 0)
    return pl.pallas_call(
        functools.partial(_diff_ctx_body, out_scale=1.0 - lam_init),
        grid=(BATCH, DIFF_HEADS),
        in_specs=[pl.BlockSpec((CTX_LEN, LANES), lambda b, h: (CTX_BLK0 + b, CB_DIFF_Q + h)),
                  pl.BlockSpec((CTX_LEN, LANES), lambda b, h: (CTX_BLK0 + b, CB_DIFF_K + h)),
                  pl.BlockSpec((CTX_LEN, LANES), lambda b, h: (CTX_BLK0 + b, CB_DIFF_V + h)),
                  pl.BlockSpec((1, LANES), const),
                  pl.BlockSpec((1, DIFF_V_DIM), const)],
        out_specs=pl.BlockSpec((CTX_LEN, LANES), lambda b, h: (b, h)),
        out_shape=jax.ShapeDtypeStruct((T_CTX, MIX_W), BF16),
        compiler_params=_cparams(("arbitrary",) * 2),
    )(p, p, p, lam_row, gain)


MERGE_TM = 512
N_LAT_TILES = T_LAT // MERGE_TM


def _merge_body(*refs, with_ctx):
    gates = refs[0:N_BRANCH]
    y_lat = refs[N_BRANCH:2 * N_BRANCH]
    y_ctx = refs[2 * N_BRANCH:3 * N_BRANCH] if with_ctx else None
    wb_ref, m_ref = refs[-2], refs[-1]
    is_lat = pl.program_id(0) < N_LAT_TILES
    acc = None
    for i in range(N_BRANCH):
        y = y_lat[i][...]
        if with_ctx:
            y = jnp.where(is_lat, y, y_ctx[i][...])
        t = _sigmoid(gates[i][...].astype(F32)) * _dot(y, wb_ref[i])
        acc = t if acc is None else acc + t
    m_ref[...] = acc.astype(BF16)


def _merge(p, ys_lat, ys_ctx, w_branch):
    with_ctx = ys_ctx is not None
    t_rows = T_ALL if with_ctx else T_LAT
    gate_specs = [pl.BlockSpec((MERGE_TM, D_MODEL), functools.partial(lambda i, g: (i, g), g=g))
                  for g in range(N_BRANCH)]
    y_specs = [pl.BlockSpec((MERGE_TM, MIX_W), lambda i: (jnp.minimum(i, N_LAT_TILES - 1), 0))
               for _ in range(N_BRANCH)]
    ys = list(ys_lat)
    if with_ctx:
        y_specs += [pl.BlockSpec((MERGE_TM, MIX_W), lambda i: (jnp.maximum(i - N_LAT_TILES, 0), 0))
                    for _ in range(N_BRANCH)]
        ys += list(ys_ctx)
    return pl.pallas_call(
        functools.partial(_merge_body, with_ctx=with_ctx),
        grid=(t_rows // MERGE_TM,),
        in_specs=gate_specs + y_specs + [pl.BlockSpec((N_BRANCH, MIX_W, D_MODEL), lambda i: (0, 0, 0),
                                                      pipeline_mode=pl.Buffered(1))],
        out_specs=pl.BlockSpec((MERGE_TM, D_MODEL), lambda i: (i, 0)),
        out_shape=jax.ShapeDtypeStruct((t_rows, D_MODEL), BF16),
        compiler_params=_cparams(("arbitrary",)),
    )(p, p, p, p, *ys, w_branch)


def _residual_ln(x, gate, f, g, b):
    return _ln_rows(DEEPNORM_ALPHA * x + gate * f) * g + b


def _outproj_body(*refs, split_input):
    if split_input:
        m_ref, x_ref, xc_ref, gate_ref, w_ref, g_ref, b_ref, o_ref = refs
        x = jnp.where(pl.program_id(0) < N_LAT_TILES, x_ref[...], xc_ref[...])
    else:
        m_ref, x_ref, gate_ref, w_ref, g_ref, b_ref, o_ref = refs
        x = x_ref[...]
    f = _dot(m_ref[...], w_ref[...])
    o_ref[...] = _residual_ln(x, gate_ref[...], f, g_ref[...], b_ref[...])


def _outproj(m, x_lat, x_ctx, mod, layer, w_out, ln_g, ln_b, t_rows):
    split_input = x_ctx is not None
    brow = _mod_row(MERGE_TM)
    vec = pl.BlockSpec((1, D_MODEL), lambda i: (0, 0))
    rows = pl.BlockSpec((MERGE_TM, D_MODEL), lambda i: (i, 0))
    x_specs, xs = [rows], [x_lat]
    if split_input:
        x_specs = [pl.BlockSpec((MERGE_TM, D_MODEL), lambda i: (jnp.minimum(i, N_LAT_TILES - 1), 0)),
                   pl.BlockSpec((MERGE_TM, D_MODEL), lambda i: (jnp.maximum(i - N_LAT_TILES, 0), 0))]
        xs = [x_lat, x_ctx]
    return pl.pallas_call(
        functools.partial(_outproj_body, split_input=split_input),
        grid=(t_rows // MERGE_TM,),
        in_specs=[rows] + x_specs
        + [pl.BlockSpec((None, 1, D_MODEL), lambda i: (layer * SUBLANES + brow(i), 0, 2)),
           pl.BlockSpec((D_MODEL, D_MODEL), lambda i: (0, 0), pipeline_mode=pl.Buffered(1)), vec, vec],
        out_specs=rows,
        out_shape=jax.ShapeDtypeStruct((t_rows, D_MODEL), F32),
        compiler_params=_cparams(("arbitrary",)),
    )(m, *xs, mod, w_out, ln_g, ln_b)


RT_E1, RT_E2, RT_W1, RT_W2, RT_RANK1, RT_RANK2 = range(6)


def _first_lane_of_max(vals, lane_f):
    m = vals.max(-1, keepdims=True)
    idx = jnp.where(vals == m, lane_f, float(LANES)).min(-1, keepdims=True)
    return m, idx


def _router_body(x_ref, sh_ref, sc_ref, wr_ref, br_ref, h_ref, route_ref, cnt_ref, run_scr):
    @pl.when(pl.program_id(0) == 0)
    def _():
        run_scr[...] = jnp.zeros_like(run_scr)

    tm = ROW_TM
    h = _ln_rows(x_ref[...]) * (1.0 + sc_ref[...]) + sh_ref[...]
    hb = h.astype(BF16)
    for s in range(ROW_SLABS):
        h_ref[pl.ds(s, tm, stride=ROW_SLABS), :] = hb[:, s * LANES:(s + 1) * LANES].astype(F32)
    logits = _dot(hb, wr_ref[...]) + br_ref[...]
    lane = lax.broadcasted_iota(jnp.int32, (tm, LANES), 1)
    lane_f = lane.astype(F32)
    is_group = lane < N_GROUPS
    g_logits = jnp.where(is_group, logits, NEG_INF)
    g_max, g_sel = _first_lane_of_max(g_logits, lane_f)
    g_w = 1.0 / jnp.where(is_group, jnp.exp(logits - g_max), 0.0).sum(-1, keepdims=True)
    lo = N_GROUPS + EXPERTS_PER_GROUP * g_sel
    e_logits = jnp.where((lane_f >= lo) & (lane_f < lo + EXPERTS_PER_GROUP), logits, NEG_INF)
    v1, i1 = _first_lane_of_max(e_logits, lane_f)
    v2, i2 = _first_lane_of_max(jnp.where(lane_f == i1, NEG_INF, e_logits), lane_f)
    t = jnp.exp(v2 - v1)
    w1 = g_w / (1.0 + t)
    w2 = g_w * t / (1.0 + t)
    e1 = i1 - N_GROUPS
    e2 = i2 - N_GROUPS

    oh1 = lane_f == e1
    oh2 = lane_f == e2
    both = jnp.where(oh1 | oh2, 1.0, 0.0)
    ri = lax.broadcasted_iota(jnp.int32, (tm, tm), 0)
    ci = lax.broadcasted_iota(jnp.int32, (tm, tm), 1)
    before = jnp.where(ci < ri, 1.0, 0.0).astype(BF16)
    excl = _dot(before, both.astype(BF16)) + run_scr[0:1, :]
    rank1 = jnp.where(oh1, excl, 0.0).sum(-1, keepdims=True)
    rank2 = jnp.where(oh2, excl, 0.0).sum(-1, keepdims=True)
    run_scr[...] = run_scr[...] + both.sum(0, keepdims=True)
    cnt_ref[...] = run_scr[...]

    rec = jnp.zeros((tm, LANES), F32)
    for idx, val in ((RT_E1, e1), (RT_E2, e2), (RT_W1, w1), (RT_W2, w2), (RT_RANK1, rank1), (RT_RANK2, rank2)):
        rec = jnp.where(lane == idx, val, rec)
    route_ref[...] = rec


def _router(x1, mod, layer, w_r, b_r, t_rows):
    brow = _mod_row(ROW_TM)
    mrow = lambda i: layer * SUBLANES + brow(i)
    return pl.pallas_call(
        _router_body,
        grid=(t_rows // ROW_TM,),
        in_specs=[pl.BlockSpec((ROW_TM, D_MODEL), lambda i: (i, 0)),
                  pl.BlockSpec((None, 1, D_MODEL), lambda i: (mrow(i), 0, 3)),
                  pl.BlockSpec((None, 1, D_MODEL), lambda i: (mrow(i), 0, 4)),
                  pl.BlockSpec((D_MODEL, LANES), lambda i: (0, 0)),
                  pl.BlockSpec((1, LANES), lambda i: (0, 0))],
        out_specs=[pl.BlockSpec((ROW_TM * ROW_SLABS, LANES), lambda i: (i, 0)),
                   pl.BlockSpec((ROW_TM, LANES), lambda i: (i, 0)),
                   pl.BlockSpec((SUBLANES, LANES), lambda i: (0, 0))],
        out_shape=[jax.ShapeDtypeStruct((t_rows * ROW_SLABS, LANES), F32),
                   jax.ShapeDtypeStruct((t_rows, LANES), F32),
                   jax.ShapeDtypeStruct((SUBLANES, LANES), F32)],
        scratch_shapes=[pltpu.VMEM((SUBLANES, LANES), F32)],
        compiler_params=_cparams(("arbitrary",)),
    )(x1, mod, mod, w_r, b_r)


def _router_weights(w_group, b_group, w_router, b_router):
    wr = jnp.concatenate([w_group, jnp.transpose(w_router, (1, 0, 2)).reshape(D_MODEL, N_EXPERTS)], axis=1)
    br = jnp.concatenate([b_group, b_router.reshape(N_EXPERTS)])
    pad = LANES - wr.shape[1]
    return jnp.pad(wr, ((0, 0), (0, pad))).astype(BF16), jnp.pad(br, (0, pad)).reshape(1, LANES)


def _token_rows(ref, tok):
    return ref.at[pl.ds(pl.multiple_of(tok * ROW_SLABS, ROW_SLABS), ROW_SLABS), :]


DMA_UNROLL = 8


def _dispatch_body(pos_ref, h_ref, xs_in_ref, xs_ref, sem, *, t_rows):
    del xs_in_ref
    tt = ROW_TM
    base = pl.program_id(0) * tt

    def issue(j, carry):
        for k in range(2):
            pltpu.make_async_copy(_token_rows(h_ref, j), _token_rows(xs_ref, pos_ref[k * t_rows + base + j]),
                                  sem).start()
        return carry

    def drain(j, carry):
        pltpu.make_async_copy(_token_rows(h_ref, 0), _token_rows(xs_ref, 0), sem).wait()
        return carry

    lax.fori_loop(0, tt, issue, 0, unroll=DMA_UNROLL)
    lax.fori_loop(0, 2 * tt, drain, 0, unroll=DMA_UNROLL)


def _dispatch(pos, h, n_slots):
    t_rows = h.shape[0] // ROW_SLABS
    xs0 = jnp.zeros((n_slots * ROW_SLABS, LANES), F32)
    return pl.pallas_call(
        functools.partial(_dispatch_body, t_rows=t_rows),
        grid_spec=pltpu.PrefetchScalarGridSpec(
            num_scalar_prefetch=1,
            grid=(t_rows // ROW_TM,),
            in_specs=[pl.BlockSpec((ROW_TM * ROW_SLABS, LANES), lambda i, pos: (i, 0)),
                      pl.BlockSpec(memory_space=pl.ANY)],
            out_specs=pl.BlockSpec(memory_space=pl.ANY),
            scratch_shapes=[pltpu.SemaphoreType.DMA(())]),
        out_shape=jax.ShapeDtypeStruct(xs0.shape, F32),
        input_output_aliases={2: 0},
        compiler_params=_cparams(("arbitrary",)),
    )(pos, h, xs0)


def _expert_body(te_ref, nu_ref, nxt_ref, slot_ref, xs_ref, wgu_hbm, wd_hbm, ys_ref,
                 wgu_f, wd_f, wgu_b, wd_b, sem, *, layer):
    i = pl.program_id(0)
    tm = EXPERT_TM
    used = i < nu_ref[0]

    def fetch(expert, sl):
        e = layer * N_EXPERTS + expert
        return (pltpu.make_async_copy(wgu_hbm.at[e], wgu_f.at[sl], sem.at[0, sl]),
                pltpu.make_async_copy(wd_hbm.at[e], wd_f.at[sl], sem.at[1, sl]))

    @pl.when(used)
    def _():
        @pl.when((i == 0) | (te_ref[i] != te_ref[jnp.maximum(i - 1, 0)]))
        def _():
            sl = slot_ref[i]

            @pl.when(i == 0)
            def _():
                for cp in fetch(te_ref[0], 0):
                    cp.start()

            @pl.when(nxt_ref[i] >= 0)
            def _():
                for cp in fetch(nxt_ref[i], 1 - sl):
                    cp.start()

            for cp in fetch(te_ref[i], sl):
                cp.wait()

            def cast(c, carry):
                r = pl.ds(pl.multiple_of(c * ROW_TM, ROW_TM), ROW_TM)
                wgu_b[r, :] = wgu_f[sl, r, :].astype(BF16)
                return carry
            lax.fori_loop(0, D_MODEL // ROW_TM, cast, 0)
            wd_b[...] = wd_f[sl].astype(BF16)

        x = jnp.concatenate([xs_ref[pl.ds(s, tm, stride=ROW_SLABS), :].astype(BF16) for s in range(ROW_SLABS)],
                            axis=1)
        hgu = _dot(x, wgu_b[...])
        hg = hgu[:, :EXPERT_DIM]
        hu = hgu[:, EXPERT_DIM:]
        act = (hg * _sigmoid(hg) * hu).astype(BF16)
        y = _dot(act, wd_b[...])
        for s in range(ROW_SLABS):
            ys_ref[pl.ds(s, tm, stride=ROW_SLABS), :] = y[:, s * LANES:(s + 1) * LANES]

    @pl.when(jnp.logical_not(used))
    def _():
        ys_ref[...] = jnp.zeros_like(ys_ref)


def _experts(tile_expert, n_used, next_expert, tile_slot, xs, w_gu, w_down, layer):
    n_tiles = xs.shape[0] // (EXPERT_TM * ROW_SLABS)
    blk = EXPERT_TM * ROW_SLABS
    return pl.pallas_call(
        functools.partial(_expert_body, layer=layer),
        grid_spec=pltpu.PrefetchScalarGridSpec(
            num_scalar_prefetch=4,
            grid=(n_tiles,),
            in_specs=[pl.BlockSpec((blk, LANES), lambda i, te, nu, nx, sl: (jnp.minimum(i, nu[0] - 1), 0)),
                      pl.BlockSpec(memory_space=pl.ANY),
                      pl.BlockSpec(memory_space=pl.ANY)],
            out_specs=pl.BlockSpec((blk, LANES), lambda i, te, nu, nx, sl: (i, 0)),
            scratch_shapes=[pltpu.VMEM((2, D_MODEL, 2 * EXPERT_DIM), F32),
                            pltpu.VMEM((2, EXPERT_DIM, D_MODEL), F32),
                            pltpu.VMEM((D_MODEL, 2 * EXPERT_DIM), BF16),
                            pltpu.VMEM((EXPERT_DIM, D_MODEL), BF16),
                            pltpu.SemaphoreType.DMA((2, 2))]),
        out_shape=jax.ShapeDtypeStruct(xs.shape, F32),
        compiler_params=_cparams(("arbitrary",)),
    )(tile_expert, n_used, next_expert, tile_slot, xs, w_gu, w_down)


GATHER_PITCH = ROW_SLABS + SUBLANES


def _final_body(pos_ref, x_ref, ys_ref, route_ref, gate_ref, g_ref, b_ref, out_ref, buf, sem, *, t_rows):
    tm = ROW_TM
    i = pl.program_id(0)
    slot = lax.rem(i, 2)

    def gather_copy(tile, sl, t, k):
        dst = buf.at[sl, pl.ds(pl.multiple_of((k * tm + t) * GATHER_PITCH, SUBLANES), ROW_SLABS), :]
        return pltpu.make_async_copy(_token_rows(ys_ref, pos_ref[k * t_rows + tile * tm + t]), dst, sem.at[sl])

    def start_tile(tile, sl):
        def issue(t, carry):
            for k in range(2):
                gather_copy(tile, sl, t, k).start()
            return carry
        lax.fori_loop(0, tm, issue, 0, unroll=DMA_UNROLL // 2)

    @pl.when(i == 0)
    def _():
        start_tile(0, 0)

    @pl.when(i + 1 < pl.num_programs(0))
    def _():
        start_tile(i + 1, 1 - slot)

    def drain(t, carry):
        for k in range(2):
            gather_copy(i, slot, 0, k).wait()
        return carry
    lax.fori_loop(0, tm, drain, 0, unroll=DMA_UNROLL // 2)

    rt = route_ref[...]
    w1 = rt[:, RT_W1:RT_W1 + 1]
    w2 = rt[:, RT_W2:RT_W2 + 1]
    f = jnp.concatenate(
        [w1 * buf[slot, pl.ds(s, tm, stride=GATHER_PITCH), :]
         + w2 * buf[slot, pl.ds(tm * GATHER_PITCH + s, tm, stride=GATHER_PITCH), :] for s in range(ROW_SLABS)],
        axis=1)
    out_ref[...] = _residual_ln(x_ref[...], gate_ref[...], f, g_ref[...], b_ref[...])


def _final(pos, x1, ys, route, mod, layer, ln_g, ln_b, t_rows):
    brow = _mod_row(ROW_TM)
    vec = pl.BlockSpec((1, D_MODEL), lambda i, pos: (0, 0))
    return pl.pallas_call(
        functools.partial(_final_body, t_rows=t_rows),
        grid_spec=pltpu.PrefetchScalarGridSpec(
            num_scalar_prefetch=1,
            grid=(t_rows // ROW_TM,),
            in_specs=[pl.BlockSpec((ROW_TM, D_MODEL), lambda i, pos: (i, 0)),
                      pl.BlockSpec(memory_space=pl.ANY),
                      pl.BlockSpec((ROW_TM, LANES), lambda i, pos: (i, 0)),
                      pl.BlockSpec((None, 1, D_MODEL), lambda i, pos: (layer * SUBLANES + brow(i), 0, 5)),
                      vec, vec],
            out_specs=pl.BlockSpec((ROW_TM, D_MODEL), lambda i, pos: (i, 0)),
            scratch_shapes=[pltpu.VMEM((2, 2 * ROW_TM * GATHER_PITCH, LANES), F32),
                            pltpu.SemaphoreType.DMA((2,))]),
        out_shape=jax.ShapeDtypeStruct((t_rows, D_MODEL), F32),
        compiler_params=_cparams(("arbitrary",)),
    )(pos, x1, ys, route, mod, ln_g, ln_b)


def _moe(x1, mod, layer, router_w, w_gu, w_down, ln_g, ln_b, t_rows):
    h, route, cnt = _router(x1, mod, layer, *router_w, t_rows)
    n_tiles = -(-(2 * t_rows + N_EXPERTS * (EXPERT_TM - 1)) // EXPERT_TM)
    counts = cnt[0, :N_EXPERTS].astype(jnp.int32)
    tiles_per = (counts + EXPERT_TM - 1) // EXPERT_TM
    tile_end = jnp.cumsum(tiles_per)
    tile_start = tile_end - tiles_per
    n_used = tile_end[-1:]
    route_t = route.T
    e = route_t[RT_E1:RT_E2 + 1].astype(jnp.int32)
    rank = route_t[RT_RANK1:RT_RANK2 + 1].astype(jnp.int32)
    ids = jnp.arange(N_EXPERTS, dtype=jnp.int32)

    def lookup(table, idx):
        hit = idx[None] == ids.reshape((N_EXPERTS,) + (1,) * idx.ndim)
        return jnp.sum(jnp.where(hit, table.reshape((N_EXPERTS,) + (1,) * idx.ndim), 0), axis=0)

    pos = (lookup(tile_start, e) * EXPERT_TM + rank).reshape(-1)
    ti = jnp.minimum(jnp.arange(n_tiles, dtype=jnp.int32), n_used[0] - 1)
    tile_expert = jnp.sum((ti[:, None] >= tile_end[None, :]).astype(jnp.int32), axis=1)
    owns = tiles_per > 0
    later = jnp.where(owns[None, :] & (ids[None, :] > ids[:, None]), ids[None, :], N_EXPERTS)
    nxt_e = jnp.min(later, axis=1)
    nxt_e = jnp.where(nxt_e == N_EXPERTS, -1, nxt_e)
    order_e = jnp.cumsum(owns.astype(jnp.int32)) - 1
    next_expert = lookup(nxt_e, tile_expert)
    tile_slot = lookup(order_e, tile_expert) % 2
    xs = _dispatch(pos, h, n_tiles * EXPERT_TM)
    ys = _experts(tile_expert, n_used, next_expert, tile_slot, xs, w_gu, w_down, layer)
    return _final(pos, x1, ys, route, mod, layer, ln_g, ln_b, t_rows)


def _prep_w_in(w):
    offs = np.concatenate([[0], np.cumsum(ALL_SPLITS)])
    piece = lambda i: w[:, int(offs[i]):int(offs[i + 1])]
    w_main = jnp.concatenate([piece(i) for i in _PIECE_ORDER], axis=1).astype(BF16)
    w_lr = jnp.pad(piece(_LR_PIECE), ((0, 0), (0, LANES - 2 * GLA_RANK))).astype(BF16)
    return w_main, w_lr


def kernel(x, c, ctx, c_ctx, w_mod, b_mod, w_in, na_rpb, gla_w_gate, gla_b_gate, gla_norm, diff_lambda, diff_norm,
           w_branch, w_out, ln1_g, ln1_b, ln2_g, ln2_b, w_group, b_group, w_router, b_router, w_gu, w_down):
    x_lat, x_ctx = x.reshape(T_LAT, D_MODEL), ctx.reshape(T_CTX, D_MODEL)
    c8 =jnp.concatenate([c, c_ctx[None], jnp.zeros((SUBLANES - BATCH - 1, D_MODEL), F32)], axis=0)
    mod = _modulation(c8, w_mod, b_mod).reshape(DEPTH * SUBLANES, 1, 6 * D_MODEL)
    rope = _rope_tables()
    w_gu_flat = w_gu.reshape(DEPTH * N_EXPERTS, D_MODEL, 2 * EXPERT_DIM)
    w_down_flat = w_down.reshape(DEPTH * N_EXPERTS, EXPERT_DIM, D_MODEL)
    s_zero = jnp.zeros((BATCH, GLA_HEADS // 2, 4, GLA_VAL_DIM, LANES), F32)

    for l in range(DEPTH):
        last = l == DEPTH - 1
        t_rows = T_LAT if last else T_ALL
        lam_init = 0.8 - 0.6 * math.exp(-0.3 * l)
        lq1, lk1, lq2, lk2 = diff_lambda[l].astype(F32)
        lam = jnp.exp(jnp.sum(lq1 * lk1)) - jnp.exp(jnp.sum(lq2 * lk2)) + lam_init
        lam_row = jnp.full((1, LANES), lam, F32)

        w_main, w_lr = _prep_w_in(w_in[l])
        p, lr = _projection(x_lat, x_ctx, mod, l, w_main, w_lr, skip_ctx_gates=last)

        gla_w = _gla_gate_weights(gla_w_gate[l], gla_b_gate[l])
        gla_gain = gla_norm[l].reshape(1, GLA_VAL_DIM)
        diff_gain = diff_norm[l].reshape(1, DIFF_V_DIM)
        y_gla_ctx, states = _gla(p, lr, *gla_w, gla_gain, s_zero, ctx=True)
        y_gla, _ = _gla(p, lr, *gla_w, gla_gain, states, ctx=False)
        ys_lat = (_na_lat(p, _na_bias_tiles(na_rpb[l])), y_gla, _fnet(p, ctx=False),
                  _diff_lat(p, rope, lam_row, diff_gain, lam_init))
        ys_ctx = None
        if not last:
            ys_ctx = (_na_ctx(p), y_gla_ctx, _fnet(p, ctx=True), _diff_ctx(p, lam_row, diff_gain, lam_init))

        m = _merge(p, ys_lat, ys_ctx, w_branch[l].astype(BF16))
        x1 = _outproj(m, x_lat, x_ctx, mod, l, w_out[l].astype(BF16), ln1_g[l].reshape(1, -1),
                      ln1_b[l].reshape(1, -1), t_rows)
        router_w = _router_weights(w_group[l], b_group[l], w_router[l], b_router[l])
        x_lat = _moe(x1, mod, l, router_w, w_gu_flat, w_down_flat, ln2_g[l].reshape(1, -1),
                     ln2_b[l].reshape(1, -1), t_rows)
        x_ctx = None
    return x_lat.reshape(BATCH, SEQ, D_MODEL)
```

```python
import functools
import math

import numpy as np
import jax
import jax.numpy as jnp
from jax import lax
from jax.experimental import pallas as pl
from jax.experimental.pallas import tpu as pltpu

F32 = jnp.float32
BF16 = jnp.bfloat16

D_MODEL = 2048
BATCH = 4
SEQ = 2048
DEPTH = 2
GRID_W = 64
CTX_LEN = 256
N_BRANCH = 4
MIX_W = D_MODEL // N_BRANCH
NA_HEADS = 4
NA_HEAD_DIM = MIX_W // NA_HEADS
NA_WIN_ROWS = 8
NA_WIN_COLS = 16
GLA_HEADS = 4
GLA_VAL_DIM = MIX_W // GLA_HEADS
GLA_KEY_DIM = GLA_VAL_DIM // 2
GLA_RANK = 16
GLA_TAU = 16.0
GLA_CHUNK = 64
FNET_GROUPS = 4
FNET_GROUP_DIM = MIX_W // FNET_GROUPS
DIFF_HEADS = 4
DIFF_V_DIM = MIX_W // DIFF_HEADS
DIFF_QK_DIM = DIFF_V_DIM // 2
ROPE_BASE = 10000.0
N_GROUPS = 4
EXPERTS_PER_GROUP = 8
N_EXPERTS = N_GROUPS * EXPERTS_PER_GROUP
EXPERT_DIM = D_MODEL // 4
DEEPNORM_ALPHA = (2 * DEPTH) ** 0.25
LN_EPS = 1e-6
RMS_EPS = 1e-5
NEG_INF = -1e30

GLA_QK_W = GLA_HEADS * GLA_KEY_DIM
DIFF_QK_W = DIFF_HEADS * 2 * DIFF_QK_DIM
KV_SPLITS = (MIX_W, MIX_W, GLA_QK_W, MIX_W, 2 * GLA_RANK, DIFF_QK_W, MIX_W)
Q_SPLITS = (MIX_W, GLA_QK_W, MIX_W, MIX_W, DIFF_QK_W)
GATE_W = N_BRANCH * D_MODEL
ALL_SPLITS = KV_SPLITS + Q_SPLITS + (GATE_W,)

LANES = 128
SUBLANES = 8
VMEM_LIMIT = 56 * 1024 * 1024

T_LAT = BATCH * SEQ
T_CTX = BATCH * CTX_LEN
T_ALL = T_LAT + T_CTX
ROW_SLABS = D_MODEL // LANES

CB_GATES = 0
CB_NA_K = 64
CB_NA_V = 68
CB_GLA_K = 72
CB_GLA_V = 74
CB_DIFF_K = 78
CB_DIFF_V = 82
CB_NA_Q = 86
CB_GLA_Q = 90
CB_GLA_R = 92
CB_FNET_U = 96
CB_DIFF_Q = 100
P_COLS = 104 * LANES
_PIECE_ORDER = (12, 0, 1, 2, 3, 5, 6, 7, 8, 9, 10, 11)
_LR_PIECE = 4

PROJ_TM = 1024
PROJ_TN = 1024
ROW_TM = 256
EXPERT_TM = 256
NA_QROWS = 8
NA_KROWS = 16
DIFF_QB = 512
DIFF_SUB = 256
CTX_BLK0 = T_LAT // CTX_LEN


def _cparams(sem, vmem=VMEM_LIMIT):
    return pltpu.CompilerParams(dimension_semantics=sem, vmem_limit_bytes=vmem)


def _dot(a, b):
    return jnp.dot(a, b, preferred_element_type=F32)


def _dot_nt(a, b):
    return lax.dot_general(a, b, (((1,), (1,)), ((), ())), preferred_element_type=F32)


def _dot_tn(a, b):
    return lax.dot_general(a, b, (((0,), (0,)), ((), ())), preferred_element_type=F32)


def _sigmoid(x):
    return 1.0 / (1.0 + jnp.exp(-x))


def _ln_rows(x):
    mu = jnp.mean(x, -1, keepdims=True)
    xc = x - mu
    var = jnp.mean(xc * xc, -1, keepdims=True)
    return xc * lax.rsqrt(var + LN_EPS)


def _rms_rows(x, gain):
    return x * lax.rsqrt(jnp.mean(x * x, -1, keepdims=True) + RMS_EPS) * gain


def _mod_row(tile_rows):
    per_batch = SEQ // tile_rows
    return lambda i: jnp.minimum(i // per_batch, BATCH)


MOD_TK = 256


def _mod_body(c_ref, w_ref, b_ref, o_ref):
    @pl.when(pl.program_id(1) == 0)
    def _():
        o_ref[...] = jnp.broadcast_to(b_ref[...], o_ref.shape)

    c = c_ref[...]
    s = (c * _sigmoid(c)).astype(BF16)
    o_ref[...] += _dot(s, w_ref[...].astype(BF16))


def _modulation(c8, w_mod, b_mod):
    depth, d, cols = w_mod.shape
    return pl.pallas_call(
        _mod_body,
        grid=(depth, d // MOD_TK),
        in_specs=[pl.BlockSpec((SUBLANES, MOD_TK), lambda l, k: (0, k)),
                  pl.BlockSpec((None, MOD_TK, cols), lambda l, k: (l, k, 0)),
                  pl.BlockSpec((None, 1, cols), lambda l, k: (l, 0, 0))],
        out_specs=pl.BlockSpec((None, SUBLANES, cols), lambda l, k: (l, 0, 0)),
        out_shape=jax.ShapeDtypeStruct((depth, SUBLANES, cols), F32),
        compiler_params=_cparams(("arbitrary", "arbitrary")),
    )(c8, w_mod, b_mod.reshape(depth, 1, cols))


PROJ_LAT_TILES = T_LAT // PROJ_TM


def _proj_body(*refs, split_input, skip_ctx_gates):
    if split_input:
        x_ref, xc_ref, sh_ref, sc_ref, w_ref, wlr_ref, p_ref, lr_ref, h_scr = refs
    else:
        x_ref, sh_ref, sc_ref, w_ref, wlr_ref, p_ref, lr_ref, h_scr = refs
    i = pl.program_id(0)
    j = pl.program_id(1)

    @pl.when(j == 0)
    def _():
        def chunk(c, carry):
            r = pl.ds(pl.multiple_of(c * ROW_TM, ROW_TM), ROW_TM)
            xv = x_ref[r, :]
            if split_input:
                xv = jnp.where(i < PROJ_LAT_TILES, xv, xc_ref[r, :])
            h = _ln_rows(xv) * (1.0 + sc_ref[...]) + sh_ref[...]
            hb = h.astype(BF16)
            h_scr[r, :] = hb
            lr_ref[r, :] = _dot(hb, wlr_ref[...])
            return carry
        lax.fori_loop(0, PROJ_TM // ROW_TM, chunk, 0)

    def project():
        p_ref[...] = _dot(h_scr[...], w_ref[...]).astype(BF16)

    if skip_ctx_gates:
        unused = (i >= PROJ_LAT_TILES) & (j < GATE_W // PROJ_TN)
        pl.when(jnp.logical_not(unused))(project)

        @pl.when(unused)
        def _():
            p_ref[...] = jnp.zeros_like(p_ref)
    else:
        project()


def _projection(x_lat, x_ctx, mod, layer, w_main, w_lr, *, skip_ctx_gates):
    split_input = x_ctx is not None
    brow = _mod_row(PROJ_TM)
    mrow = lambda i: layer * SUBLANES + brow(i)
    x_specs = [pl.BlockSpec((PROJ_TM, D_MODEL), lambda i, j: (jnp.minimum(i, PROJ_LAT_TILES - 1), 0))
               if split_input else pl.BlockSpec((PROJ_TM, D_MODEL), lambda i, j: (i, 0))]
    xs = [x_lat]
    if split_input:
        assert x_ctx.shape[0] == PROJ_TM
        x_specs.append(pl.BlockSpec((PROJ_TM, D_MODEL), lambda i, j: (0, 0), pipeline_mode=pl.Buffered(1)))
        xs.append(x_ctx)
    return pl.pallas_call(
        functools.partial(_proj_body, split_input=split_input, skip_ctx_gates=skip_ctx_gates),
        grid=(T_ALL // PROJ_TM, P_COLS // PROJ_TN),
        in_specs=x_specs + [pl.BlockSpec((None, 1, D_MODEL), lambda i, j: (mrow(i), 0, 0)),
                            pl.BlockSpec((None, 1, D_MODEL), lambda i, j: (mrow(i), 0, 1)),
                            pl.BlockSpec((D_MODEL, PROJ_TN), lambda i, j: (0, j)),
                            pl.BlockSpec((D_MODEL, LANES), lambda i, j: (0, 0))],
        out_specs=[pl.BlockSpec((PROJ_TM, PROJ_TN), lambda i, j: (i, j)),
                   pl.BlockSpec((PROJ_TM, LANES), lambda i, j: (i, 0))],
        out_shape=[jax.ShapeDtypeStruct((T_ALL, P_COLS), BF16),
                   jax.ShapeDtypeStruct((T_ALL, LANES), F32)],
        scratch_shapes=[pltpu.VMEM((PROJ_TM, D_MODEL), BF16)],
        compiler_params=_cparams(("arbitrary", "arbitrary")),
    )(*xs, mod, mod, w_main, w_lr)


def _na_key_row_start(rb):
    return jnp.clip(rb * NA_QROWS - NA_WIN_ROWS // 2, 0, SEQ // GRID_W - NA_KROWS)


def _softmax_pv(s_list, v_list):
    m = s_list[0].max(-1, keepdims=True)
    for s in s_list[1:]:
        m = jnp.maximum(m, s.max(-1, keepdims=True))
    den = None
    o = None
    for s, v in zip(s_list, v_list):
        p = jnp.exp(s - m)
        ps = p.sum(-1, keepdims=True)
        den = ps if den is None else den + ps
        pv = _dot(p.astype(BF16), v)
        o = pv if o is None else o + pv
    return o / den


def _na_lat_body(q_ref, k_ref, v_ref, kc_ref, vc_ref, tile_ref, o_ref, bias_ref):
    rb = pl.program_id(1)
    krows = NA_KROWS * GRID_W
    grid_rows = SEQ // GRID_W

    @pl.when(pl.program_id(2) == 0)
    def _():
        left = lax.broadcasted_iota(jnp.int32, (GRID_W, LANES), 1) < GRID_W
        k0 = _na_key_row_start(rb)
        for qr in range(NA_QROWS):
            r = rb * NA_QROWS + qr
            r0 = jnp.clip(r - NA_WIN_ROWS // 2, 0, grid_rows - NA_WIN_ROWS)
            for kp in range(NA_KROWS // 2):
                idx = []
                for kk in range(2):
                    kr = k0 + 2 * kp + kk
                    inside = (kr >= r0) & (kr < r0 + NA_WIN_ROWS)
                    idx.append(jnp.where(inside, kr - r + NA_WIN_ROWS - 1, 2 * NA_WIN_ROWS - 1))
                bias_ref[qr * GRID_W:(qr + 1) * GRID_W, kp * LANES:(kp + 1) * LANES] = jnp.where(
                    left, tile_ref[idx[0]], tile_ref[idx[1]])

    start = pl.multiple_of(_na_key_row_start(rb) * GRID_W, 4 * GRID_W)
    k = k_ref[pl.ds(start, krows), :]
    v = v_ref[pl.ds(start, krows), :]
    kc = kc_ref[...]
    vc = vc_ref[...]
    scale = NA_HEAD_DIM ** -0.5
    half = NA_QROWS * GRID_W // 2
    for part in range(2):
        rows = slice(part * half, (part + 1) * half)
        q = q_ref[rows, :]
        s_lat = _dot_nt(q, k) * scale + bias_ref[rows, :]
        s_ctx = _dot_nt(q, kc) * scale
        o_ref[rows, :] = _softmax_pv([s_lat, s_ctx], [v, vc]).astype(BF16)


def _na_ctx_body(q_ref, kc_ref, vc_ref, o_ref):
    s = _dot_nt(q_ref[...], kc_ref[...]) * (NA_HEAD_DIM ** -0.5)
    o_ref[...] = _softmax_pv([s], [vc_ref[...]]).astype(BF16)


def _na_bias_tiles(rpb):
    n_dr = 2 * NA_WIN_ROWS - 1
    sel_c = np.zeros((GRID_W, GRID_W, 2 * NA_WIN_COLS - 1), np.float32)
    ok_c = np.zeros((GRID_W, GRID_W), bool)
    for qc in range(GRID_W):
        w0 = int(np.clip(qc - NA_WIN_COLS // 2, 0, GRID_W - NA_WIN_COLS))
        for kc in range(GRID_W):
            if w0 <= kc < w0 + NA_WIN_COLS:
                ok_c[qc, kc] = True
                sel_c[qc, kc, int(np.clip(kc - qc + NA_WIN_COLS - 1, 0, 2 * NA_WIN_COLS - 2))] = 1.0
    t = jnp.einsum('xyc,hac->haxy', jnp.asarray(sel_c), rpb.astype(F32), precision=lax.Precision.HIGHEST)
    t = jnp.where(jnp.asarray(ok_c)[None, None], t, NEG_INF)
    t = jnp.concatenate([t, jnp.full((NA_HEADS, 1, GRID_W, GRID_W), NEG_INF, F32)], axis=1)
    assert t.shape[1] == n_dr + 1
    return jnp.concatenate([t, t], axis=-1)


def _na_lat(p, tiles):
    qrows = NA_QROWS * GRID_W
    n_rb = SEQ // qrows
    return pl.pallas_call(
        _na_lat_body,
        grid=(NA_HEADS, n_rb, BATCH),
        in_specs=[pl.BlockSpec((qrows, LANES), lambda h, rb, b: (b * n_rb + rb, CB_NA_Q + h)),
                  pl.BlockSpec((SEQ, LANES), lambda h, rb, b: (b, CB_NA_K + h)),
                  pl.BlockSpec((SEQ, LANES), lambda h, rb, b: (b, CB_NA_V + h)),
                  pl.BlockSpec((CTX_LEN, LANES), lambda h, rb, b: (CTX_BLK0 + b, CB_NA_K + h)),
                  pl.BlockSpec((CTX_LEN, LANES), lambda h, rb, b: (CTX_BLK0 + b, CB_NA_V + h)),
                  pl.BlockSpec((None, 2 * NA_WIN_ROWS, GRID_W, LANES), lambda h, rb, b: (h, 0, 0, 0))],
        out_specs=pl.BlockSpec((qrows, LANES), lambda h, rb, b: (b * n_rb + rb, h)),
        out_shape=jax.ShapeDtypeStruct((T_LAT, MIX_W), BF16),
        scratch_shapes=[pltpu.VMEM((qrows, NA_KROWS * GRID_W), F32)],
        compiler_params=_cparams(("arbitrary",) * 3),
    )(p, p, p, p, p, tiles)


def _na_ctx(p):
    return pl.pallas_call(
        _na_ctx_body,
        grid=(BATCH, NA_HEADS),
        in_specs=[pl.BlockSpec((CTX_LEN, LANES), lambda b, h: (CTX_BLK0 + b, CB_NA_Q + h)),
                  pl.BlockSpec((CTX_LEN, LANES), lambda b, h: (CTX_BLK0 + b, CB_NA_K + h)),
                  pl.BlockSpec((CTX_LEN, LANES), lambda b, h: (CTX_BLK0 + b, CB_NA_V + h))],
        out_specs=pl.BlockSpec((CTX_LEN, LANES), lambda b, h: (b, h)),
        out_shape=jax.ShapeDtypeStruct((T_CTX, MIX_W), BF16),
        compiler_params=_cparams(("arbitrary",) * 2),
    )(p, p, p)


def _log_sigmoid(z):
    return jnp.minimum(z, 0.0) - jnp.log1p(jnp.exp(-jnp.abs(z)))


GLA_GROUP = 256


def _gla_body(q_ref, k_ref, v_ref, r_ref, lr_ref, wgf_ref, wgb_ref, bgf_ref, bgb_ref, gain_ref, s0_ref,
              y_ref, sfin_ref, la_scr, o_scr, qh_scr, ut_scr, sp_scr, g_scr, st_scr, *, n):
    c = GLA_CHUNK
    nc = n // c
    grp = GLA_GROUP
    cpg = grp // c
    lr = lr_ref[...].astype(BF16)
    la_scr[0] = _log_sigmoid(_dot(lr, wgf_ref[...]) + bgf_ref[...]) / GLA_TAU
    la_scr[1] = _log_sigmoid(_dot(lr, wgb_ref[...]) + bgb_ref[...]) / GLA_TAU
    o_scr[...] = jnp.zeros_like(o_scr)

    lane = lax.broadcasted_iota(jnp.int32, (grp, LANES), 1)
    head_mask = (lane < GLA_KEY_DIM, lane >= GLA_KEY_DIM)
    ri = lax.broadcasted_iota(jnp.int32, (grp, grp), 0)
    ci = lax.broadcasted_iota(jnp.int32, (grp, grp), 1)
    same_chunk = lax.shift_right_logical(ri, 6) == lax.shift_right_logical(ci, 6)
    assert c == 1 << 6
    tri_mask = (same_chunk & (ri >= ci), same_chunk & (ci >= ri))
    tri_mat = tuple(jnp.where(m, 1.0, 0.0).astype(BF16) for m in tri_mask)
    chunk_ones = jnp.where(same_chunk, 1.0, 0.0).astype(BF16)
    scale = GLA_KEY_DIM ** -0.5

    def group(gi, carry):
        r = pl.ds(pl.multiple_of(gi * grp, grp), grp)
        q = q_ref[r, :].astype(F32)
        k = k_ref[r, :].astype(F32)
        for d in range(2):
            la = la_scr[d, r, :]
            la_hi = la.astype(BF16)
            la_lo = (la - la_hi.astype(F32)).astype(BF16)
            cum = _dot(tri_mat[d], la_hi) + _dot(tri_mat[d], la_lo)
            tot = _dot(chunk_ones, la_hi) + _dot(chunk_ones, la_lo)
            qf = q * jnp.exp(cum) * scale
            k_in = (k * jnp.exp(-cum)).astype(BF16)
            k_end = k * jnp.exp(tot - cum)
            g = jnp.exp(tot)
            for ch in range(cpg):
                g_scr[d, pl.ds(gi * cpg + ch, 1), :] = g[ch * c:ch * c + 1, :]
            for hh in range(2):
                chain = hh * 2 + d
                cols = slice(hh * GLA_VAL_DIM, (hh + 1) * GLA_VAL_DIM)
                qh = jnp.where(head_mask[hh], qf, 0.0).astype(BF16)
                qh_scr[chain, r, :] = qh
                a = jnp.where(tri_mask[d], _dot_nt(qh, k_in), 0.0)
                vh = v_ref[r, cols]
                o_scr[r, cols] = o_scr[r, cols] + _dot(a.astype(BF16), vh)
                kh = jnp.where(head_mask[hh], k_end, 0.0).astype(BF16)
                for ch in range(cpg):
                    rows = slice(ch * c, (ch + 1) * c)
                    ut_scr[chain, gi * cpg + ch] = _dot_tn(vh[rows], kh[rows])
        return carry

    lax.fori_loop(0, n // grp, group, 0)

    st_scr[...] = s0_ref[...]

    def scan(i, carry):
        for d in range(2):
            cc = i if d == 0 else nc - 1 - i
            g = g_scr[d, pl.ds(cc, 1), :]
            for hh in range(2):
                chain = hh * 2 + d
                st = st_scr[chain]
                sp_scr[chain, cc] = st.astype(BF16)
                st_scr[chain] = g * st + ut_scr[chain, cc]
        return carry

    lax.fori_loop(0, nc, scan, 0)
    sfin_ref[...] = st_scr[...]

    def inter(cc, carry):
        r = pl.ds(pl.multiple_of(cc * c, c), c)
        for hh in range(2):
            cols = slice(hh * GLA_VAL_DIM, (hh + 1) * GLA_VAL_DIM)
            o = (_dot_nt(qh_scr[hh * 2, r, :], sp_scr[hh * 2, cc])
                 + _dot_nt(qh_scr[hh * 2 + 1, r, :], sp_scr[hh * 2 + 1, cc]))
            o_scr[r, cols] = o_scr[r, cols] + o
        return carry

    lax.fori_loop(0, nc, inter, 0, unroll=4)
    for hh in range(2):
        cols = slice(hh * GLA_VAL_DIM, (hh + 1) * GLA_VAL_DIM)
        rr = r_ref[:, cols].astype(F32)
        y_ref[:, cols] = (_rms_rows(o_scr[:, cols], gain_ref[...]) * (rr * _sigmoid(rr))).astype(BF16)


def _gla(p, lr, wg_f, wg_b, bg_f, bg_b, gain, s0, *, ctx):
    n = CTX_LEN if ctx else SEQ
    blk0 = CTX_BLK0 if ctx else 0
    pairs = GLA_HEADS // 2
    st_spec = pl.BlockSpec((None, None, 4, GLA_VAL_DIM, LANES), lambda b, pr: (b, pr, 0, 0, 0))
    return pl.pallas_call(
        functools.partial(_gla_body, n=n),
        grid=(BATCH, pairs),
        in_specs=[pl.BlockSpec((n, LANES), lambda b, pr: (blk0 + b, CB_GLA_Q + pr)),
                  pl.BlockSpec((n, LANES), lambda b, pr: (blk0 + b, CB_GLA_K + pr)),
                  pl.BlockSpec((n, 2 * LANES), lambda b, pr: (blk0 + b, CB_GLA_V // 2 + pr)),
                  pl.BlockSpec((n, 2 * LANES), lambda b, pr: (blk0 + b, CB_GLA_R // 2 + pr)),
                  pl.BlockSpec((n, LANES), lambda b, pr: (blk0 + b, 0)),
                  pl.BlockSpec((LANES, LANES), lambda b, pr: (0, pr)),
                  pl.BlockSpec((LANES, LANES), lambda b, pr: (0, pr)),
                  pl.BlockSpec((1, LANES), lambda b, pr: (0, pr)),
                  pl.BlockSpec((1, LANES), lambda b, pr: (0, pr)),
                  pl.BlockSpec((1, GLA_VAL_DIM), lambda b, pr: (0, 0)),
                  st_spec],
        out_specs=[pl.BlockSpec((n, 2 * LANES), lambda b, pr: (b, pr)), st_spec],
        out_shape=[jax.ShapeDtypeStruct((BATCH * n, MIX_W), BF16),
                   jax.ShapeDtypeStruct((BATCH, pairs, 4, GLA_VAL_DIM, LANES), F32)],
        scratch_shapes=[pltpu.VMEM((2, n, LANES), F32),
                        pltpu.VMEM((n, 2 * LANES), F32),
                        pltpu.VMEM((4, n, LANES), BF16),
                        pltpu.VMEM((4, n // GLA_CHUNK, GLA_VAL_DIM, LANES), F32),
                        pltpu.VMEM((4, n // GLA_CHUNK, GLA_VAL_DIM, LANES), BF16),
                        pltpu.VMEM((2, max(n // GLA_CHUNK, SUBLANES), LANES), F32),
                        pltpu.VMEM((4, GLA_VAL_DIM, LANES), F32)],
        compiler_params=_cparams(("arbitrary",) * 2),
    )(p, p, p, p, lr, wg_f, wg_b, bg_f, bg_b, gain, s0)


def _gla_gate_weights(w_gate, b_gate):
    wf = jnp.zeros((LANES, GLA_QK_W), F32).at[:GLA_RANK].set(w_gate[0])
    wb = jnp.zeros((LANES, GLA_QK_W), F32).at[GLA_RANK:2 * GLA_RANK].set(w_gate[1])
    return wf.astype(BF16), wb.astype(BF16), b_gate[0:1], b_gate[1:2]


def _dft_tables(n):
    def cs(m):
        idx = (np.arange(m)[:, None] * np.arange(m)[None, :]) % m
        ang = 2.0 * np.pi * idx / m
        return np.cos(ang) / np.sqrt(m), np.sin(ang) / np.sqrt(m)
    cn, sn = cs(n)
    cg, sg = cs(FNET_GROUP_DIM)
    as_bf16 = lambda a: jnp.asarray(a, F32).astype(BF16)
    return as_bf16(cn), as_bf16(sn), as_bf16(np.concatenate([cg, sg], axis=1))


def _fnet_body(u_ref, cn_ref, sn_ref, csg_ref, o_ref):
    gd = FNET_GROUP_DIM
    ts = [_dot(u_ref[:, g * gd:(g + 1) * gd], csg_ref[...]).astype(BF16) for g in range(2)]
    tc = jnp.concatenate([t[:, :gd] for t in ts], axis=1)
    tsn = jnp.concatenate([t[:, gd:] for t in ts], axis=1)
    o_ref[...] = (_dot(cn_ref[...], tc) - _dot(sn_ref[...], tsn)).astype(BF16)


def _fnet(p, *, ctx):
    n = CTX_LEN if ctx else SEQ
    blk0 = CTX_BLK0 if ctx else 0
    cn, sn, csg = _dft_tables(n)
    return pl.pallas_call(
        _fnet_body,
        grid=(BATCH, FNET_GROUPS // 2),
        in_specs=[pl.BlockSpec((n, 2 * LANES), lambda b, g: (blk0 + b, CB_FNET_U // 2 + g)),
                  pl.BlockSpec((n, n), lambda b, g: (0, 0)),
                  pl.BlockSpec((n, n), lambda b, g: (0, 0)),
                  pl.BlockSpec((FNET_GROUP_DIM, 2 * FNET_GROUP_DIM), lambda b, g: (0, 0))],
        out_specs=pl.BlockSpec((n, 2 * LANES), lambda b, g: (b, g)),
        out_shape=jax.ShapeDtypeStruct((BATCH * n, MIX_W), BF16),
        compiler_params=_cparams(("arbitrary",) * 2),
    )(p, cn, sn, csg)


def _rope_tables():
    half = DIFF_QK_DIM // 2
    inv = ROPE_BASE ** (-jnp.arange(0, half, 2, dtype=F32) / half)
    t = jnp.arange(SEQ)
    ang_r = (t // GRID_W).astype(F32)[:, None] * inv
    ang_c = (t % GRID_W).astype(F32)[:, None] * inv
    cos = jnp.concatenate([jnp.cos(ang_r)] * 2 + [jnp.cos(ang_c)] * 2, axis=1)
    sin = jnp.concatenate([jnp.sin(ang_r)] * 2 + [jnp.sin(ang_c)] * 2, axis=1)
    rot = np.zeros((LANES, LANES), np.float32)
    q4 = half // 2
    for base in range(0, LANES, half):
        for j in range(q4):
            rot[base + j + q4, base + j] = -1.0
            rot[base + j, base + j + q4] = 1.0
    return jnp.tile(cos, (1, 2)), jnp.tile(sin, (1, 2)), jnp.asarray(rot, BF16)


def _rope(x, cos, sin, rot):
    return x.astype(F32) * cos + _dot(x, rot) * sin


def _diff_out(q, k_list, v_list, lam, gain, out_scale):
    q = q * (DIFF_QK_DIM ** -0.5)
    lane = lax.broadcasted_iota(jnp.int32, q.shape, 1)
    o = None
    for sub, coef in ((lane < DIFF_QK_DIM, None), (lane >= DIFF_QK_DIM, lam)):
        qs = jnp.where(sub, q, 0.0).astype(BF16)
        os = _softmax_pv([_dot_nt(qs, k) for k in k_list], v_list)
        o = os if coef is None else o - coef * os
    return (_rms_rows(o, gain) * out_scale).astype(BF16)


def _diff_lat_body(q_ref, k_ref, v_ref, kc_ref, vc_ref, cosq_ref, sinq_ref, cosk_ref, sink_ref, rot_ref,
                   lam_ref, gain_ref, o_ref, k_scr, *, out_scale):
    @pl.when(pl.program_id(2) == 0)
    def _():
        k_scr[...] = _rope(k_ref[...], cosk_ref[...], sink_ref[...], rot_ref[...]).astype(BF16)

    for part in range(DIFF_QB // DIFF_SUB):
        rows = slice(part * DIFF_SUB, (part + 1) * DIFF_SUB)
        q = _rope(q_ref[rows, :], cosq_ref[rows, :], sinq_ref[rows, :], rot_ref[...])
        o_ref[rows, :] = _diff_out(q, [kc_ref[...], k_scr[...]], [vc_ref[...], v_ref[...]], lam_ref[:, 0:1],
                                   gain_ref[...], out_scale)


def _diff_ctx_body(q_ref, kc_ref, vc_ref, lam_ref, gain_ref, o_ref, *, out_scale):
    o_ref[...] = _diff_out(q_ref[...].astype(F32), [kc_ref[...]], [vc_ref[...]], lam_ref[:, 0:1],
                           gain_ref[...], out_scale)


def _diff_lat(p, rope, lam_row, gain, lam_init):
    cos, sin, rot = rope
    nqb = SEQ // DIFF_QB
    const = lambda b, h, qb: (0, 0)
    return pl.pallas_call(
        functools.partial(_diff_lat_body, out_scale=1.0 - lam_init),
        grid=(BATCH, DIFF_HEADS, nqb),
        in_specs=[pl.BlockSpec((DIFF_QB, LANES), lambda b, h, qb: (b * nqb + qb, CB_DIFF_Q + h)),
                  pl.BlockSpec((SEQ, LANES), lambda b, h, qb: (b, CB_DIFF_K + h)),
                  pl.BlockSpec((SEQ, LANES), lambda b, h, qb: (b, CB_DIFF_V + h)),
                  pl.BlockSpec((CTX_LEN, LANES), lambda b, h, qb: (CTX_BLK0 + b, CB_DIFF_K + h)),
                  pl.BlockSpec((CTX_LEN, LANES), lambda b, h, qb: (CTX_BLK0 + b, CB_DIFF_V + h)),
                  pl.BlockSpec((DIFF_QB, LANES), lambda b, h, qb: (qb, 0)),
                  pl.BlockSpec((DIFF_QB, LANES), lambda b, h, qb: (qb, 0)),
                  pl.BlockSpec((SEQ, LANES), const),
                  pl.BlockSpec((SEQ, LANES), const),
                  pl.BlockSpec((LANES, LANES), const),
                  pl.BlockSpec((1, LANES), const),
                  pl.BlockSpec((1, DIFF_V_DIM), const)],
        out_specs=pl.BlockSpec((DIFF_QB, LANES), lambda b, h, qb: (b * nqb + qb, h)),
        out_shape=jax.ShapeDtypeStruct((T_LAT, MIX_W), BF16),
        scratch_shapes=[pltpu.VMEM((SEQ, LANES), BF16)],
        compiler_params=_cparams(("arbitrary",) * 3),
    )(p, p, p, p, p, cos, sin, cos, sin, rot, lam_row, gain)


def _diff_ctx(p, lam_row, gain, lam_init):
    const = lambda b, h: (0, 0)
    return pl.pallas_call(
        functools.partial(_diff_ctx_body, out_scale=1.0 - lam_init),
        grid=(BATCH, DIFF_HEADS),
        in_specs=[pl.BlockSpec((CTX_LEN, LANES), lambda b, h: (CTX_BLK0 + b, CB_DIFF_Q + h)),
                  pl.BlockSpec((CTX_LEN, LANES), lambda b, h: (CTX_BLK0 + b, CB_DIFF_K + h)),
                  pl.BlockSpec((CTX_LEN, LANES), lambda b, h: (CTX_BLK0 + b, CB_DIFF_V + h)),
                  pl.BlockSpec((1, LANES), const),
                  pl.BlockSpec((1, DIFF_V_DIM), const)],
        out_specs=pl.BlockSpec((CTX_LEN, LANES), lambda b, h: (b, h)),
        out_shape=jax.ShapeDtypeStruct((T_CTX, MIX_W), BF16),
        compiler_params=_cparams(("arbitrary",) * 2),
    )(p, p, p, lam_row, gain)


MERGE_TM = 512
N_LAT_TILES = T_LAT // MERGE_TM


def _merge_body(*refs, with_ctx):
    gates = refs[0:N_BRANCH]
    y_lat = refs[N_BRANCH:2 * N_BRANCH]
    y_ctx = refs[2 * N_BRANCH:3 * N_BRANCH] if with_ctx else None
    wb_ref, m_ref = refs[-2], refs[-1]
    is_lat = pl.program_id(0) < N_LAT_TILES
    acc = None
    for i in range(N_BRANCH):
        y = y_lat[i][...]
        if with_ctx:
            y = jnp.where(is_lat, y, y_ctx[i][...])
        t = _sigmoid(gates[i][...].astype(F32)) * _dot(y, wb_ref[i])
        acc = t if acc is None else acc + t
    m_ref[...] = acc.astype(BF16)


def _merge(p, ys_lat, ys_ctx, w_branch):
    with_ctx = ys_ctx is not None
    t_rows = T_ALL if with_ctx else T_LAT
    gate_specs = [pl.BlockSpec((MERGE_TM, D_MODEL), functools.partial(lambda i, g: (i, g), g=g))
                  for g in range(N_BRANCH)]
    y_specs = [pl.BlockSpec((MERGE_TM, MIX_W), lambda i: (jnp.minimum(i, N_LAT_TILES - 1), 0))
               for _ in range(N_BRANCH)]
    ys = list(ys_lat)
    if with_ctx:
        y_specs += [pl.BlockSpec((MERGE_TM, MIX_W), lambda i: (jnp.maximum(i - N_LAT_TILES, 0), 0))
                    for _ in range(N_BRANCH)]
        ys += list(ys_ctx)
    return pl.pallas_call(
        functools.partial(_merge_body, with_ctx=with_ctx),
        grid=(t_rows // MERGE_TM,),
        in_specs=gate_specs + y_specs + [pl.BlockSpec((N_BRANCH, MIX_W, D_MODEL), lambda i: (0, 0, 0),
                                                      pipeline_mode=pl.Buffered(1))],
        out_specs=pl.BlockSpec((MERGE_TM, D_MODEL), lambda i: (i, 0)),
        out_shape=jax.ShapeDtypeStruct((t_rows, D_MODEL), BF16),
        compiler_params=_cparams(("arbitrary",)),
    )(p, p, p, p, *ys, w_branch)


def _residual_ln(x, gate, f, g, b):
    return _ln_rows(DEEPNORM_ALPHA * x + gate * f) * g + b


def _outproj_body(*refs, split_input):
    if split_input:
        m_ref, x_ref, xc_ref, gate_ref, w_ref, g_ref, b_ref, o_ref = refs
        x = jnp.where(pl.program_id(0) < N_LAT_TILES, x_ref[...], xc_ref[...])
    else:
        m_ref, x_ref, gate_ref, w_ref, g_ref, b_ref, o_ref = refs
        x = x_ref[...]
    f = _dot(m_ref[...], w_ref[...])
    o_ref[...] = _residual_ln(x, gate_ref[...], f, g_ref[...], b_ref[...])


def _outproj(m, x_lat, x_ctx, mod, layer, w_out, ln_g, ln_b, t_rows):
    split_input = x_ctx is not None
    brow = _mod_row(MERGE_TM)
    vec = pl.BlockSpec((1, D_MODEL), lambda i: (0, 0))
    rows = pl.BlockSpec((MERGE_TM, D_MODEL), lambda i: (i, 0))
    x_specs, xs = [rows], [x_lat]
    if split_input:
        x_specs = [pl.BlockSpec((MERGE_TM, D_MODEL), lambda i: (jnp.minimum(i, N_LAT_TILES - 1), 0)),
                   pl.BlockSpec((MERGE_TM, D_MODEL), lambda i: (jnp.maximum(i - N_LAT_TILES, 0), 0))]
        xs = [x_lat, x_ctx]
    return pl.pallas_call(
        functools.partial(_outproj_body, split_input=split_input),
        grid=(t_rows // MERGE_TM,),
        in_specs=[rows] + x_specs
        + [pl.BlockSpec((None, 1, D_MODEL), lambda i: (layer * SUBLANES + brow(i), 0, 2)),
           pl.BlockSpec((D_MODEL, D_MODEL), lambda i: (0, 0), pipeline_mode=pl.Buffered(1)), vec, vec],
        out_specs=rows,
        out_shape=jax.ShapeDtypeStruct((t_rows, D_MODEL), F32),
        compiler_params=_cparams(("arbitrary",)),
    )(m, *xs, mod, w_out, ln_g, ln_b)


RT_E1, RT_E2, RT_W1, RT_W2, RT_RANK1, RT_RANK2 = range(6)


def _first_lane_of_max(vals, lane_f):
    m = vals.max(-1, keepdims=True)
    idx = jnp.where(vals == m, lane_f, float(LANES)).min(-1, keepdims=True)
    return m, idx


def _router_body(x_ref, sh_ref, sc_ref, wr_ref, br_ref, h_ref, route_ref, cnt_ref, run_scr):
    @pl.when(pl.program_id(0) == 0)
    def _():
        run_scr[...] = jnp.zeros_like(run_scr)

    tm = ROW_TM
    h = _ln_rows(x_ref[...]) * (1.0 + sc_ref[...]) + sh_ref[...]
    hb = h.astype(BF16)
    for s in range(ROW_SLABS):
        h_ref[pl.ds(s, tm, stride=ROW_SLABS), :] = hb[:, s * LANES:(s + 1) * LANES].astype(F32)
    logits = _dot(hb, wr_ref[...]) + br_ref[...]
    lane = lax.broadcasted_iota(jnp.int32, (tm, LANES), 1)
    lane_f = lane.astype(F32)
    is_group = lane < N_GROUPS
    g_logits = jnp.where(is_group, logits, NEG_INF)
    g_max, g_sel = _first_lane_of_max(g_logits, lane_f)
    g_w = 1.0 / jnp.where(is_group, jnp.exp(logits - g_max), 0.0).sum(-1, keepdims=True)
    lo = N_GROUPS + EXPERTS_PER_GROUP * g_sel
    e_logits = jnp.where((lane_f >= lo) & (lane_f < lo + EXPERTS_PER_GROUP), logits, NEG_INF)
    v1, i1 = _first_lane_of_max(e_logits, lane_f)
    v2, i2 = _first_lane_of_max(jnp.where(lane_f == i1, NEG_INF, e_logits), lane_f)
    t = jnp.exp(v2 - v1)
    w1 = g_w / (1.0 + t)
    w2 = g_w * t / (1.0 + t)
    e1 = i1 - N_GROUPS
    e2 = i2 - N_GROUPS

    oh1 = lane_f == e1
    oh2 = lane_f == e2
    both = jnp.where(oh1 | oh2, 1.0, 0.0)
    ri = lax.broadcasted_iota(jnp.int32, (tm, tm), 0)
    ci = lax.broadcasted_iota(jnp.int32, (tm, tm), 1)
    before = jnp.where(ci < ri, 1.0, 0.0).astype(BF16)
    excl = _dot(before, both.astype(BF16)) + run_scr[0:1, :]
    rank1 = jnp.where(oh1, excl, 0.0).sum(-1, keepdims=True)
    rank2 = jnp.where(oh2, excl, 0.0).sum(-1, keepdims=True)
    run_scr[...] = run_scr[...] + both.sum(0, keepdims=True)
    cnt_ref[...] = run_scr[...]

    rec = jnp.zeros((tm, LANES), F32)
    for idx, val in ((RT_E1, e1), (RT_E2, e2), (RT_W1, w1), (RT_W2, w2), (RT_RANK1, rank1), (RT_RANK2, rank2)):
        rec = jnp.where(lane == idx, val, rec)
    route_ref[...] = rec


def _router(x1, mod, layer, w_r, b_r, t_rows):
    brow = _mod_row(ROW_TM)
    mrow = lambda i: layer * SUBLANES + brow(i)
    return pl.pallas_call(
        _router_body,
        grid=(t_rows // ROW_TM,),
        in_specs=[pl.BlockSpec((ROW_TM, D_MODEL), lambda i: (i, 0)),
                  pl.BlockSpec((None, 1, D_MODEL), lambda i: (mrow(i), 0, 3)),
                  pl.BlockSpec((None, 1, D_MODEL), lambda i: (mrow(i), 0, 4)),
                  pl.BlockSpec((D_MODEL, LANES), lambda i: (0, 0)),
                  pl.BlockSpec((1, LANES), lambda i: (0, 0))],
        out_specs=[pl.BlockSpec((ROW_TM * ROW_SLABS, LANES), lambda i: (i, 0)),
                   pl.BlockSpec((ROW_TM, LANES), lambda i: (i, 0)),
                   pl.BlockSpec((SUBLANES, LANES), lambda i: (0, 0))],
        out_shape=[jax.ShapeDtypeStruct((t_rows * ROW_SLABS, LANES), F32),
                   jax.ShapeDtypeStruct((t_rows, LANES), F32),
                   jax.ShapeDtypeStruct((SUBLANES, LANES), F32)],
        scratch_shapes=[pltpu.VMEM((SUBLANES, LANES), F32)],
        compiler_params=_cparams(("arbitrary",)),
    )(x1, mod, mod, w_r, b_r)


def _router_weights(w_group, b_group, w_router, b_router):
    wr = jnp.concatenate([w_group, jnp.transpose(w_router, (1, 0, 2)).reshape(D_MODEL, N_EXPERTS)], axis=1)
    br = jnp.concatenate([b_group, b_router.reshape(N_EXPERTS)])
    pad = LANES - wr.shape[1]
    return jnp.pad(wr, ((0, 0), (0, pad))).astype(BF16), jnp.pad(br, (0, pad)).reshape(1, LANES)


def _token_rows(ref, tok):
    return ref.at[pl.ds(pl.multiple_of(tok * ROW_SLABS, ROW_SLABS), ROW_SLABS), :]


DMA_UNROLL = 8


def _dispatch_body(pos_ref, h_ref, xs_in_ref, xs_ref, sem, *, t_rows):
    del xs_in_ref
    tt = ROW_TM
    base = pl.program_id(0) * tt

    def issue(j, carry):
        for k in range(2):
            pltpu.make_async_copy(_token_rows(h_ref, j), _token_rows(xs_ref, pos_ref[k * t_rows + base + j]),
                                  sem).start()
        return carry

    def drain(j, carry):
        pltpu.make_async_copy(_token_rows(h_ref, 0), _token_rows(xs_ref, 0), sem).wait()
        return carry

    lax.fori_loop(0, tt, issue, 0, unroll=DMA_UNROLL)
    lax.fori_loop(0, 2 * tt, drain, 0, unroll=DMA_UNROLL)


def _dispatch(pos, h, xs0):
    t_rows = h.shape[0] // ROW_SLABS
    return pl.pallas_call(
        functools.partial(_dispatch_body, t_rows=t_rows),
        grid_spec=pltpu.PrefetchScalarGridSpec(
            num_scalar_prefetch=1,
            grid=(t_rows // ROW_TM,),
            in_specs=[pl.BlockSpec((ROW_TM * ROW_SLABS, LANES), lambda i, pos: (i, 0)),
                      pl.BlockSpec(memory_space=pl.ANY)],
            out_specs=pl.BlockSpec(memory_space=pl.ANY),
            scratch_shapes=[pltpu.SemaphoreType.DMA(())]),
        out_shape=jax.ShapeDtypeStruct(xs0.shape, F32),
        input_output_aliases={2: 0},
        compiler_params=_cparams(("arbitrary",)),
    )(pos, h, xs0)


def _expert_body(te_ref, nu_ref, nxt_ref, slot_ref, xs_ref, wgu_hbm, wd_hbm, ys_ref,
                 wgu_f, wd_f, wgu_b, wd_b, sem, *, layer):
    i = pl.program_id(0)
    tm = EXPERT_TM
    used = i < nu_ref[0]

    def fetch(expert, sl):
        e = layer * N_EXPERTS + expert
        return (pltpu.make_async_copy(wgu_hbm.at[e], wgu_f.at[sl], sem.at[0, sl]),
                pltpu.make_async_copy(wd_hbm.at[e], wd_f.at[sl], sem.at[1, sl]))

    @pl.when(used)
    def _():
        @pl.when((i == 0) | (te_ref[i] != te_ref[jnp.maximum(i - 1, 0)]))
        def _():
            sl = slot_ref[i]

            @pl.when(i == 0)
            def _():
                for cp in fetch(te_ref[0], 0):
                    cp.start()

            @pl.when(nxt_ref[i] >= 0)
            def _():
                for cp in fetch(nxt_ref[i], 1 - sl):
                    cp.start()

            for cp in fetch(te_ref[i], sl):
                cp.wait()

            def cast(c, carry):
                r = pl.ds(pl.multiple_of(c * ROW_TM, ROW_TM), ROW_TM)
                wgu_b[r, :] = wgu_f[sl, r, :].astype(BF16)
                return carry
            lax.fori_loop(0, D_MODEL // ROW_TM, cast, 0)
            wd_b[...] = wd_f[sl].astype(BF16)

        x = jnp.concatenate([xs_ref[pl.ds(s, tm, stride=ROW_SLABS), :].astype(BF16) for s in range(ROW_SLABS)],
                            axis=1)
        hgu = _dot(x, wgu_b[...])
        hg = hgu[:, :EXPERT_DIM]
        hu = hgu[:, EXPERT_DIM:]
        act = (hg * _sigmoid(hg) * hu).astype(BF16)
        y = _dot(act, wd_b[...])
        for s in range(ROW_SLABS):
            ys_ref[pl.ds(s, tm, stride=ROW_SLABS), :] = y[:, s * LANES:(s + 1) * LANES]

    @pl.when(jnp.logical_not(used))
    def _():
        ys_ref[...] = jnp.zeros_like(ys_ref)


def _experts(tile_expert, n_used, next_expert, tile_slot, xs, w_gu, w_down, layer):
    n_tiles = xs.shape[0] // (EXPERT_TM * ROW_SLABS)
    blk = EXPERT_TM * ROW_SLABS
    return pl.pallas_call(
        functools.partial(_expert_body, layer=layer),
        grid_spec=pltpu.PrefetchScalarGridSpec(
            num_scalar_prefetch=4,
            grid=(n_tiles,),
            in_specs=[pl.BlockSpec((blk, LANES), lambda i, te, nu, nx, sl: (jnp.minimum(i, nu[0] - 1), 0)),
                      pl.BlockSpec(memory_space=pl.ANY),
                      pl.BlockSpec(memory_space=pl.ANY)],
            out_specs=pl.BlockSpec((blk, LANES), lambda i, te, nu, nx, sl: (i, 0)),
            scratch_shapes=[pltpu.VMEM((2, D_MODEL, 2 * EXPERT_DIM), F32),
                            pltpu.VMEM((2, EXPERT_DIM, D_MODEL), F32),
                            pltpu.VMEM((D_MODEL, 2 * EXPERT_DIM), BF16),
                            pltpu.VMEM((EXPERT_DIM, D_MODEL), BF16),
                            pltpu.SemaphoreType.DMA((2, 2))]),
        out_shape=jax.ShapeDtypeStruct(xs.shape, F32),
        compiler_params=_cparams(("arbitrary",)),
    )(tile_expert, n_used, next_expert, tile_slot, xs, w_gu, w_down)


GATHER_PITCH = ROW_SLABS + SUBLANES


def _final_body(pos_ref, x_ref, ys_ref, route_ref, gate_ref, g_ref, b_ref, out_ref, buf, sem, *, t_rows):
    tm = ROW_TM
    i = pl.program_id(0)
    slot = lax.rem(i, 2)

    def gather_copy(tile, sl, t, k):
        dst = buf.at[sl, pl.ds(pl.multiple_of((k * tm + t) * GATHER_PITCH, SUBLANES), ROW_SLABS), :]
        return pltpu.make_async_copy(_token_rows(ys_ref, pos_ref[k * t_rows + tile * tm + t]), dst, sem.at[sl])

    def start_tile(tile, sl):
        def issue(t, carry):
            for k in range(2):
                gather_copy(tile, sl, t, k).start()
            return carry
        lax.fori_loop(0, tm, issue, 0, unroll=DMA_UNROLL // 2)

    @pl.when(i == 0)
    def _():
        start_tile(0, 0)

    @pl.when(i + 1 < pl.num_programs(0))
    def _():
        start_tile(i + 1, 1 - slot)

    def drain(t, carry):
        for k in range(2):
            gather_copy(i, slot, 0, k).wait()
        return carry
    lax.fori_loop(0, tm, drain, 0, unroll=DMA_UNROLL // 2)

    rt = route_ref[...]
    w1 = rt[:, RT_W1:RT_W1 + 1]
    w2 = rt[:, RT_W2:RT_W2 + 1]
    f = jnp.concatenate(
        [w1 * buf[slot, pl.ds(s, tm, stride=GATHER_PITCH), :]
         + w2 * buf[slot, pl.ds(tm * GATHER_PITCH + s, tm, stride=GATHER_PITCH), :] for s in range(ROW_SLABS)],
        axis=1)
    out_ref[...] = _residual_ln(x_ref[...], gate_ref[...], f, g_ref[...], b_ref[...])


def _final(pos, x1, ys, route, mod, layer, ln_g, ln_b, t_rows):
    brow = _mod_row(ROW_TM)
    vec = pl.BlockSpec((1, D_MODEL), lambda i, pos: (0, 0))
    return pl.pallas_call(
        functools.partial(_final_body, t_rows=t_rows),
        grid_spec=pltpu.PrefetchScalarGridSpec(
            num_scalar_prefetch=1,
            grid=(t_rows // ROW_TM,),
            in_specs=[pl.BlockSpec((ROW_TM, D_MODEL), lambda i, pos: (i, 0)),
                      pl.BlockSpec(memory_space=pl.ANY),
                      pl.BlockSpec((ROW_TM, LANES), lambda i, pos: (i, 0)),
                      pl.BlockSpec((None, 1, D_MODEL), lambda i, pos: (layer * SUBLANES + brow(i), 0, 5)),
                      vec, vec],
            out_specs=pl.BlockSpec((ROW_TM, D_MODEL), lambda i, pos: (i, 0)),
            scratch_shapes=[pltpu.VMEM((2, 2 * ROW_TM * GATHER_PITCH, LANES), F32),
                            pltpu.SemaphoreType.DMA((2,))]),
        out_shape=jax.ShapeDtypeStruct((t_rows, D_MODEL), F32),
        compiler_params=_cparams(("arbitrary",)),
    )(pos, x1, ys, route, mod, ln_g, ln_b)


MOE_TILES = -(-(2 * T_ALL + N_EXPERTS * (EXPERT_TM - 1)) // EXPERT_TM)


def _moe(x1, mod, layer, router_w, w_gu, w_down, ln_g, ln_b, t_rows, slot_buf):
    h, route, cnt = _router(x1, mod, layer, *router_w, t_rows)
    n_tiles = MOE_TILES
    if slot_buf is None:
        slot_buf = jnp.zeros((n_tiles * EXPERT_TM * ROW_SLABS, LANES), F32)
    counts = cnt[0, :N_EXPERTS].astype(jnp.int32)
    tiles_per = (counts + EXPERT_TM - 1) // EXPERT_TM
    tile_end = jnp.cumsum(tiles_per)
    tile_start = tile_end - tiles_per
    n_used = tile_end[-1:]
    route_t = route.T
    e = route_t[RT_E1:RT_E2 + 1].astype(jnp.int32)
    rank = route_t[RT_RANK1:RT_RANK2 + 1].astype(jnp.int32)
    ids = jnp.arange(N_EXPERTS, dtype=jnp.int32)

    def lookup(table, idx):
        hit = idx[None] == ids.reshape((N_EXPERTS,) + (1,) * idx.ndim)
        return jnp.sum(jnp.where(hit, table.reshape((N_EXPERTS,) + (1,) * idx.ndim), 0), axis=0)

    pos = (lookup(tile_start, e) * EXPERT_TM + rank).reshape(-1)
    ti = jnp.minimum(jnp.arange(n_tiles, dtype=jnp.int32), n_used[0] - 1)
    tile_expert = jnp.sum((ti[:, None] >= tile_end[None, :]).astype(jnp.int32), axis=1)
    owns = tiles_per > 0
    later = jnp.where(owns[None, :] & (ids[None, :] > ids[:, None]), ids[None, :], N_EXPERTS)
    nxt_e = jnp.min(later, axis=1)
    nxt_e = jnp.where(nxt_e == N_EXPERTS, -1, nxt_e)
    order_e = jnp.cumsum(owns.astype(jnp.int32)) - 1
    next_expert = lookup(nxt_e, tile_expert)
    tile_slot = lookup(order_e, tile_expert) % 2
    xs = _dispatch(pos, h, slot_buf)
    ys = _experts(tile_expert, n_used, next_expert, tile_slot, xs, w_gu, w_down, layer)
    return _final(pos, x1, ys, route, mod, layer, ln_g, ln_b, t_rows), ys


def _prep_w_in(w):
    offs = np.concatenate([[0], np.cumsum(ALL_SPLITS)])
    piece = lambda i: w[:, int(offs[i]):int(offs[i + 1])]
    w_main = jnp.concatenate([piece(i) for i in _PIECE_ORDER], axis=1).astype(BF16)
    w_lr = jnp.pad(piece(_LR_PIECE), ((0, 0), (0, LANES - 2 * GLA_RANK))).astype(BF16)
    return w_main, w_lr


def kernel(x, c, ctx, c_ctx, w_mod, b_mod, w_in, na_rpb, gla_w_gate, gla_b_gate, gla_norm, diff_lambda, diff_norm,
           w_branch, w_out, ln1_g, ln1_b, ln2_g, ln2_b, w_group, b_group, w_router, b_router, w_gu, w_down):
    x_lat, x_ctx = x.reshape(T_LAT, D_MODEL), ctx.reshape(T_CTX, D_MODEL)
    c8 =jnp.concatenate([c, c_ctx[None], jnp.zeros((SUBLANES - BATCH - 1, D_MODEL), F32)], axis=0)
    mod = _modulation(c8, w_mod, b_mod).reshape(DEPTH * SUBLANES, 1, 6 * D_MODEL)
    rope = _rope_tables()
    w_gu_flat = w_gu.reshape(DEPTH * N_EXPERTS, D_MODEL, 2 * EXPERT_DIM)
    w_down_flat = w_down.reshape(DEPTH * N_EXPERTS, EXPERT_DIM, D_MODEL)
    s_zero = jnp.zeros((BATCH, GLA_HEADS // 2, 4, GLA_VAL_DIM, LANES), F32)
    slot_buf = None

    for l in range(DEPTH):
        last = l == DEPTH - 1
        t_rows = T_LAT if last else T_ALL
        lam_init = 0.8 - 0.6 * math.exp(-0.3 * l)
        lq1, lk1, lq2, lk2 = diff_lambda[l].astype(F32)
        lam = jnp.exp(jnp.sum(lq1 * lk1)) - jnp.exp(jnp.sum(lq2 * lk2)) + lam_init
        lam_row = jnp.full((1, LANES), lam, F32)

        w_main, w_lr = _prep_w_in(w_in[l])
        p, lr = _projection(x_lat, x_ctx, mod, l, w_main, w_lr, skip_ctx_gates=last)

        gla_w = _gla_gate_weights(gla_w_gate[l], gla_b_gate[l])
        gla_gain = gla_norm[l].reshape(1, GLA_VAL_DIM)
        diff_gain = diff_norm[l].reshape(1, DIFF_V_DIM)
        y_gla_ctx, states = _gla(p, lr, *gla_w, gla_gain, s_zero, ctx=True)
        y_gla, _ = _gla(p, lr, *gla_w, gla_gain, states, ctx=False)
        ys_lat = (_na_lat(p, _na_bias_tiles(na_rpb[l])), y_gla, _fnet(p, ctx=False),
                  _diff_lat(p, rope, lam_row, diff_gain, lam_init))
        ys_ctx = None
        if not last:
            ys_ctx = (_na_ctx(p), y_gla_ctx, _fnet(p, ctx=True), _diff_ctx(p, lam_row, diff_gain, lam_init))

        m = _merge(p, ys_lat, ys_ctx, w_branch[l].astype(BF16))
        x1 = _outproj(m, x_lat, x_ctx, mod, l, w_out[l].astype(BF16), ln1_g[l].reshape(1, -1),
                      ln1_b[l].reshape(1, -1), t_rows)
        router_w = _router_weights(w_group[l], b_group[l], w_router[l], b_router[l])
        x_lat, slot_buf = _moe(x1, mod, l, router_w, w_gu_flat, w_down_flat, ln2_g[l].reshape(1, -1),
                               ln2_b[l].reshape(1, -1), t_rows, slot_buf)
        x_ctx = None
    return x_lat.reshape(BATCH, SEQ, D_MODEL)
```

```python
import functools
import math

import numpy as np
import jax
import jax.numpy as jnp
from jax import lax
from jax.experimental import pallas as pl
from jax.experimental.pallas import tpu as pltpu

F32 = jnp.float32
BF16 = jnp.bfloat16

D_MODEL = 2048
BATCH = 4
SEQ = 2048
DEPTH = 2
GRID_W = 64
CTX_LEN = 256
N_BRANCH = 4
MIX_W = D_MODEL // N_BRANCH
NA_HEADS = 4
NA_HEAD_DIM = MIX_W // NA_HEADS
NA_WIN_ROWS = 8
NA_WIN_COLS = 16
GLA_HEADS = 4
GLA_VAL_DIM = MIX_W // GLA_HEADS
GLA_KEY_DIM = GLA_VAL_DIM // 2
GLA_RANK = 16
GLA_TAU = 16.0
GLA_CHUNK = 64
FNET_GROUPS = 4
FNET_GROUP_DIM = MIX_W // FNET_GROUPS
DIFF_HEADS = 4
DIFF_V_DIM = MIX_W // DIFF_HEADS
DIFF_QK_DIM = DIFF_V_DIM // 2
ROPE_BASE = 10000.0
N_GROUPS = 4
EXPERTS_PER_GROUP = 8
N_EXPERTS = N_GROUPS * EXPERTS_PER_GROUP
EXPERT_DIM = D_MODEL // 4
DEEPNORM_ALPHA = (2 * DEPTH) ** 0.25
LN_EPS = 1e-6
RMS_EPS = 1e-5
NEG_INF = -1e30

GLA_QK_W = GLA_HEADS * GLA_KEY_DIM
DIFF_QK_W = DIFF_HEADS * 2 * DIFF_QK_DIM
KV_SPLITS = (MIX_W, MIX_W, GLA_QK_W, MIX_W, 2 * GLA_RANK, DIFF_QK_W, MIX_W)
Q_SPLITS = (MIX_W, GLA_QK_W, MIX_W, MIX_W, DIFF_QK_W)
GATE_W = N_BRANCH * D_MODEL
ALL_SPLITS = KV_SPLITS + Q_SPLITS + (GATE_W,)

LANES = 128
SUBLANES = 8
VMEM_LIMIT = 56 * 1024 * 1024

T_LAT = BATCH * SEQ
T_CTX = BATCH * CTX_LEN
T_ALL = T_LAT + T_CTX
ROW_SLABS = D_MODEL // LANES

CB_GATES = 0
CB_NA_K = 64
CB_NA_V = 68
CB_GLA_K = 72
CB_GLA_V = 74
CB_DIFF_K = 78
CB_DIFF_V = 82
CB_NA_Q = 86
CB_GLA_Q = 90
CB_GLA_R = 92
CB_FNET_U = 96
CB_DIFF_Q = 100
P_COLS = 104 * LANES
_PIECE_ORDER = (12, 0, 1, 2, 3, 5, 6, 7, 8, 9, 10, 11)
_LR_PIECE = 4

PROJ_TM = 1024
PROJ_TN = 1024
ROW_TM = 256
EXPERT_TM = 256
NA_QROWS = 8
NA_KROWS = 16
DIFF_QB = 1024
DIFF_SUB = 256
CTX_BLK0 = T_LAT // CTX_LEN


def _cparams(sem, vmem=VMEM_LIMIT):
    return pltpu.CompilerParams(dimension_semantics=sem, vmem_limit_bytes=vmem)


def _dot(a, b):
    return jnp.dot(a, b, preferred_element_type=F32)


def _dot_nt(a, b):
    return lax.dot_general(a, b, (((1,), (1,)), ((), ())), preferred_element_type=F32)


def _dot_tn(a, b):
    return lax.dot_general(a, b, (((0,), (0,)), ((), ())), preferred_element_type=F32)


def _sigmoid(x):
    return 1.0 / (1.0 + jnp.exp(-x))


def _ln_rows(x):
    mu = jnp.mean(x, -1, keepdims=True)
    xc = x - mu
    var = jnp.mean(xc * xc, -1, keepdims=True)
    return xc * lax.rsqrt(var + LN_EPS)


def _rms_rows(x, gain):
    return x * lax.rsqrt(jnp.mean(x * x, -1, keepdims=True) + RMS_EPS) * gain


def _mod_row(tile_rows):
    per_batch = SEQ // tile_rows
    return lambda i: jnp.minimum(i // per_batch, BATCH)


MOD_TK = 256


def _mod_body(c_ref, w_ref, b_ref, o_ref):
    @pl.when(pl.program_id(1) == 0)
    def _():
        o_ref[...] = jnp.broadcast_to(b_ref[...], o_ref.shape)

    c = c_ref[...]
    s = (c * _sigmoid(c)).astype(BF16)
    o_ref[...] += _dot(s, w_ref[...].astype(BF16))


def _modulation(c8, w_mod, b_mod):
    depth, d, cols = w_mod.shape
    return pl.pallas_call(
        _mod_body,
        grid=(depth, d // MOD_TK),
        in_specs=[pl.BlockSpec((SUBLANES, MOD_TK), lambda l, k: (0, k)),
                  pl.BlockSpec((None, MOD_TK, cols), lambda l, k: (l, k, 0)),
                  pl.BlockSpec((None, 1, cols), lambda l, k: (l, 0, 0))],
        out_specs=pl.BlockSpec((None, SUBLANES, cols), lambda l, k: (l, 0, 0)),
        out_shape=jax.ShapeDtypeStruct((depth, SUBLANES, cols), F32),
        compiler_params=_cparams(("arbitrary", "arbitrary")),
    )(c8, w_mod, b_mod.reshape(depth, 1, cols))


PROJ_LAT_TILES = T_LAT // PROJ_TM


def _proj_body(*refs, split_input, skip_ctx_gates):
    if split_input:
        x_ref, xc_ref, sh_ref, sc_ref, w_ref, wlr_ref, p_ref, lr_ref, h_scr = refs
    else:
        x_ref, sh_ref, sc_ref, w_ref, wlr_ref, p_ref, lr_ref, h_scr = refs
    i = pl.program_id(0)
    j = pl.program_id(1)

    @pl.when(j == 0)
    def _():
        def chunk(c, carry):
            r = pl.ds(pl.multiple_of(c * ROW_TM, ROW_TM), ROW_TM)
            xv = x_ref[r, :]
            if split_input:
                xv = jnp.where(i < PROJ_LAT_TILES, xv, xc_ref[r, :])
            h = _ln_rows(xv) * (1.0 + sc_ref[...]) + sh_ref[...]
            hb = h.astype(BF16)
            h_scr[r, :] = hb
            lr_ref[r, :] = _dot(hb, wlr_ref[...])
            return carry
        lax.fori_loop(0, PROJ_TM // ROW_TM, chunk, 0)

    def project():
        p_ref[...] = _dot(h_scr[...], w_ref[...]).astype(BF16)

    if skip_ctx_gates:
        unused = (i >= PROJ_LAT_TILES) & (j < GATE_W // PROJ_TN)
        pl.when(jnp.logical_not(unused))(project)

        @pl.when(unused)
        def _():
            p_ref[...] = jnp.zeros_like(p_ref)
    else:
        project()


def _projection(x_lat, x_ctx, mod, layer, w_main, w_lr, *, skip_ctx_gates):
    split_input = x_ctx is not None
    brow = _mod_row(PROJ_TM)
    mrow = lambda i: layer * SUBLANES + brow(i)
    x_specs = [pl.BlockSpec((PROJ_TM, D_MODEL), lambda i, j: (jnp.minimum(i, PROJ_LAT_TILES - 1), 0))
               if split_input else pl.BlockSpec((PROJ_TM, D_MODEL), lambda i, j: (i, 0))]
    xs = [x_lat]
    if split_input:
        assert x_ctx.shape[0] == PROJ_TM
        x_specs.append(pl.BlockSpec((PROJ_TM, D_MODEL), lambda i, j: (0, 0), pipeline_mode=pl.Buffered(1)))
        xs.append(x_ctx)
    return pl.pallas_call(
        functools.partial(_proj_body, split_input=split_input, skip_ctx_gates=skip_ctx_gates),
        grid=(T_ALL // PROJ_TM, P_COLS // PROJ_TN),
        in_specs=x_specs + [pl.BlockSpec((None, 1, D_MODEL), lambda i, j: (mrow(i), 0, 0)),
                            pl.BlockSpec((None, 1, D_MODEL), lambda i, j: (mrow(i), 0, 1)),
                            pl.BlockSpec((D_MODEL, PROJ_TN), lambda i, j: (0, j)),
                            pl.BlockSpec((D_MODEL, LANES), lambda i, j: (0, 0))],
        out_specs=[pl.BlockSpec((PROJ_TM, PROJ_TN), lambda i, j: (i, j)),
                   pl.BlockSpec((PROJ_TM, LANES), lambda i, j: (i, 0))],
        out_shape=[jax.ShapeDtypeStruct((T_ALL, P_COLS), BF16),
                   jax.ShapeDtypeStruct((T_ALL, LANES), F32)],
        scratch_shapes=[pltpu.VMEM((PROJ_TM, D_MODEL), BF16)],
        compiler_params=_cparams(("arbitrary", "arbitrary")),
    )(*xs, mod, mod, w_main, w_lr)


def _na_key_row_start(rb):
    return jnp.clip(rb * NA_QROWS - NA_WIN_ROWS // 2, 0, SEQ // GRID_W - NA_KROWS)


def _softmax_pv(s_list, v_list):
    m = s_list[0].max(-1, keepdims=True)
    for s in s_list[1:]:
        m = jnp.maximum(m, s.max(-1, keepdims=True))
    den = None
    o = None
    for s, v in zip(s_list, v_list):
        p = jnp.exp(s - m)
        ps = p.sum(-1, keepdims=True)
        den = ps if den is None else den + ps
        pv = _dot(p.astype(BF16), v)
        o = pv if o is None else o + pv
    return o / den


def _na_lat_body(q_ref, k_ref, v_ref, kc_ref, vc_ref, tile_ref, o_ref, bias_ref):
    rb = pl.program_id(1)
    krows = NA_KROWS * GRID_W
    grid_rows = SEQ // GRID_W

    @pl.when(pl.program_id(2) == 0)
    def _():
        left = lax.broadcasted_iota(jnp.int32, (GRID_W, LANES), 1) < GRID_W
        k0 = _na_key_row_start(rb)
        for qr in range(NA_QROWS):
            r = rb * NA_QROWS + qr
            r0 = jnp.clip(r - NA_WIN_ROWS // 2, 0, grid_rows - NA_WIN_ROWS)
            for kp in range(NA_KROWS // 2):
                idx = []
                for kk in range(2):
                    kr = k0 + 2 * kp + kk
                    inside = (kr >= r0) & (kr < r0 + NA_WIN_ROWS)
                    idx.append(jnp.where(inside, kr - r + NA_WIN_ROWS - 1, 2 * NA_WIN_ROWS - 1))
                bias_ref[qr * GRID_W:(qr + 1) * GRID_W, kp * LANES:(kp + 1) * LANES] = jnp.where(
                    left, tile_ref[idx[0]], tile_ref[idx[1]])

    start = pl.multiple_of(_na_key_row_start(rb) * GRID_W, 4 * GRID_W)
    k = k_ref[pl.ds(start, krows), :]
    v = v_ref[pl.ds(start, krows), :]
    kc = kc_ref[...]
    vc = vc_ref[...]
    scale = NA_HEAD_DIM ** -0.5
    half = NA_QROWS * GRID_W // 2
    for part in range(2):
        rows = slice(part * half, (part + 1) * half)
        q = q_ref[rows, :]
        s_lat = _dot_nt(q, k) * scale + bias_ref[rows, :]
        s_ctx = _dot_nt(q, kc) * scale
        o_ref[rows, :] = _softmax_pv([s_lat, s_ctx], [v, vc]).astype(BF16)


def _na_ctx_body(q_ref, kc_ref, vc_ref, o_ref):
    s = _dot_nt(q_ref[...], kc_ref[...]) * (NA_HEAD_DIM ** -0.5)
    o_ref[...] = _softmax_pv([s], [vc_ref[...]]).astype(BF16)


def _na_bias_tiles(rpb):
    n_dr = 2 * NA_WIN_ROWS - 1
    sel_c = np.zeros((GRID_W, GRID_W, 2 * NA_WIN_COLS - 1), np.float32)
    ok_c = np.zeros((GRID_W, GRID_W), bool)
    for qc in range(GRID_W):
        w0 = int(np.clip(qc - NA_WIN_COLS // 2, 0, GRID_W - NA_WIN_COLS))
        for kc in range(GRID_W):
            if w0 <= kc < w0 + NA_WIN_COLS:
                ok_c[qc, kc] = True
                sel_c[qc, kc, int(np.clip(kc - qc + NA_WIN_COLS - 1, 0, 2 * NA_WIN_COLS - 2))] = 1.0
    t = jnp.einsum('xyc,hac->haxy', jnp.asarray(sel_c), rpb.astype(F32), precision=lax.Precision.HIGHEST)
    t = jnp.where(jnp.asarray(ok_c)[None, None], t, NEG_INF)
    t = jnp.concatenate([t, jnp.full((NA_HEADS, 1, GRID_W, GRID_W), NEG_INF, F32)], axis=1)
    assert t.shape[1] == n_dr + 1
    return jnp.concatenate([t, t], axis=-1)


def _na_lat(p, tiles):
    qrows = NA_QROWS * GRID_W
    n_rb = SEQ // qrows
    return pl.pallas_call(
        _na_lat_body,
        grid=(NA_HEADS, n_rb, BATCH),
        in_specs=[pl.BlockSpec((qrows, LANES), lambda h, rb, b: (b * n_rb + rb, CB_NA_Q + h)),
                  pl.BlockSpec((SEQ, LANES), lambda h, rb, b: (b, CB_NA_K + h)),
                  pl.BlockSpec((SEQ, LANES), lambda h, rb, b: (b, CB_NA_V + h)),
                  pl.BlockSpec((CTX_LEN, LANES), lambda h, rb, b: (CTX_BLK0 + b, CB_NA_K + h)),
                  pl.BlockSpec((CTX_LEN, LANES), lambda h, rb, b: (CTX_BLK0 + b, CB_NA_V + h)),
                  pl.BlockSpec((None, 2 * NA_WIN_ROWS, GRID_W, LANES), lambda h, rb, b: (h, 0, 0, 0))],
        out_specs=pl.BlockSpec((qrows, LANES), lambda h, rb, b: (b * n_rb + rb, h)),
        out_shape=jax.ShapeDtypeStruct((T_LAT, MIX_W), BF16),
        scratch_shapes=[pltpu.VMEM((qrows, NA_KROWS * GRID_W), F32)],
        compiler_params=_cparams(("arbitrary",) * 3),
    )(p, p, p, p, p, tiles)


def _na_ctx(p):
    return pl.pallas_call(
        _na_ctx_body,
        grid=(BATCH, NA_HEADS),
        in_specs=[pl.BlockSpec((CTX_LEN, LANES), lambda b, h: (CTX_BLK0 + b, CB_NA_Q + h)),
                  pl.BlockSpec((CTX_LEN, LANES), lambda b, h: (CTX_BLK0 + b, CB_NA_K + h)),
                  pl.BlockSpec((CTX_LEN, LANES), lambda b, h: (CTX_BLK0 + b, CB_NA_V + h))],
        out_specs=pl.BlockSpec((CTX_LEN, LANES), lambda b, h: (b, h)),
        out_shape=jax.ShapeDtypeStruct((T_CTX, MIX_W), BF16),
        compiler_params=_cparams(("arbitrary",) * 2),
    )(p, p, p)


def _log_sigmoid(z):
    return jnp.minimum(z, 0.0) - jnp.log1p(jnp.exp(-jnp.abs(z)))


GLA_GROUP = 256


def _gla_body(q_ref, k_ref, v_ref, r_ref, lr_ref, wgf_ref, wgb_ref, bgf_ref, bgb_ref, gain_ref, s0_ref,
              y_ref, sfin_ref, la_scr, o_scr, qh_scr, ut_scr, sp_scr, g_scr, st_scr, *, n):
    c = GLA_CHUNK
    nc = n // c
    grp = GLA_GROUP
    cpg = grp // c
    lr = lr_ref[...].astype(BF16)
    la_scr[0] = _log_sigmoid(_dot(lr, wgf_ref[...]) + bgf_ref[...]) / GLA_TAU
    la_scr[1] = _log_sigmoid(_dot(lr, wgb_ref[...]) + bgb_ref[...]) / GLA_TAU
    o_scr[...] = jnp.zeros_like(o_scr)

    lane = lax.broadcasted_iota(jnp.int32, (grp, LANES), 1)
    head_mask = (lane < GLA_KEY_DIM, lane >= GLA_KEY_DIM)
    ri = lax.broadcasted_iota(jnp.int32, (grp, grp), 0)
    ci = lax.broadcasted_iota(jnp.int32, (grp, grp), 1)
    same_chunk = lax.shift_right_logical(ri, 6) == lax.shift_right_logical(ci, 6)
    assert c == 1 << 6
    tri_mask = (same_chunk & (ri >= ci), same_chunk & (ci >= ri))
    tri_mat = tuple(jnp.where(m, 1.0, 0.0).astype(BF16) for m in tri_mask)
    chunk_ones = jnp.where(same_chunk, 1.0, 0.0).astype(BF16)
    scale = GLA_KEY_DIM ** -0.5

    def group(gi, carry):
        r = pl.ds(pl.multiple_of(gi * grp, grp), grp)
        q = q_ref[r, :].astype(F32)
        k = k_ref[r, :].astype(F32)
        for d in range(2):
            la = la_scr[d, r, :]
            la_hi = la.astype(BF16)
            la_lo = (la - la_hi.astype(F32)).astype(BF16)
            cum = _dot(tri_mat[d], la_hi) + _dot(tri_mat[d], la_lo)
            tot = _dot(chunk_ones, la_hi) + _dot(chunk_ones, la_lo)
            qf = q * jnp.exp(cum) * scale
            k_in = (k * jnp.exp(-cum)).astype(BF16)
            k_end = k * jnp.exp(tot - cum)
            g = jnp.exp(tot)
            for ch in range(cpg):
                g_scr[d, pl.ds(gi * cpg + ch, 1), :] = g[ch * c:ch * c + 1, :]
            for hh in range(2):
                chain = hh * 2 + d
                cols = slice(hh * GLA_VAL_DIM, (hh + 1) * GLA_VAL_DIM)
                qh = jnp.where(head_mask[hh], qf, 0.0).astype(BF16)
                qh_scr[chain, r, :] = qh
                a = jnp.where(tri_mask[d], _dot_nt(qh, k_in), 0.0)
                vh = v_ref[r, cols]
                o_scr[r, cols] = o_scr[r, cols] + _dot(a.astype(BF16), vh)
                kh = jnp.where(head_mask[hh], k_end, 0.0).astype(BF16)
                for ch in range(cpg):
                    rows = slice(ch * c, (ch + 1) * c)
                    ut_scr[chain, gi * cpg + ch] = _dot_tn(vh[rows], kh[rows])
        return carry

    lax.fori_loop(0, n // grp, group, 0)

    st_scr[...] = s0_ref[...]

    def scan(i, carry):
        for d in range(2):
            cc = i if d == 0 else nc - 1 - i
            g = g_scr[d, pl.ds(cc, 1), :]
            for hh in range(2):
                chain = hh * 2 + d
                st = st_scr[chain]
                sp_scr[chain, cc] = st.astype(BF16)
                st_scr[chain] = g * st + ut_scr[chain, cc]
        return carry

    lax.fori_loop(0, nc, scan, 0)
    sfin_ref[...] = st_scr[...]

    def inter(cc, carry):
        r = pl.ds(pl.multiple_of(cc * c, c), c)
        for hh in range(2):
            cols = slice(hh * GLA_VAL_DIM, (hh + 1) * GLA_VAL_DIM)
            o = (_dot_nt(qh_scr[hh * 2, r, :], sp_scr[hh * 2, cc])
                 + _dot_nt(qh_scr[hh * 2 + 1, r, :], sp_scr[hh * 2 + 1, cc]))
            o_scr[r, cols] = o_scr[r, cols] + o
        return carry

    lax.fori_loop(0, nc, inter, 0, unroll=4)
    for hh in range(2):
        cols = slice(hh * GLA_VAL_DIM, (hh + 1) * GLA_VAL_DIM)
        rr = r_ref[:, cols].astype(F32)
        y_ref[:, cols] = (_rms_rows(o_scr[:, cols], gain_ref[...]) * (rr * _sigmoid(rr))).astype(BF16)


def _gla(p, lr, wg_f, wg_b, bg_f, bg_b, gain, s0, *, ctx):
    n = CTX_LEN if ctx else SEQ
    blk0 = CTX_BLK0 if ctx else 0
    pairs = GLA_HEADS // 2
    st_spec = pl.BlockSpec((None, None, 4, GLA_VAL_DIM, LANES), lambda b, pr: (b, pr, 0, 0, 0))
    return pl.pallas_call(
        functools.partial(_gla_body, n=n),
        grid=(BATCH, pairs),
        in_specs=[pl.BlockSpec((n, LANES), lambda b, pr: (blk0 + b, CB_GLA_Q + pr)),
                  pl.BlockSpec((n, LANES), lambda b, pr: (blk0 + b, CB_GLA_K + pr)),
                  pl.BlockSpec((n, 2 * LANES), lambda b, pr: (blk0 + b, CB_GLA_V // 2 + pr)),
                  pl.BlockSpec((n, 2 * LANES), lambda b, pr: (blk0 + b, CB_GLA_R // 2 + pr)),
                  pl.BlockSpec((n, LANES), lambda b, pr: (blk0 + b, 0)),
                  pl.BlockSpec((LANES, LANES), lambda b, pr: (0, pr)),
                  pl.BlockSpec((LANES, LANES), lambda b, pr: (0, pr)),
                  pl.BlockSpec((1, LANES), lambda b, pr: (0, pr)),
                  pl.BlockSpec((1, LANES), lambda b, pr: (0, pr)),
                  pl.BlockSpec((1, GLA_VAL_DIM), lambda b, pr: (0, 0)),
                  st_spec],
        out_specs=[pl.BlockSpec((n, 2 * LANES), lambda b, pr: (b, pr)), st_spec],
        out_shape=[jax.ShapeDtypeStruct((BATCH * n, MIX_W), BF16),
                   jax.ShapeDtypeStruct((BATCH, pairs, 4, GLA_VAL_DIM, LANES), F32)],
        scratch_shapes=[pltpu.VMEM((2, n, LANES), F32),
                        pltpu.VMEM((n, 2 * LANES), F32),
                        pltpu.VMEM((4, n, LANES), BF16),
                        pltpu.VMEM((4, n // GLA_CHUNK, GLA_VAL_DIM, LANES), F32),
                        pltpu.VMEM((4, n // GLA_CHUNK, GLA_VAL_DIM, LANES), BF16),
                        pltpu.VMEM((2, max(n // GLA_CHUNK, SUBLANES), LANES), F32),
                        pltpu.VMEM((4, GLA_VAL_DIM, LANES), F32)],
        compiler_params=_cparams(("arbitrary",) * 2),
    )(p, p, p, p, lr, wg_f, wg_b, bg_f, bg_b, gain, s0)


def _gla_gate_weights(w_gate, b_gate):
    wf = jnp.zeros((LANES, GLA_QK_W), F32).at[:GLA_RANK].set(w_gate[0])
    wb = jnp.zeros((LANES, GLA_QK_W), F32).at[GLA_RANK:2 * GLA_RANK].set(w_gate[1])
    return wf.astype(BF16), wb.astype(BF16), b_gate[0:1], b_gate[1:2]


def _dft_tables(n):
    def cs(m):
        idx = (np.arange(m)[:, None] * np.arange(m)[None, :]) % m
        ang = 2.0 * np.pi * idx / m
        return np.cos(ang) / np.sqrt(m), np.sin(ang) / np.sqrt(m)
    cn, sn = cs(n)
    cg, sg = cs(FNET_GROUP_DIM)
    as_bf16 = lambda a: jnp.asarray(a, F32).astype(BF16)
    return as_bf16(cn), as_bf16(sn), as_bf16(np.concatenate([cg, sg], axis=1))


def _fnet_body(u_ref, cn_ref, sn_ref, csg_ref, o_ref):
    gd = FNET_GROUP_DIM
    ts = [_dot(u_ref[:, g * gd:(g + 1) * gd], csg_ref[...]).astype(BF16) for g in range(2)]
    tc = jnp.concatenate([t[:, :gd] for t in ts], axis=1)
    tsn = jnp.concatenate([t[:, gd:] for t in ts], axis=1)
    o_ref[...] = (_dot(cn_ref[...], tc) - _dot(sn_ref[...], tsn)).astype(BF16)


def _fnet(p, *, ctx):
    n = CTX_LEN if ctx else SEQ
    blk0 = CTX_BLK0 if ctx else 0
    cn, sn, csg = _dft_tables(n)
    return pl.pallas_call(
        _fnet_body,
        grid=(BATCH, FNET_GROUPS // 2),
        in_specs=[pl.BlockSpec((n, 2 * LANES), lambda b, g: (blk0 + b, CB_FNET_U // 2 + g)),
                  pl.BlockSpec((n, n), lambda b, g: (0, 0)),
                  pl.BlockSpec((n, n), lambda b, g: (0, 0)),
                  pl.BlockSpec((FNET_GROUP_DIM, 2 * FNET_GROUP_DIM), lambda b, g: (0, 0))],
        out_specs=pl.BlockSpec((n, 2 * LANES), lambda b, g: (b, g)),
        out_shape=jax.ShapeDtypeStruct((BATCH * n, MIX_W), BF16),
        compiler_params=_cparams(("arbitrary",) * 2),
    )(p, cn, sn, csg)


def _rope_tables():
    half = DIFF_QK_DIM // 2
    inv = ROPE_BASE ** (-jnp.arange(0, half, 2, dtype=F32) / half)
    t = jnp.arange(SEQ)
    ang_r = (t // GRID_W).astype(F32)[:, None] * inv
    ang_c = (t % GRID_W).astype(F32)[:, None] * inv
    cos = jnp.concatenate([jnp.cos(ang_r)] * 2 + [jnp.cos(ang_c)] * 2, axis=1)
    sin = jnp.concatenate([jnp.sin(ang_r)] * 2 + [jnp.sin(ang_c)] * 2, axis=1)
    rot = np.zeros((LANES, LANES), np.float32)
    q4 = half // 2
    for base in range(0, LANES, half):
        for j in range(q4):
            rot[base + j + q4, base + j] = -1.0
            rot[base + j, base + j + q4] = 1.0
    return jnp.tile(cos, (1, 2)), jnp.tile(sin, (1, 2)), jnp.asarray(rot, BF16)


def _rope(x, cos, sin, rot):
    return x.astype(F32) * cos + _dot(x, rot) * sin


def _diff_out(q, k_list, v_list, lam, gain, out_scale):
    q = q * (DIFF_QK_DIM ** -0.5)
    lane = lax.broadcasted_iota(jnp.int32, q.shape, 1)
    o = None
    for sub, coef in ((lane < DIFF_QK_DIM, None), (lane >= DIFF_QK_DIM, lam)):
        qs = jnp.where(sub, q, 0.0).astype(BF16)
        os = _softmax_pv([_dot_nt(qs, k) for k in k_list], v_list)
        o = os if coef is None else o - coef * os
    return (_rms_rows(o, gain) * out_scale).astype(BF16)


def _diff_lat_body(q_ref, k_ref, v_ref, kc_ref, vc_ref, cosq_ref, sinq_ref, cosk_ref, sink_ref, rot_ref,
                   lam_ref, gain_ref, o_ref, k_scr, *, out_scale):
    @pl.when(pl.program_id(2) == 0)
    def _():
        k_scr[...] = _rope(k_ref[...], cosk_ref[...], sink_ref[...], rot_ref[...]).astype(BF16)

    for part in range(DIFF_QB // DIFF_SUB):
        rows = slice(part * DIFF_SUB, (part + 1) * DIFF_SUB)
        q = _rope(q_ref[rows, :], cosq_ref[rows, :], sinq_ref[rows, :], rot_ref[...])
        o_ref[rows, :] = _diff_out(q, [kc_ref[...], k_scr[...]], [vc_ref[...], v_ref[...]], lam_ref[:, 0:1],
                                   gain_ref[...], out_scale)


def _diff_ctx_body(q_ref, kc_ref, vc_ref, lam_ref, gain_ref, o_ref, *, out_scale):
    o_ref[...] = _diff_out(q_ref[...].astype(F32), [kc_ref[...]], [vc_ref[...]], lam_ref[:, 0:1],
                           gain_ref[...], out_scale)


def _diff_lat(p, rope, lam_row, gain, lam_init):
    cos, sin, rot = rope
    nqb = SEQ // DIFF_QB
    const = lambda b, h, qb: (0, 0)
    return pl.pallas_call(
        functools.partial(_diff_lat_body, out_scale=1.0 - lam_init),
        grid=(BATCH, DIFF_HEADS, nqb),
        in_specs=[pl.BlockSpec((DIFF_QB, LANES), lambda b, h, qb: (b * nqb + qb, CB_DIFF_Q + h)),
                  pl.BlockSpec((SEQ, LANES), lambda b, h, qb: (b, CB_DIFF_K + h)),
                  pl.BlockSpec((SEQ, LANES), lambda b, h, qb: (b, CB_DIFF_V + h)),
                  pl.BlockSpec((CTX_LEN, LANES), lambda b, h, qb: (CTX_BLK0 + b, CB_DIFF_K + h)),
                  pl.BlockSpec((CTX_LEN, LANES), lambda b, h, qb: (CTX_BLK0 + b, CB_DIFF_V + h)),
                  pl.BlockSpec((DIFF_QB, LANES), lambda b, h, qb: (qb, 0)),
                  pl.BlockSpec((DIFF_QB, LANES), lambda b, h, qb: (qb, 0)),
                  pl.BlockSpec((SEQ, LANES), const),
                  pl.BlockSpec((SEQ, LANES), const),
                  pl.BlockSpec((LANES, LANES), const),
                  pl.BlockSpec((1, LANES), const),
                  pl.BlockSpec((1, DIFF_V_DIM), const)],
        out_specs=pl.BlockSpec((DIFF_QB, LANES), lambda b, h, qb: (b * nqb + qb, h)),
        out_shape=jax.ShapeDtypeStruct((T_LAT, MIX_W), BF16),
        scratch_shapes=[pltpu.VMEM((SEQ, LANES), BF16)],
        compiler_params=_cparams(("arbitrary",) * 3),
    )(p, p, p, p, p, cos, sin, cos, sin, rot, lam_row, gain)


def _diff_ctx(p, lam_row, gain, lam_init):
    const = lambda b, h: (0, 0)
    return pl.pallas_call(
        functools.partial(_diff_ctx_body, out_scale=1.0 - lam_init),
        grid=(BATCH, DIFF_HEADS),
        in_specs=[pl.BlockSpec((CTX_LEN, LANES), lambda b, h: (CTX_BLK0 + b, CB_DIFF_Q + h)),
                  pl.BlockSpec((CTX_LEN, LANES), lambda b, h: (CTX_BLK0 + b, CB_DIFF_K + h)),
                  pl.BlockSpec((CTX_LEN, LANES), lambda b, h: (CTX_BLK0 + b, CB_DIFF_V + h)),
                  pl.BlockSpec((1, LANES), const),
                  pl.BlockSpec((1, DIFF_V_DIM), const)],
        out_specs=pl.BlockSpec((CTX_LEN, LANES), lambda b, h: (b, h)),
        out_shape=jax.ShapeDtypeStruct((T_CTX, MIX_W), BF16),
        compiler_params=_cparams(("arbitrary",) * 2),
    )(p, p, p, lam_row, gain)


MERGE_TM = 512
N_LAT_TILES = T_LAT // MERGE_TM


def _merge_body(*refs, with_ctx):
    gates = refs[0:N_BRANCH]
    y_lat = refs[N_BRANCH:2 * N_BRANCH]
    y_ctx = refs[2 * N_BRANCH:3 * N_BRANCH] if with_ctx else None
    wb_ref, m_ref = refs[-2], refs[-1]
    is_lat = pl.program_id(0) < N_LAT_TILES
    acc = None
    for i in range(N_BRANCH):
        y = y_lat[i][...]
        if with_ctx:
            y = jnp.where(is_lat, y, y_ctx[i][...])
        t = _sigmoid(gates[i][...].astype(F32)) * _dot(y, wb_ref[i])
        acc = t if acc is None else acc + t
    m_ref[...] = acc.astype(BF16)


def _merge(p, ys_lat, ys_ctx, w_branch):
    with_ctx = ys_ctx is not None
    t_rows = T_ALL if with_ctx else T_LAT
    gate_specs = [pl.BlockSpec((MERGE_TM, D_MODEL), functools.partial(lambda i, g: (i, g), g=g))
                  for g in range(N_BRANCH)]
    y_specs = [pl.BlockSpec((MERGE_TM, MIX_W), lambda i: (jnp.minimum(i, N_LAT_TILES - 1), 0))
               for _ in range(N_BRANCH)]
    ys = list(ys_lat)
    if with_ctx:
        y_specs += [pl.BlockSpec((MERGE_TM, MIX_W), lambda i: (jnp.maximum(i - N_LAT_TILES, 0), 0))
                    for _ in range(N_BRANCH)]
        ys += list(ys_ctx)
    return pl.pallas_call(
        functools.partial(_merge_body, with_ctx=with_ctx),
        grid=(t_rows // MERGE_TM,),
        in_specs=gate_specs + y_specs + [pl.BlockSpec((N_BRANCH, MIX_W, D_MODEL), lambda i: (0, 0, 0),
                                                      pipeline_mode=pl.Buffered(1))],
        out_specs=pl.BlockSpec((MERGE_TM, D_MODEL), lambda i: (i, 0)),
        out_shape=jax.ShapeDtypeStruct((t_rows, D_MODEL), BF16),
        compiler_params=_cparams(("arbitrary",)),
    )(p, p, p, p, *ys, w_branch)


def _residual_ln(x, gate, f, g, b):
    return _ln_rows(DEEPNORM_ALPHA * x + gate * f) * g + b


def _outproj_body(*refs, split_input):
    if split_input:
        m_ref, x_ref, xc_ref, gate_ref, w_ref, g_ref, b_ref, o_ref = refs
        x = jnp.where(pl.program_id(0) < N_LAT_TILES, x_ref[...], xc_ref[...])
    else:
        m_ref, x_ref, gate_ref, w_ref, g_ref, b_ref, o_ref = refs
        x = x_ref[...]
    f = _dot(m_ref[...], w_ref[...])
    o_ref[...] = _residual_ln(x, gate_ref[...], f, g_ref[...], b_ref[...])


def _outproj(m, x_lat, x_ctx, mod, layer, w_out, ln_g, ln_b, t_rows):
    split_input = x_ctx is not None
    brow = _mod_row(MERGE_TM)
    vec = pl.BlockSpec((1, D_MODEL), lambda i: (0, 0))
    rows = pl.BlockSpec((MERGE_TM, D_MODEL), lambda i: (i, 0))
    x_specs, xs = [rows], [x_lat]
    if split_input:
        x_specs = [pl.BlockSpec((MERGE_TM, D_MODEL), lambda i: (jnp.minimum(i, N_LAT_TILES - 1), 0)),
                   pl.BlockSpec((MERGE_TM, D_MODEL), lambda i: (jnp.maximum(i - N_LAT_TILES, 0), 0))]
        xs = [x_lat, x_ctx]
    return pl.pallas_call(
        functools.partial(_outproj_body, split_input=split_input),
        grid=(t_rows // MERGE_TM,),
        in_specs=[rows] + x_specs
        + [pl.BlockSpec((None, 1, D_MODEL), lambda i: (layer * SUBLANES + brow(i), 0, 2)),
           pl.BlockSpec((D_MODEL, D_MODEL), lambda i: (0, 0), pipeline_mode=pl.Buffered(1)), vec, vec],
        out_specs=rows,
        out_shape=jax.ShapeDtypeStruct((t_rows, D_MODEL), F32),
        compiler_params=_cparams(("arbitrary",)),
    )(m, *xs, mod, w_out, ln_g, ln_b)


RT_E1, RT_E2, RT_W1, RT_W2, RT_RANK1, RT_RANK2 = range(6)


def _first_lane_of_max(vals, lane_f):
    m = vals.max(-1, keepdims=True)
    idx = jnp.where(vals == m, lane_f, float(LANES)).min(-1, keepdims=True)
    return m, idx


def _router_body(x_ref, sh_ref, sc_ref, wr_ref, br_ref, h_ref, route_ref, cnt_ref, run_scr):
    @pl.when(pl.program_id(0) == 0)
    def _():
        run_scr[...] = jnp.zeros_like(run_scr)

    tm = ROW_TM
    h = _ln_rows(x_ref[...]) * (1.0 + sc_ref[...]) + sh_ref[...]
    hb = h.astype(BF16)
    for s in range(ROW_SLABS):
        h_ref[pl.ds(s, tm, stride=ROW_SLABS), :] = hb[:, s * LANES:(s + 1) * LANES].astype(F32)
    logits = _dot(hb, wr_ref[...]) + br_ref[...]
    lane = lax.broadcasted_iota(jnp.int32, (tm, LANES), 1)
    lane_f = lane.astype(F32)
    is_group = lane < N_GROUPS
    g_logits = jnp.where(is_group, logits, NEG_INF)
    g_max, g_sel = _first_lane_of_max(g_logits, lane_f)
    g_w = 1.0 / jnp.where(is_group, jnp.exp(logits - g_max), 0.0).sum(-1, keepdims=True)
    lo = N_GROUPS + EXPERTS_PER_GROUP * g_sel
    e_logits = jnp.where((lane_f >= lo) & (lane_f < lo + EXPERTS_PER_GROUP), logits, NEG_INF)
    v1, i1 = _first_lane_of_max(e_logits, lane_f)
    v2, i2 = _first_lane_of_max(jnp.where(lane_f == i1, NEG_INF, e_logits), lane_f)
    t = jnp.exp(v2 - v1)
    w1 = g_w / (1.0 + t)
    w2 = g_w * t / (1.0 + t)
    e1 = i1 - N_GROUPS
    e2 = i2 - N_GROUPS

    oh1 = lane_f == e1
    oh2 = lane_f == e2
    both = jnp.where(oh1 | oh2, 1.0, 0.0)
    ri = lax.broadcasted_iota(jnp.int32, (tm, tm), 0)
    ci = lax.broadcasted_iota(jnp.int32, (tm, tm), 1)
    before = jnp.where(ci < ri, 1.0, 0.0).astype(BF16)
    excl = _dot(before, both.astype(BF16)) + run_scr[0:1, :]
    rank1 = jnp.where(oh1, excl, 0.0).sum(-1, keepdims=True)
    rank2 = jnp.where(oh2, excl, 0.0).sum(-1, keepdims=True)
    run_scr[...] = run_scr[...] + both.sum(0, keepdims=True)
    cnt_ref[...] = run_scr[...]

    rec = jnp.zeros((tm, LANES), F32)
    for idx, val in ((RT_E1, e1), (RT_E2, e2), (RT_W1, w1), (RT_W2, w2), (RT_RANK1, rank1), (RT_RANK2, rank2)):
        rec = jnp.where(lane == idx, val, rec)
    route_ref[...] = rec


def _router(x1, mod, layer, w_r, b_r, t_rows):
    brow = _mod_row(ROW_TM)
    mrow = lambda i: layer * SUBLANES + brow(i)
    return pl.pallas_call(
        _router_body,
        grid=(t_rows // ROW_TM,),
        in_specs=[pl.BlockSpec((ROW_TM, D_MODEL), lambda i: (i, 0)),
                  pl.BlockSpec((None, 1, D_MODEL), lambda i: (mrow(i), 0, 3)),
                  pl.BlockSpec((None, 1, D_MODEL), lambda i: (mrow(i), 0, 4)),
                  pl.BlockSpec((D_MODEL, LANES), lambda i: (0, 0)),
                  pl.BlockSpec((1, LANES), lambda i: (0, 0))],
        out_specs=[pl.BlockSpec((ROW_TM * ROW_SLABS, LANES), lambda i: (i, 0)),
                   pl.BlockSpec((ROW_TM, LANES), lambda i: (i, 0)),
                   pl.BlockSpec((SUBLANES, LANES), lambda i: (0, 0))],
        out_shape=[jax.ShapeDtypeStruct((t_rows * ROW_SLABS, LANES), F32),
                   jax.ShapeDtypeStruct((t_rows, LANES), F32),
                   jax.ShapeDtypeStruct((SUBLANES, LANES), F32)],
        scratch_shapes=[pltpu.VMEM((SUBLANES, LANES), F32)],
        compiler_params=_cparams(("arbitrary",)),
    )(x1, mod, mod, w_r, b_r)


def _router_weights(w_group, b_group, w_router, b_router):
    wr = jnp.concatenate([w_group, jnp.transpose(w_router, (1, 0, 2)).reshape(D_MODEL, N_EXPERTS)], axis=1)
    br = jnp.concatenate([b_group, b_router.reshape(N_EXPERTS)])
    pad = LANES - wr.shape[1]
    return jnp.pad(wr, ((0, 0), (0, pad))).astype(BF16), jnp.pad(br, (0, pad)).reshape(1, LANES)


def _token_rows(ref, tok):
    return ref.at[pl.ds(pl.multiple_of(tok * ROW_SLABS, ROW_SLABS), ROW_SLABS), :]


DMA_UNROLL = 8


def _dispatch_body(pos_ref, h_ref, xs_in_ref, xs_ref, sem, *, t_rows):
    del xs_in_ref
    tt = ROW_TM
    base = pl.program_id(0) * tt

    def issue(j, carry):
        for k in range(2):
            pltpu.make_async_copy(_token_rows(h_ref, j), _token_rows(xs_ref, pos_ref[k * t_rows + base + j]),
                                  sem).start(priority=k)
        return carry

    def drain(j, carry):
        pltpu.make_async_copy(_token_rows(h_ref, 0), _token_rows(xs_ref, 0), sem).wait()
        return carry

    lax.fori_loop(0, tt, issue, 0, unroll=DMA_UNROLL)
    lax.fori_loop(0, 2 * tt, drain, 0, unroll=DMA_UNROLL)


def _dispatch(pos, h, xs0):
    t_rows = h.shape[0] // ROW_SLABS
    return pl.pallas_call(
        functools.partial(_dispatch_body, t_rows=t_rows),
        grid_spec=pltpu.PrefetchScalarGridSpec(
            num_scalar_prefetch=1,
            grid=(t_rows // ROW_TM,),
            in_specs=[pl.BlockSpec((ROW_TM * ROW_SLABS, LANES), lambda i, pos: (i, 0)),
                      pl.BlockSpec(memory_space=pl.ANY)],
            out_specs=pl.BlockSpec(memory_space=pl.ANY),
            scratch_shapes=[pltpu.SemaphoreType.DMA(())]),
        out_shape=jax.ShapeDtypeStruct(xs0.shape, F32),
        input_output_aliases={2: 0},
        compiler_params=_cparams(("arbitrary",)),
    )(pos, h, xs0)


def _expert_body(te_ref, nu_ref, nxt_ref, slot_ref, xs_ref, wgu_hbm, wd_hbm, ys_ref,
                 wgu_f, wd_f, wgu_b, wd_b, sem, *, layer):
    i = pl.program_id(0)
    tm = EXPERT_TM
    used = i < nu_ref[0]

    def fetch(expert, sl):
        e = layer * N_EXPERTS + expert
        return (pltpu.make_async_copy(wgu_hbm.at[e], wgu_f.at[sl], sem.at[0, sl]),
                pltpu.make_async_copy(wd_hbm.at[e], wd_f.at[sl], sem.at[1, sl]))

    @pl.when(used)
    def _():
        @pl.when((i == 0) | (te_ref[i] != te_ref[jnp.maximum(i - 1, 0)]))
        def _():
            sl = slot_ref[i]

            @pl.when(i == 0)
            def _():
                for cp in fetch(te_ref[0], 0):
                    cp.start()

            @pl.when(nxt_ref[i] >= 0)
            def _():
                for cp in fetch(nxt_ref[i], 1 - sl):
                    cp.start()

            for cp in fetch(te_ref[i], sl):
                cp.wait()

            def cast(c, carry):
                r = pl.ds(pl.multiple_of(c * ROW_TM, ROW_TM), ROW_TM)
                wgu_b[r, :] = wgu_f[sl, r, :].astype(BF16)
                return carry
            lax.fori_loop(0, D_MODEL // ROW_TM, cast, 0)
            wd_b[...] = wd_f[sl].astype(BF16)

        x = jnp.concatenate([xs_ref[pl.ds(s, tm, stride=ROW_SLABS), :].astype(BF16) for s in range(ROW_SLABS)],
                            axis=1)
        hgu = _dot(x, wgu_b[...])
        hg = hgu[:, :EXPERT_DIM]
        hu = hgu[:, EXPERT_DIM:]
        act = (hg * _sigmoid(hg) * hu).astype(BF16)
        y = _dot(act, wd_b[...])
        for s in range(ROW_SLABS):
            ys_ref[pl.ds(s, tm, stride=ROW_SLABS), :] = y[:, s * LANES:(s + 1) * LANES]

    @pl.when(jnp.logical_not(used))
    def _():
        ys_ref[...] = jnp.zeros_like(ys_ref)


def _experts(tile_expert, n_used, next_expert, tile_slot, xs, w_gu, w_down, layer):
    n_tiles = xs.shape[0] // (EXPERT_TM * ROW_SLABS)
    blk = EXPERT_TM * ROW_SLABS
    return pl.pallas_call(
        functools.partial(_expert_body, layer=layer),
        grid_spec=pltpu.PrefetchScalarGridSpec(
            num_scalar_prefetch=4,
            grid=(n_tiles,),
            in_specs=[pl.BlockSpec((blk, LANES), lambda i, te, nu, nx, sl: (jnp.minimum(i, nu[0] - 1), 0)),
                      pl.BlockSpec(memory_space=pl.ANY),
                      pl.BlockSpec(memory_space=pl.ANY)],
            out_specs=pl.BlockSpec((blk, LANES), lambda i, te, nu, nx, sl: (i, 0)),
            scratch_shapes=[pltpu.VMEM((2, D_MODEL, 2 * EXPERT_DIM), F32),
                            pltpu.VMEM((2, EXPERT_DIM, D_MODEL), F32),
                            pltpu.VMEM((D_MODEL, 2 * EXPERT_DIM), BF16),
                            pltpu.VMEM((EXPERT_DIM, D_MODEL), BF16),
                            pltpu.SemaphoreType.DMA((2, 2))]),
        out_shape=jax.ShapeDtypeStruct(xs.shape, F32),
        compiler_params=_cparams(("arbitrary",)),
    )(tile_expert, n_used, next_expert, tile_slot, xs, w_gu, w_down)


GATHER_PITCH = ROW_SLABS + SUBLANES


def _final_body(pos_ref, x_ref, ys_ref, route_ref, gate_ref, g_ref, b_ref, out_ref, buf, sem, *, t_rows):
    tm = ROW_TM
    i = pl.program_id(0)
    slot = lax.rem(i, 2)

    def gather_copy(tile, sl, t, k):
        dst = buf.at[sl, pl.ds(pl.multiple_of((k * tm + t) * GATHER_PITCH, SUBLANES), ROW_SLABS), :]
        return pltpu.make_async_copy(_token_rows(ys_ref, pos_ref[k * t_rows + tile * tm + t]), dst, sem.at[sl])

    def start_tile(tile, sl):
        def issue(t, carry):
            for k in range(2):
                gather_copy(tile, sl, t, k).start(priority=k)
            return carry
        lax.fori_loop(0, tm, issue, 0, unroll=DMA_UNROLL // 2)

    @pl.when(i == 0)
    def _():
        start_tile(0, 0)

    @pl.when(i + 1 < pl.num_programs(0))
    def _():
        start_tile(i + 1, 1 - slot)

    def drain(t, carry):
        for k in range(2):
            gather_copy(i, slot, 0, k).wait()
        return carry
    lax.fori_loop(0, tm, drain, 0, unroll=DMA_UNROLL // 2)

    rt = route_ref[...]
    w1 = rt[:, RT_W1:RT_W1 + 1]
    w2 = rt[:, RT_W2:RT_W2 + 1]
    f = jnp.concatenate(
        [w1 * buf[slot, pl.ds(s, tm, stride=GATHER_PITCH), :]
         + w2 * buf[slot, pl.ds(tm * GATHER_PITCH + s, tm, stride=GATHER_PITCH), :] for s in range(ROW_SLABS)],
        axis=1)
    out_ref[...] = _residual_ln(x_ref[...], gate_ref[...], f, g_ref[...], b_ref[...])


def _final(pos, x1, ys, route, mod, layer, ln_g, ln_b, t_rows):
    brow = _mod_row(ROW_TM)
    vec = pl.BlockSpec((1, D_MODEL), lambda i, pos: (0, 0))
    return pl.pallas_call(
        functools.partial(_final_body, t_rows=t_rows),
        grid_spec=pltpu.PrefetchScalarGridSpec(
            num_scalar_prefetch=1,
            grid=(t_rows // ROW_TM,),
            in_specs=[pl.BlockSpec((ROW_TM, D_MODEL), lambda i, pos: (i, 0)),
                      pl.BlockSpec(memory_space=pl.ANY),
                      pl.BlockSpec((ROW_TM, LANES), lambda i, pos: (i, 0)),
                      pl.BlockSpec((None, 1, D_MODEL), lambda i, pos: (layer * SUBLANES + brow(i), 0, 5)),
                      vec, vec],
            out_specs=pl.BlockSpec((ROW_TM, D_MODEL), lambda i, pos: (i, 0)),
            scratch_shapes=[pltpu.VMEM((2, 2 * ROW_TM * GATHER_PITCH, LANES), F32),
                            pltpu.SemaphoreType.DMA((2,))]),
        out_shape=jax.ShapeDtypeStruct((t_rows, D_MODEL), F32),
        compiler_params=_cparams(("arbitrary",)),
    )(pos, x1, ys, route, mod, ln_g, ln_b)


MOE_TILES = -(-(2 * T_ALL + N_EXPERTS * (EXPERT_TM - 1)) // EXPERT_TM)


def _moe(x1, mod, layer, router_w, w_gu, w_down, ln_g, ln_b, t_rows, slot_buf):
    h, route, cnt = _router(x1, mod, layer, *router_w, t_rows)
    n_tiles = MOE_TILES
    if slot_buf is None:
        slot_buf = jnp.zeros((n_tiles * EXPERT_TM * ROW_SLABS, LANES), F32)
    counts = cnt[0, :N_EXPERTS].astype(jnp.int32)
    tiles_per = (counts + EXPERT_TM - 1) // EXPERT_TM
    tile_end = jnp.cumsum(tiles_per)
    tile_start = tile_end - tiles_per
    n_used = tile_end[-1:]
    route_t = route.T
    e = route_t[RT_E1:RT_E2 + 1].astype(jnp.int32)
    rank = route_t[RT_RANK1:RT_RANK2 + 1].astype(jnp.int32)
    ids = jnp.arange(N_EXPERTS, dtype=jnp.int32)

    def lookup(table, idx):
        hit = idx[None] == ids.reshape((N_EXPERTS,) + (1,) * idx.ndim)
        return jnp.sum(jnp.where(hit, table.reshape((N_EXPERTS,) + (1,) * idx.ndim), 0), axis=0)

    pos = (lookup(tile_start, e) * EXPERT_TM + rank).reshape(-1)
    ti = jnp.minimum(jnp.arange(n_tiles, dtype=jnp.int32), n_used[0] - 1)
    tile_expert = jnp.sum((ti[:, None] >= tile_end[None, :]).astype(jnp.int32), axis=1)
    owns = tiles_per > 0
    later = jnp.where(owns[None, :] & (ids[None, :] > ids[:, None]), ids[None, :], N_EXPERTS)
    nxt_e = jnp.min(later, axis=1)
    nxt_e = jnp.where(nxt_e == N_EXPERTS, -1, nxt_e)
    order_e = jnp.cumsum(owns.astype(jnp.int32)) - 1
    next_expert = lookup(nxt_e, tile_expert)
    tile_slot = lookup(order_e, tile_expert) % 2
    xs = _dispatch(pos, h, slot_buf)
    ys = _experts(tile_expert, n_used, next_expert, tile_slot, xs, w_gu, w_down, layer)
    return _final(pos, x1, ys, route, mod, layer, ln_g, ln_b, t_rows), ys


def _prep_w_in(w):
    offs = np.concatenate([[0], np.cumsum(ALL_SPLITS)])
    piece = lambda i: w[:, int(offs[i]):int(offs[i + 1])]
    w_main = jnp.concatenate([piece(i) for i in _PIECE_ORDER], axis=1).astype(BF16)
    w_lr = jnp.pad(piece(_LR_PIECE), ((0, 0), (0, LANES - 2 * GLA_RANK))).astype(BF16)
    return w_main, w_lr


def kernel(x, c, ctx, c_ctx, w_mod, b_mod, w_in, na_rpb, gla_w_gate, gla_b_gate, gla_norm, diff_lambda, diff_norm,
           w_branch, w_out, ln1_g, ln1_b, ln2_g, ln2_b, w_group, b_group, w_router, b_router, w_gu, w_down):
    x_lat, x_ctx = x.reshape(T_LAT, D_MODEL), ctx.reshape(T_CTX, D_MODEL)
    c8 =jnp.concatenate([c, c_ctx[None], jnp.zeros((SUBLANES - BATCH - 1, D_MODEL), F32)], axis=0)
    mod = _modulation(c8, w_mod, b_mod).reshape(DEPTH * SUBLANES, 1, 6 * D_MODEL)
    rope = _rope_tables()
    w_gu_flat = w_gu.reshape(DEPTH * N_EXPERTS, D_MODEL, 2 * EXPERT_DIM)
    w_down_flat = w_down.reshape(DEPTH * N_EXPERTS, EXPERT_DIM, D_MODEL)
    s_zero = jnp.zeros((BATCH, GLA_HEADS // 2, 4, GLA_VAL_DIM, LANES), F32)
    slot_buf = None

    for l in range(DEPTH):
        last = l == DEPTH - 1
        t_rows = T_LAT if last else T_ALL
        lam_init = 0.8 - 0.6 * math.exp(-0.3 * l)
        lq1, lk1, lq2, lk2 = diff_lambda[l].astype(F32)
        lam = jnp.exp(jnp.sum(lq1 * lk1)) - jnp.exp(jnp.sum(lq2 * lk2)) + lam_init
        lam_row = jnp.full((1, LANES), lam, F32)

        w_main, w_lr = _prep_w_in(w_in[l])
        p, lr = _projection(x_lat, x_ctx, mod, l, w_main, w_lr, skip_ctx_gates=last)

        gla_w = _gla_gate_weights(gla_w_gate[l], gla_b_gate[l])
        gla_gain = gla_norm[l].reshape(1, GLA_VAL_DIM)
        diff_gain = diff_norm[l].reshape(1, DIFF_V_DIM)
        y_gla_ctx, states = _gla(p, lr, *gla_w, gla_gain, s_zero, ctx=True)
        y_gla, _ = _gla(p, lr, *gla_w, gla_gain, states, ctx=False)
        ys_lat = (_na_lat(p, _na_bias_tiles(na_rpb[l])), y_gla, _fnet(p, ctx=False),
                  _diff_lat(p, rope, lam_row, diff_gain, lam_init))
        ys_ctx = None
        if not last:
            ys_ctx = (_na_ctx(p), y_gla_ctx, _fnet(p, ctx=True), _diff_ctx(p, lam_row, diff_gain, lam_init))

        m = _merge(p, ys_lat, ys_ctx, w_branch[l].astype(BF16))
        x1 = _outproj(m, x_lat, x_ctx, mod, l, w_out[l].astype(BF16), ln1_g[l].reshape(1, -1),
                      ln1_b[l].reshape(1, -1), t_rows)
        router_w = _router_weights(w_group[l], b_group[l], w_router[l], b_router[l])
        x_lat, slot_buf = _moe(x1, mod, l, router_w, w_gu_flat, w_down_flat, ln2_g[l].reshape(1, -1),
                               ln2_b[l].reshape(1, -1), t_rows, slot_buf)
        x_ctx = None
    return x_lat.reshape(BATCH, SEQ, D_MODEL)
```

```python
import functools
import math

import numpy as np
import jax
import jax.numpy as jnp
from jax import lax
from jax.experimental import pallas as pl
from jax.experimental.pallas import tpu as pltpu

F32 = jnp.float32
BF16 = jnp.bfloat16

D_MODEL = 2048
BATCH = 4
SEQ = 2048
DEPTH = 2
GRID_W = 64
CTX_LEN = 256
N_BRANCH = 4
MIX_W = D_MODEL // N_BRANCH
NA_HEADS = 4
NA_HEAD_DIM = MIX_W // NA_HEADS
NA_WIN_ROWS = 8
NA_WIN_COLS = 16
GLA_HEADS = 4
GLA_VAL_DIM = MIX_W // GLA_HEADS
GLA_KEY_DIM = GLA_VAL_DIM // 2
GLA_RANK = 16
GLA_TAU = 16.0
GLA_CHUNK = 64
FNET_GROUPS = 4
FNET_GROUP_DIM = MIX_W // FNET_GROUPS
DIFF_HEADS = 4
DIFF_V_DIM = MIX_W // DIFF_HEADS
DIFF_QK_DIM = DIFF_V_DIM // 2
ROPE_BASE = 10000.0
N_GROUPS = 4
EXPERTS_PER_GROUP = 8
N_EXPERTS = N_GROUPS * EXPERTS_PER_GROUP
EXPERT_DIM = D_MODEL // 4
DEEPNORM_ALPHA = (2 * DEPTH) ** 0.25
LN_EPS = 1e-6
RMS_EPS = 1e-5
NEG_INF = -1e30

GLA_QK_W = GLA_HEADS * GLA_KEY_DIM
DIFF_QK_W = DIFF_HEADS * 2 * DIFF_QK_DIM
KV_SPLITS = (MIX_W, MIX_W, GLA_QK_W, MIX_W, 2 * GLA_RANK, DIFF_QK_W, MIX_W)
Q_SPLITS = (MIX_W, GLA_QK_W, MIX_W, MIX_W, DIFF_QK_W)
GATE_W = N_BRANCH * D_MODEL
ALL_SPLITS = KV_SPLITS + Q_SPLITS + (GATE_W,)

LANES = 128
SUBLANES = 8
VMEM_LIMIT = 56 * 1024 * 1024

T_LAT = BATCH * SEQ
T_CTX = BATCH * CTX_LEN
T_ALL = T_LAT + T_CTX
ROW_SLABS = D_MODEL // LANES

CB_GATES = 0
CB_NA_K = 64
CB_NA_V = 68
CB_GLA_K = 72
CB_GLA_V = 74
CB_DIFF_K = 78
CB_DIFF_V = 82
CB_NA_Q = 86
CB_GLA_Q = 90
CB_GLA_R = 92
CB_FNET_U = 96
CB_DIFF_Q = 100
P_COLS = 104 * LANES
_PIECE_ORDER = (12, 0, 1, 2, 3, 5, 6, 7, 8, 9, 10, 11)
_LR_PIECE = 4

PROJ_TM = 1024
PROJ_TN = 1024
ROW_TM = 512
EXPERT_TM = 256
NA_QROWS = 8
NA_KROWS = 16
DIFF_QB = 1024
DIFF_SUB = 256
CTX_BLK0 = T_LAT // CTX_LEN


def _cparams(sem, vmem=VMEM_LIMIT):
    return pltpu.CompilerParams(dimension_semantics=sem, vmem_limit_bytes=vmem)


def _dot(a, b):
    return jnp.dot(a, b, preferred_element_type=F32)


def _dot_nt(a, b):
    return lax.dot_general(a, b, (((1,), (1,)), ((), ())), preferred_element_type=F32)


def _dot_tn(a, b):
    return lax.dot_general(a, b, (((0,), (0,)), ((), ())), preferred_element_type=F32)


def _sigmoid(x):
    return 1.0 / (1.0 + jnp.exp(-x))


def _ln_rows(x):
    mu = jnp.mean(x, -1, keepdims=True)
    xc = x - mu
    var = jnp.mean(xc * xc, -1, keepdims=True)
    return xc * lax.rsqrt(var + LN_EPS)


def _rms_rows(x, gain):
    return x * lax.rsqrt(jnp.mean(x * x, -1, keepdims=True) + RMS_EPS) * gain


def _mod_row(tile_rows):
    per_batch = SEQ // tile_rows
    return lambda i: jnp.minimum(i // per_batch, BATCH)


MOD_TK = 256


def _mod_body(c_ref, w_ref, b_ref, o_ref):
    @pl.when(pl.program_id(1) == 0)
    def _():
        o_ref[...] = jnp.broadcast_to(b_ref[...], o_ref.shape)

    c = c_ref[...]
    s = (c * _sigmoid(c)).astype(BF16)
    o_ref[...] += _dot(s, w_ref[...].astype(BF16))


def _modulation(c8, w_mod, b_mod):
    depth, d, cols = w_mod.shape
    return pl.pallas_call(
        _mod_body,
        grid=(depth, d // MOD_TK),
        in_specs=[pl.BlockSpec((SUBLANES, MOD_TK), lambda l, k: (0, k)),
                  pl.BlockSpec((None, MOD_TK, cols), lambda l, k: (l, k, 0)),
                  pl.BlockSpec((None, 1, cols), lambda l, k: (l, 0, 0))],
        out_specs=pl.BlockSpec((None, SUBLANES, cols), lambda l, k: (l, 0, 0)),
        out_shape=jax.ShapeDtypeStruct((depth, SUBLANES, cols), F32),
        compiler_params=_cparams(("arbitrary", "arbitrary")),
    )(c8, w_mod, b_mod.reshape(depth, 1, cols))


PROJ_LAT_TILES = T_LAT // PROJ_TM


def _proj_body(*refs, split_input, skip_ctx_gates):
    if split_input:
        x_ref, xc_ref, sh_ref, sc_ref, w_ref, wlr_ref, p_ref, lr_ref, h_scr = refs
    else:
        x_ref, sh_ref, sc_ref, w_ref, wlr_ref, p_ref, lr_ref, h_scr = refs
    i = pl.program_id(0)
    j = pl.program_id(1)

    @pl.when(j == 0)
    def _():
        def chunk(c, carry):
            r = pl.ds(pl.multiple_of(c * ROW_TM, ROW_TM), ROW_TM)
            xv = x_ref[r, :]
            if split_input:
                xv = jnp.where(i < PROJ_LAT_TILES, xv, xc_ref[r, :])
            h = _ln_rows(xv) * (1.0 + sc_ref[...]) + sh_ref[...]
            hb = h.astype(BF16)
            h_scr[r, :] = hb
            lr_ref[r, :] = _dot(hb, wlr_ref[...])
            return carry
        lax.fori_loop(0, PROJ_TM // ROW_TM, chunk, 0)

    def project():
        p_ref[...] = _dot(h_scr[...], w_ref[...]).astype(BF16)

    if skip_ctx_gates:
        unused = (i >= PROJ_LAT_TILES) & (j < GATE_W // PROJ_TN)
        pl.when(jnp.logical_not(unused))(project)

        @pl.when(unused)
        def _():
            p_ref[...] = jnp.zeros_like(p_ref)
    else:
        project()


def _projection(x_lat, x_ctx, mod, layer, w_main, w_lr, *, skip_ctx_gates):
    split_input = x_ctx is not None
    brow = _mod_row(PROJ_TM)
    mrow = lambda i: layer * SUBLANES + brow(i)
    x_specs = [pl.BlockSpec((PROJ_TM, D_MODEL), lambda i, j: (jnp.minimum(i, PROJ_LAT_TILES - 1), 0))
               if split_input else pl.BlockSpec((PROJ_TM, D_MODEL), lambda i, j: (i, 0))]
    xs = [x_lat]
    if split_input:
        assert x_ctx.shape[0] == PROJ_TM
        x_specs.append(pl.BlockSpec((PROJ_TM, D_MODEL), lambda i, j: (0, 0), pipeline_mode=pl.Buffered(1)))
        xs.append(x_ctx)
    return pl.pallas_call(
        functools.partial(_proj_body, split_input=split_input, skip_ctx_gates=skip_ctx_gates),
        grid=(T_ALL // PROJ_TM, P_COLS // PROJ_TN),
        in_specs=x_specs + [pl.BlockSpec((None, 1, D_MODEL), lambda i, j: (mrow(i), 0, 0)),
                            pl.BlockSpec((None, 1, D_MODEL), lambda i, j: (mrow(i), 0, 1)),
                            pl.BlockSpec((D_MODEL, PROJ_TN), lambda i, j: (0, j)),
                            pl.BlockSpec((D_MODEL, LANES), lambda i, j: (0, 0))],
        out_specs=[pl.BlockSpec((PROJ_TM, PROJ_TN), lambda i, j: (i, j)),
                   pl.BlockSpec((PROJ_TM, LANES), lambda i, j: (i, 0))],
        out_shape=[jax.ShapeDtypeStruct((T_ALL, P_COLS), BF16),
                   jax.ShapeDtypeStruct((T_ALL, LANES), F32)],
        scratch_shapes=[pltpu.VMEM((PROJ_TM, D_MODEL), BF16)],
        compiler_params=_cparams(("arbitrary", "arbitrary")),
    )(*xs, mod, mod, w_main, w_lr)


def _na_key_row_start(rb):
    return jnp.clip(rb * NA_QROWS - NA_WIN_ROWS // 2, 0, SEQ // GRID_W - NA_KROWS)


def _softmax_pv(s_list, v_list):
    m = s_list[0].max(-1, keepdims=True)
    for s in s_list[1:]:
        m = jnp.maximum(m, s.max(-1, keepdims=True))
    den = None
    o = None
    for s, v in zip(s_list, v_list):
        p = jnp.exp(s - m)
        ps = p.sum(-1, keepdims=True)
        den = ps if den is None else den + ps
        pv = _dot(p.astype(BF16), v)
        o = pv if o is None else o + pv
    return o / den


def _na_lat_body(q_ref, k_ref, v_ref, kc_ref, vc_ref, tile_ref, o_ref, bias_ref):
    rb = pl.program_id(1)
    krows = NA_KROWS * GRID_W
    grid_rows = SEQ // GRID_W

    @pl.when(pl.program_id(2) == 0)
    def _():
        left = lax.broadcasted_iota(jnp.int32, (GRID_W, LANES), 1) < GRID_W
        k0 = _na_key_row_start(rb)
        for qr in range(NA_QROWS):
            r = rb * NA_QROWS + qr
            r0 = jnp.clip(r - NA_WIN_ROWS // 2, 0, grid_rows - NA_WIN_ROWS)
            for kp in range(NA_KROWS // 2):
                idx = []
                for kk in range(2):
                    kr = k0 + 2 * kp + kk
                    inside = (kr >= r0) & (kr < r0 + NA_WIN_ROWS)
                    idx.append(jnp.where(inside, kr - r + NA_WIN_ROWS - 1, 2 * NA_WIN_ROWS - 1))
                bias_ref[qr * GRID_W:(qr + 1) * GRID_W, kp * LANES:(kp + 1) * LANES] = jnp.where(
                    left, tile_ref[idx[0]], tile_ref[idx[1]])

    start = pl.multiple_of(_na_key_row_start(rb) * GRID_W, 4 * GRID_W)
    k = k_ref[pl.ds(start, krows), :]
    v = v_ref[pl.ds(start, krows), :]
    kc = kc_ref[...]
    vc = vc_ref[...]
    scale = NA_HEAD_DIM ** -0.5
    half = NA_QROWS * GRID_W // 2
    for part in range(2):
        rows = slice(part * half, (part + 1) * half)
        q = q_ref[rows, :]
        s_lat = _dot_nt(q, k) * scale + bias_ref[rows, :]
        s_ctx = _dot_nt(q, kc) * scale
        o_ref[rows, :] = _softmax_pv([s_lat, s_ctx], [v, vc]).astype(BF16)


def _na_ctx_body(q_ref, kc_ref, vc_ref, o_ref):
    s = _dot_nt(q_ref[...], kc_ref[...]) * (NA_HEAD_DIM ** -0.5)
    o_ref[...] = _softmax_pv([s], [vc_ref[...]]).astype(BF16)


def _na_bias_tiles(rpb):
    n_dr = 2 * NA_WIN_ROWS - 1
    sel_c = np.zeros((GRID_W, GRID_W, 2 * NA_WIN_COLS - 1), np.float32)
    ok_c = np.zeros((GRID_W, GRID_W), bool)
    for qc in range(GRID_W):
        w0 = int(np.clip(qc - NA_WIN_COLS // 2, 0, GRID_W - NA_WIN_COLS))
        for kc in range(GRID_W):
            if w0 <= kc < w0 + NA_WIN_COLS:
                ok_c[qc, kc] = True
                sel_c[qc, kc, int(np.clip(kc - qc + NA_WIN_COLS - 1, 0, 2 * NA_WIN_COLS - 2))] = 1.0
    t = jnp.einsum('xyc,hac->haxy', jnp.asarray(sel_c), rpb.astype(F32), precision=lax.Precision.HIGHEST)
    t = jnp.where(jnp.asarray(ok_c)[None, None], t, NEG_INF)
    t = jnp.concatenate([t, jnp.full((NA_HEADS, 1, GRID_W, GRID_W), NEG_INF, F32)], axis=1)
    assert t.shape[1] == n_dr + 1
    return jnp.concatenate([t, t], axis=-1)


def _na_lat(p, tiles):
    qrows = NA_QROWS * GRID_W
    n_rb = SEQ // qrows
    return pl.pallas_call(
        _na_lat_body,
        grid=(NA_HEADS, n_rb, BATCH),
        in_specs=[pl.BlockSpec((qrows, LANES), lambda h, rb, b: (b * n_rb + rb, CB_NA_Q + h)),
                  pl.BlockSpec((SEQ, LANES), lambda h, rb, b: (b, CB_NA_K + h)),
                  pl.BlockSpec((SEQ, LANES), lambda h, rb, b: (b, CB_NA_V + h)),
                  pl.BlockSpec((CTX_LEN, LANES), lambda h, rb, b: (CTX_BLK0 + b, CB_NA_K + h)),
                  pl.BlockSpec((CTX_LEN, LANES), lambda h, rb, b: (CTX_BLK0 + b, CB_NA_V + h)),
                  pl.BlockSpec((None, 2 * NA_WIN_ROWS, GRID_W, LANES), lambda h, rb, b: (h, 0, 0, 0))],
        out_specs=pl.BlockSpec((qrows, LANES), lambda h, rb, b: (b * n_rb + rb, h)),
        out_shape=jax.ShapeDtypeStruct((T_LAT, MIX_W), BF16),
        scratch_shapes=[pltpu.VMEM((qrows, NA_KROWS * GRID_W), F32)],
        compiler_params=_cparams(("arbitrary",) * 3),
    )(p, p, p, p, p, tiles)


def _na_ctx(p):
    return pl.pallas_call(
        _na_ctx_body,
        grid=(BATCH, NA_HEADS),
        in_specs=[pl.BlockSpec((CTX_LEN, LANES), lambda b, h: (CTX_BLK0 + b, CB_NA_Q + h)),
                  pl.BlockSpec((CTX_LEN, LANES), lambda b, h: (CTX_BLK0 + b, CB_NA_K + h)),
                  pl.BlockSpec((CTX_LEN, LANES), lambda b, h: (CTX_BLK0 + b, CB_NA_V + h))],
        out_specs=pl.BlockSpec((CTX_LEN, LANES), lambda b, h: (b, h)),
        out_shape=jax.ShapeDtypeStruct((T_CTX, MIX_W), BF16),
        compiler_params=_cparams(("arbitrary",) * 2),
    )(p, p, p)


def _log_sigmoid(z):
    return jnp.minimum(z, 0.0) - jnp.log1p(jnp.exp(-jnp.abs(z)))


GLA_GROUP = 256


def _gla_body(q_ref, k_ref, v_ref, r_ref, lr_ref, wgf_ref, wgb_ref, bgf_ref, bgb_ref, gain_ref, s0_ref,
              y_ref, sfin_ref, la_scr, o_scr, qh_scr, ut_scr, sp_scr, g_scr, st_scr, *, n):
    c = GLA_CHUNK
    nc = n // c
    grp = GLA_GROUP
    cpg = grp // c
    lr = lr_ref[...].astype(BF16)
    la_scr[0] = _log_sigmoid(_dot(lr, wgf_ref[...]) + bgf_ref[...]) / GLA_TAU
    la_scr[1] = _log_sigmoid(_dot(lr, wgb_ref[...]) + bgb_ref[...]) / GLA_TAU
    o_scr[...] = jnp.zeros_like(o_scr)

    lane = lax.broadcasted_iota(jnp.int32, (grp, LANES), 1)
    head_mask = (lane < GLA_KEY_DIM, lane >= GLA_KEY_DIM)
    ri = lax.broadcasted_iota(jnp.int32, (grp, grp), 0)
    ci = lax.broadcasted_iota(jnp.int32, (grp, grp), 1)
    same_chunk = lax.shift_right_logical(ri, 6) == lax.shift_right_logical(ci, 6)
    assert c == 1 << 6
    tri_mask = (same_chunk & (ri >= ci), same_chunk & (ci >= ri))
    tri_mat = tuple(jnp.where(m, 1.0, 0.0).astype(BF16) for m in tri_mask)
    chunk_ones = jnp.where(same_chunk, 1.0, 0.0).astype(BF16)
    scale = GLA_KEY_DIM ** -0.5

    def group(gi, carry):
        r = pl.ds(pl.multiple_of(gi * grp, grp), grp)
        q = q_ref[r, :].astype(F32)
        k = k_ref[r, :].astype(F32)
        for d in range(2):
            la = la_scr[d, r, :]
            la_hi = la.astype(BF16)
            la_lo = (la - la_hi.astype(F32)).astype(BF16)
            cum = _dot(tri_mat[d], la_hi) + _dot(tri_mat[d], la_lo)
            tot = _dot(chunk_ones, la_hi) + _dot(chunk_ones, la_lo)
            qf = q * jnp.exp(cum) * scale
            k_in = (k * jnp.exp(-cum)).astype(BF16)
            k_end = k * jnp.exp(tot - cum)
            g = jnp.exp(tot)
            for ch in range(cpg):
                g_scr[d, pl.ds(gi * cpg + ch, 1), :] = g[ch * c:ch * c + 1, :]
            for hh in range(2):
                chain = hh * 2 + d
                cols = slice(hh * GLA_VAL_DIM, (hh + 1) * GLA_VAL_DIM)
                qh = jnp.where(head_mask[hh], qf, 0.0).astype(BF16)
                qh_scr[chain, r, :] = qh
                a = jnp.where(tri_mask[d], _dot_nt(qh, k_in), 0.0)
                vh = v_ref[r, cols]
                o_scr[r, cols] = o_scr[r, cols] + _dot(a.astype(BF16), vh)
                kh = jnp.where(head_mask[hh], k_end, 0.0).astype(BF16)
                for ch in range(cpg):
                    rows = slice(ch * c, (ch + 1) * c)
                    ut_scr[chain, gi * cpg + ch] = _dot_tn(vh[rows], kh[rows])
        return carry

    lax.fori_loop(0, n // grp, group, 0)

    st_scr[...] = s0_ref[...]

    def scan(i, carry):
        for d in range(2):
            cc = i if d == 0 else nc - 1 - i
            g = g_scr[d, pl.ds(cc, 1), :]
            for hh in range(2):
                chain = hh * 2 + d
                st = st_scr[chain]
                sp_scr[chain, cc] = st.astype(BF16)
                st_scr[chain] = g * st + ut_scr[chain, cc]
        return carry

    lax.fori_loop(0, nc, scan, 0)
    sfin_ref[...] = st_scr[...]

    def inter(cc, carry):
        r = pl.ds(pl.multiple_of(cc * c, c), c)
        for hh in range(2):
            cols = slice(hh * GLA_VAL_DIM, (hh + 1) * GLA_VAL_DIM)
            o = (_dot_nt(qh_scr[hh * 2, r, :], sp_scr[hh * 2, cc])
                 + _dot_nt(qh_scr[hh * 2 + 1, r, :], sp_scr[hh * 2 + 1, cc]))
            o_scr[r, cols] = o_scr[r, cols] + o
        return carry

    lax.fori_loop(0, nc, inter, 0, unroll=4)
    for hh in range(2):
        cols = slice(hh * GLA_VAL_DIM, (hh + 1) * GLA_VAL_DIM)
        rr = r_ref[:, cols].astype(F32)
        y_ref[:, cols] = (_rms_rows(o_scr[:, cols], gain_ref[...]) * (rr * _sigmoid(rr))).astype(BF16)


def _gla(p, lr, wg_f, wg_b, bg_f, bg_b, gain, s0, *, ctx):
    n = CTX_LEN if ctx else SEQ
    blk0 = CTX_BLK0 if ctx else 0
    pairs = GLA_HEADS // 2
    st_spec = pl.BlockSpec((None, None, 4, GLA_VAL_DIM, LANES), lambda b, pr: (b, pr, 0, 0, 0))
    return pl.pallas_call(
        functools.partial(_gla_body, n=n),
        grid=(BATCH, pairs),
        in_specs=[pl.BlockSpec((n, LANES), lambda b, pr: (blk0 + b, CB_GLA_Q + pr)),
                  pl.BlockSpec((n, LANES), lambda b, pr: (blk0 + b, CB_GLA_K + pr)),
                  pl.BlockSpec((n, 2 * LANES), lambda b, pr: (blk0 + b, CB_GLA_V // 2 + pr)),
                  pl.BlockSpec((n, 2 * LANES), lambda b, pr: (blk0 + b, CB_GLA_R // 2 + pr)),
                  pl.BlockSpec((n, LANES), lambda b, pr: (blk0 + b, 0)),
                  pl.BlockSpec((LANES, LANES), lambda b, pr: (0, pr)),
                  pl.BlockSpec((LANES, LANES), lambda b, pr: (0, pr)),
                  pl.BlockSpec((1, LANES), lambda b, pr: (0, pr)),
                  pl.BlockSpec((1, LANES), lambda b, pr: (0, pr)),
                  pl.BlockSpec((1, GLA_VAL_DIM), lambda b, pr: (0, 0)),
                  st_spec],
        out_specs=[pl.BlockSpec((n, 2 * LANES), lambda b, pr: (b, pr)), st_spec],
        out_shape=[jax.ShapeDtypeStruct((BATCH * n, MIX_W), BF16),
                   jax.ShapeDtypeStruct((BATCH, pairs, 4, GLA_VAL_DIM, LANES), F32)],
        scratch_shapes=[pltpu.VMEM((2, n, LANES), F32),
                        pltpu.VMEM((n, 2 * LANES), F32),
                        pltpu.VMEM((4, n, LANES), BF16),
                        pltpu.VMEM((4, n // GLA_CHUNK, GLA_VAL_DIM, LANES), F32),
                        pltpu.VMEM((4, n // GLA_CHUNK, GLA_VAL_DIM, LANES), BF16),
                        pltpu.VMEM((2, max(n // GLA_CHUNK, SUBLANES), LANES), F32),
                        pltpu.VMEM((4, GLA_VAL_DIM, LANES), F32)],
        compiler_params=_cparams(("arbitrary",) * 2),
    )(p, p, p, p, lr, wg_f, wg_b, bg_f, bg_b, gain, s0)


def _gla_gate_weights(w_gate, b_gate):
    wf = jnp.zeros((LANES, GLA_QK_W), F32).at[:GLA_RANK].set(w_gate[0])
    wb = jnp.zeros((LANES, GLA_QK_W), F32).at[GLA_RANK:2 * GLA_RANK].set(w_gate[1])
    return wf.astype(BF16), wb.astype(BF16), b_gate[0:1], b_gate[1:2]


def _dft_tables(n):
    def cs(m):
        idx = (np.arange(m)[:, None] * np.arange(m)[None, :]) % m
        ang = 2.0 * np.pi * idx / m
        return np.cos(ang) / np.sqrt(m), np.sin(ang) / np.sqrt(m)
    cn, sn = cs(n)
    cg, sg = cs(FNET_GROUP_DIM)
    as_bf16 = lambda a: jnp.asarray(a, F32).astype(BF16)
    return as_bf16(cn), as_bf16(sn), as_bf16(np.concatenate([cg, sg], axis=1))


def _fnet_body(u_ref, cn_ref, sn_ref, csg_ref, o_ref):
    gd = FNET_GROUP_DIM
    ts = [_dot(u_ref[:, g * gd:(g + 1) * gd], csg_ref[...]).astype(BF16) for g in range(2)]
    tc = jnp.concatenate([t[:, :gd] for t in ts], axis=1)
    tsn = jnp.concatenate([t[:, gd:] for t in ts], axis=1)
    o_ref[...] = (_dot(cn_ref[...], tc) - _dot(sn_ref[...], tsn)).astype(BF16)


def _fnet(p, *, ctx):
    n = CTX_LEN if ctx else SEQ
    blk0 = CTX_BLK0 if ctx else 0
    cn, sn, csg = _dft_tables(n)
    return pl.pallas_call(
        _fnet_body,
        grid=(BATCH, FNET_GROUPS // 2),
        in_specs=[pl.BlockSpec((n, 2 * LANES), lambda b, g: (blk0 + b, CB_FNET_U // 2 + g)),
                  pl.BlockSpec((n, n), lambda b, g: (0, 0)),
                  pl.BlockSpec((n, n), lambda b, g: (0, 0)),
                  pl.BlockSpec((FNET_GROUP_DIM, 2 * FNET_GROUP_DIM), lambda b, g: (0, 0))],
        out_specs=pl.BlockSpec((n, 2 * LANES), lambda b, g: (b, g)),
        out_shape=jax.ShapeDtypeStruct((BATCH * n, MIX_W), BF16),
        compiler_params=_cparams(("arbitrary",) * 2),
    )(p, cn, sn, csg)


def _rope_tables():
    half = DIFF_QK_DIM // 2
    inv = ROPE_BASE ** (-jnp.arange(0, half, 2, dtype=F32) / half)
    t = jnp.arange(SEQ)
    ang_r = (t // GRID_W).astype(F32)[:, None] * inv
    ang_c = (t % GRID_W).astype(F32)[:, None] * inv
    cos = jnp.concatenate([jnp.cos(ang_r)] * 2 + [jnp.cos(ang_c)] * 2, axis=1)
    sin = jnp.concatenate([jnp.sin(ang_r)] * 2 + [jnp.sin(ang_c)] * 2, axis=1)
    rot = np.zeros((LANES, LANES), np.float32)
    q4 = half // 2
    for base in range(0, LANES, half):
        for j in range(q4):
            rot[base + j + q4, base + j] = -1.0
            rot[base + j, base + j + q4] = 1.0
    return jnp.tile(cos, (1, 2)), jnp.tile(sin, (1, 2)), jnp.asarray(rot, BF16)


def _rope(x, cos, sin, rot):
    return x.astype(F32) * cos + _dot(x, rot) * sin


def _diff_out(q, k_list, v_list, lam, gain, out_scale):
    q = q * (DIFF_QK_DIM ** -0.5)
    lane = lax.broadcasted_iota(jnp.int32, q.shape, 1)
    o = None
    for sub, coef in ((lane < DIFF_QK_DIM, None), (lane >= DIFF_QK_DIM, lam)):
        qs = jnp.where(sub, q, 0.0).astype(BF16)
        os = _softmax_pv([_dot_nt(qs, k) for k in k_list], v_list)
        o = os if coef is None else o - coef * os
    return (_rms_rows(o, gain) * out_scale).astype(BF16)


def _diff_lat_body(q_ref, k_ref, v_ref, kc_ref, vc_ref, cosq_ref, sinq_ref, cosk_ref, sink_ref, rot_ref,
                   lam_ref, gain_ref, o_ref, k_scr, *, out_scale):
    @pl.when(pl.program_id(2) == 0)
    def _():
        k_scr[...] = _rope(k_ref[...], cosk_ref[...], sink_ref[...], rot_ref[...]).astype(BF16)

    for part in range(DIFF_QB // DIFF_SUB):
        rows = slice(part * DIFF_SUB, (part + 1) * DIFF_SUB)
        q = _rope(q_ref[rows, :], cosq_ref[rows, :], sinq_ref[rows, :], rot_ref[...])
        o_ref[rows, :] = _diff_out(q, [kc_ref[...], k_scr[...]], [vc_ref[...], v_ref[...]], lam_ref[:, 0:1],
                                   gain_ref[...], out_scale)


def _diff_ctx_body(q_ref, kc_ref, vc_ref, lam_ref, gain_ref, o_ref, *, out_scale):
    o_ref[...] = _diff_out(q_ref[...].astype(F32), [kc_ref[...]], [vc_ref[...]], lam_ref[:, 0:1],
                           gain_ref[...], out_scale)


def _diff_lat(p, rope, lam_row, gain, lam_init):
    cos, sin, rot = rope
    nqb = SEQ // DIFF_QB
    const = lambda b, h, qb: (0, 0)
    return pl.pallas_call(
        functools.partial(_diff_lat_body, out_scale=1.0 - lam_init),
        grid=(BATCH, DIFF_HEADS, nqb),
        in_specs=[pl.BlockSpec((DIFF_QB, LANES), lambda b, h, qb: (b * nqb + qb, CB_DIFF_Q + h)),
                  pl.BlockSpec((SEQ, LANES), lambda b, h, qb: (b, CB_DIFF_K + h)),
                  pl.BlockSpec((SEQ, LANES), lambda b, h, qb: (b, CB_DIFF_V + h)),
                  pl.BlockSpec((CTX_LEN, LANES), lambda b, h, qb: (CTX_BLK0 + b, CB_DIFF_K + h)),
                  pl.BlockSpec((CTX_LEN, LANES), lambda b, h, qb: (CTX_BLK0 + b, CB_DIFF_V + h)),
                  pl.BlockSpec((DIFF_QB, LANES), lambda b, h, qb: (qb, 0)),
                  pl.BlockSpec((DIFF_QB, LANES), lambda b, h, qb: (qb, 0)),
                  pl.BlockSpec((SEQ, LANES), const),
                  pl.BlockSpec((SEQ, LANES), const),
                  pl.BlockSpec((LANES, LANES), const),
                  pl.BlockSpec((1, LANES), const),
                  pl.BlockSpec((1, DIFF_V_DIM), const)],
        out_specs=pl.BlockSpec((DIFF_QB, LANES), lambda b, h, qb: (b * nqb + qb, h)),
        out_shape=jax.ShapeDtypeStruct((T_LAT, MIX_W), BF16),
        scratch_shapes=[pltpu.VMEM((SEQ, LANES), BF16)],
        compiler_params=_cparams(("arbitrary",) * 3),
    )(p, p, p, p, p, cos, sin, cos, sin, rot, lam_row, gain)


def _diff_ctx(p, lam_row, gain, lam_init):
    const = lambda b, h: (0, 0)
    return pl.pallas_call(
        functools.partial(_diff_ctx_body, out_scale=1.0 - lam_init),
        grid=(BATCH, DIFF_HEADS),
        in_specs=[pl.BlockSpec((CTX_LEN, LANES), lambda b, h: (CTX_BLK0 + b, CB_DIFF_Q + h)),
                  pl.BlockSpec((CTX_LEN, LANES), lambda b, h: (CTX_BLK0 + b, CB_DIFF_K + h)),
                  pl.BlockSpec((CTX_LEN, LANES), lambda b, h: (CTX_BLK0 + b, CB_DIFF_V + h)),
                  pl.BlockSpec((1, LANES), const),
                  pl.BlockSpec((1, DIFF_V_DIM), const)],
        out_specs=pl.BlockSpec((CTX_LEN, LANES), lambda b, h: (b, h)),
        out_shape=jax.ShapeDtypeStruct((T_CTX, MIX_W), BF16),
        compiler_params=_cparams(("arbitrary",) * 2),
    )(p, p, p, lam_row, gain)


MERGE_TM = 512
N_LAT_TILES = T_LAT // MERGE_TM


def _merge_body(*refs, with_ctx):
    gates = refs[0:N_BRANCH]
    y_lat = refs[N_BRANCH:2 * N_BRANCH]
    y_ctx = refs[2 * N_BRANCH:3 * N_BRANCH] if with_ctx else None
    wb_ref, m_ref = refs[-2], refs[-1]
    is_lat = pl.program_id(0) < N_LAT_TILES
    acc = None
    for i in range(N_BRANCH):
        y = y_lat[i][...]
        if with_ctx:
            y = jnp.where(is_lat, y, y_ctx[i][...])
        t = _sigmoid(gates[i][...].astype(F32)) * _dot(y, wb_ref[i])
        acc = t if acc is None else acc + t
    m_ref[...] = acc.astype(BF16)


def _merge(p, ys_lat, ys_ctx, w_branch):
    with_ctx = ys_ctx is not None
    t_rows = T_ALL if with_ctx else T_LAT
    gate_specs = [pl.BlockSpec((MERGE_TM, D_MODEL), functools.partial(lambda i, g: (i, g), g=g))
                  for g in range(N_BRANCH)]
    y_specs = [pl.BlockSpec((MERGE_TM, MIX_W), lambda i: (jnp.minimum(i, N_LAT_TILES - 1), 0))
               for _ in range(N_BRANCH)]
    ys = list(ys_lat)
    if with_ctx:
        y_specs += [pl.BlockSpec((MERGE_TM, MIX_W), lambda i: (jnp.maximum(i - N_LAT_TILES, 0), 0))
                    for _ in range(N_BRANCH)]
        ys += list(ys_ctx)
    return pl.pallas_call(
        functools.partial(_merge_body, with_ctx=with_ctx),
        grid=(t_rows // MERGE_TM,),
        in_specs=gate_specs + y_specs + [pl.BlockSpec((N_BRANCH, MIX_W, D_MODEL), lambda i: (0, 0, 0),
                                                      pipeline_mode=pl.Buffered(1))],
        out_specs=pl.BlockSpec((MERGE_TM, D_MODEL), lambda i: (i, 0)),
        out_shape=jax.ShapeDtypeStruct((t_rows, D_MODEL), BF16),
        compiler_params=_cparams(("arbitrary",)),
    )(p, p, p, p, *ys, w_branch)


def _residual_ln(x, gate, f, g, b):
    return _ln_rows(DEEPNORM_ALPHA * x + gate * f) * g + b


def _outproj_body(*refs, split_input):
    if split_input:
        m_ref, x_ref, xc_ref, gate_ref, w_ref, g_ref, b_ref, o_ref = refs
        x = jnp.where(pl.program_id(0) < N_LAT_TILES, x_ref[...], xc_ref[...])
    else:
        m_ref, x_ref, gate_ref, w_ref, g_ref, b_ref, o_ref = refs
        x = x_ref[...]
    f = _dot(m_ref[...], w_ref[...])
    o_ref[...] = _residual_ln(x, gate_ref[...], f, g_ref[...], b_ref[...])


def _outproj(m, x_lat, x_ctx, mod, layer, w_out, ln_g, ln_b, t_rows):
    split_input = x_ctx is not None
    brow = _mod_row(MERGE_TM)
    vec = pl.BlockSpec((1, D_MODEL), lambda i: (0, 0))
    rows = pl.BlockSpec((MERGE_TM, D_MODEL), lambda i: (i, 0))
    x_specs, xs = [rows], [x_lat]
    if split_input:
        x_specs = [pl.BlockSpec((MERGE_TM, D_MODEL), lambda i: (jnp.minimum(i, N_LAT_TILES - 1), 0)),
                   pl.BlockSpec((MERGE_TM, D_MODEL), lambda i: (jnp.maximum(i - N_LAT_TILES, 0), 0))]
        xs = [x_lat, x_ctx]
    return pl.pallas_call(
        functools.partial(_outproj_body, split_input=split_input),
        grid=(t_rows // MERGE_TM,),
        in_specs=[rows] + x_specs
        + [pl.BlockSpec((None, 1, D_MODEL), lambda i: (layer * SUBLANES + brow(i), 0, 2)),
           pl.BlockSpec((D_MODEL, D_MODEL), lambda i: (0, 0), pipeline_mode=pl.Buffered(1)), vec, vec],
        out_specs=rows,
        out_shape=jax.ShapeDtypeStruct((t_rows, D_MODEL), F32),
        compiler_params=_cparams(("arbitrary",)),
    )(m, *xs, mod, w_out, ln_g, ln_b)


RT_E1, RT_E2, RT_W1, RT_W2, RT_RANK1, RT_RANK2 = range(6)


def _first_lane_of_max(vals, lane_f):
    m = vals.max(-1, keepdims=True)
    idx = jnp.where(vals == m, lane_f, float(LANES)).min(-1, keepdims=True)
    return m, idx


def _router_body(x_ref, sh_ref, sc_ref, wr_ref, br_ref, h_ref, route_ref, cnt_ref, run_scr):
    @pl.when(pl.program_id(0) == 0)
    def _():
        run_scr[...] = jnp.zeros_like(run_scr)

    tm = ROW_TM
    h = _ln_rows(x_ref[...]) * (1.0 + sc_ref[...]) + sh_ref[...]
    hb = h.astype(BF16)
    for s in range(ROW_SLABS):
        h_ref[pl.ds(s, tm, stride=ROW_SLABS), :] = hb[:, s * LANES:(s + 1) * LANES].astype(F32)
    logits = _dot(hb, wr_ref[...]) + br_ref[...]
    lane = lax.broadcasted_iota(jnp.int32, (tm, LANES), 1)
    lane_f = lane.astype(F32)
    is_group = lane < N_GROUPS
    g_logits = jnp.where(is_group, logits, NEG_INF)
    g_max, g_sel = _first_lane_of_max(g_logits, lane_f)
    g_w = 1.0 / jnp.where(is_group, jnp.exp(logits - g_max), 0.0).sum(-1, keepdims=True)
    lo = N_GROUPS + EXPERTS_PER_GROUP * g_sel
    e_logits = jnp.where((lane_f >= lo) & (lane_f < lo + EXPERTS_PER_GROUP), logits, NEG_INF)
    v1, i1 = _first_lane_of_max(e_logits, lane_f)
    v2, i2 = _first_lane_of_max(jnp.where(lane_f == i1, NEG_INF, e_logits), lane_f)
    t = jnp.exp(v2 - v1)
    w1 = g_w / (1.0 + t)
    w2 = g_w * t / (1.0 + t)
    e1 = i1 - N_GROUPS
    e2 = i2 - N_GROUPS

    oh1 = lane_f == e1
    oh2 = lane_f == e2
    both = jnp.where(oh1 | oh2, 1.0, 0.0)
    ri = lax.broadcasted_iota(jnp.int32, (tm, tm), 0)
    ci = lax.broadcasted_iota(jnp.int32, (tm, tm), 1)
    before = jnp.where(ci < ri, 1.0, 0.0).astype(BF16)
    excl = _dot(before, both.astype(BF16)) + run_scr[0:1, :]
    rank1 = jnp.where(oh1, excl, 0.0).sum(-1, keepdims=True)
    rank2 = jnp.where(oh2, excl, 0.0).sum(-1, keepdims=True)
    run_scr[...] = run_scr[...] + both.sum(0, keepdims=True)
    cnt_ref[...] = run_scr[...]

    rec = jnp.zeros((tm, LANES), F32)
    for idx, val in ((RT_E1, e1), (RT_E2, e2), (RT_W1, w1), (RT_W2, w2), (RT_RANK1, rank1), (RT_RANK2, rank2)):
        rec = jnp.where(lane == idx, val, rec)
    route_ref[...] = rec


def _router(x1, mod, layer, w_r, b_r, t_rows):
    brow = _mod_row(ROW_TM)
    mrow = lambda i: layer * SUBLANES + brow(i)
    return pl.pallas_call(
        _router_body,
        grid=(t_rows // ROW_TM,),
        in_specs=[pl.BlockSpec((ROW_TM, D_MODEL), lambda i: (i, 0)),
                  pl.BlockSpec((None, 1, D_MODEL), lambda i: (mrow(i), 0, 3)),
                  pl.BlockSpec((None, 1, D_MODEL), lambda i: (mrow(i), 0, 4)),
                  pl.BlockSpec((D_MODEL, LANES), lambda i: (0, 0)),
                  pl.BlockSpec((1, LANES), lambda i: (0, 0))],
        out_specs=[pl.BlockSpec((ROW_TM * ROW_SLABS, LANES), lambda i: (i, 0)),
                   pl.BlockSpec((ROW_TM, LANES), lambda i: (i, 0)),
                   pl.BlockSpec((SUBLANES, LANES), lambda i: (0, 0))],
        out_shape=[jax.ShapeDtypeStruct((t_rows * ROW_SLABS, LANES), F32),
                   jax.ShapeDtypeStruct((t_rows, LANES), F32),
                   jax.ShapeDtypeStruct((SUBLANES, LANES), F32)],
        scratch_shapes=[pltpu.VMEM((SUBLANES, LANES), F32)],
        compiler_params=_cparams(("arbitrary",)),
    )(x1, mod, mod, w_r, b_r)


def _router_weights(w_group, b_group, w_router, b_router):
    wr = jnp.concatenate([w_group, jnp.transpose(w_router, (1, 0, 2)).reshape(D_MODEL, N_EXPERTS)], axis=1)
    br = jnp.concatenate([b_group, b_router.reshape(N_EXPERTS)])
    pad = LANES - wr.shape[1]
    return jnp.pad(wr, ((0, 0), (0, pad))).astype(BF16), jnp.pad(br, (0, pad)).reshape(1, LANES)


def _token_rows(ref, tok):
    return ref.at[pl.ds(pl.multiple_of(tok * ROW_SLABS, ROW_SLABS), ROW_SLABS), :]


DMA_UNROLL = 8


def _dispatch_body(pos_ref, h_ref, xs_in_ref, xs_ref, sem, *, t_rows):
    del xs_in_ref
    tt = ROW_TM
    base = pl.program_id(0) * tt

    def issue(j, carry):
        for k in range(2):
            pltpu.make_async_copy(_token_rows(h_ref, j), _token_rows(xs_ref, pos_ref[k * t_rows + base + j]),
                                  sem).start(priority=k)
        return carry

    def drain(j, carry):
        pltpu.make_async_copy(_token_rows(h_ref, 0), _token_rows(xs_ref, 0), sem).wait()
        return carry

    lax.fori_loop(0, tt, issue, 0, unroll=DMA_UNROLL)
    lax.fori_loop(0, 2 * tt, drain, 0, unroll=DMA_UNROLL)


def _dispatch(pos, h, xs0):
    t_rows = h.shape[0] // ROW_SLABS
    return pl.pallas_call(
        functools.partial(_dispatch_body, t_rows=t_rows),
        grid_spec=pltpu.PrefetchScalarGridSpec(
            num_scalar_prefetch=1,
            grid=(t_rows // ROW_TM,),
            in_specs=[pl.BlockSpec((ROW_TM * ROW_SLABS, LANES), lambda i, pos: (i, 0)),
                      pl.BlockSpec(memory_space=pl.ANY)],
            out_specs=pl.BlockSpec(memory_space=pl.ANY),
            scratch_shapes=[pltpu.SemaphoreType.DMA(())]),
        out_shape=jax.ShapeDtypeStruct(xs0.shape, F32),
        input_output_aliases={2: 0},
        compiler_params=_cparams(("arbitrary",)),
    )(pos, h, xs0)


def _expert_body(te_ref, nu_ref, nxt_ref, slot_ref, xs_ref, wgu_hbm, wd_hbm, ys_ref,
                 wgu_f, wd_f, wgu_b, wd_b, sem, *, layer):
    i = pl.program_id(0)
    tm = EXPERT_TM
    used = i < nu_ref[0]

    def fetch(expert, sl):
        e = layer * N_EXPERTS + expert
        return (pltpu.make_async_copy(wgu_hbm.at[e], wgu_f.at[sl], sem.at[0, sl]),
                pltpu.make_async_copy(wd_hbm.at[e], wd_f.at[sl], sem.at[1, sl]))

    @pl.when(used)
    def _():
        @pl.when((i == 0) | (te_ref[i] != te_ref[jnp.maximum(i - 1, 0)]))
        def _():
            sl = slot_ref[i]

            @pl.when(i == 0)
            def _():
                for cp in fetch(te_ref[0], 0):
                    cp.start()

            @pl.when(nxt_ref[i] >= 0)
            def _():
                for cp in fetch(nxt_ref[i], 1 - sl):
                    cp.start()

            for cp in fetch(te_ref[i], sl):
                cp.wait()

            def cast(c, carry):
                r = pl.ds(pl.multiple_of(c * ROW_TM, ROW_TM), ROW_TM)
                wgu_b[r, :] = wgu_f[sl, r, :].astype(BF16)
                return carry
            lax.fori_loop(0, D_MODEL // ROW_TM, cast, 0)
            wd_b[...] = wd_f[sl].astype(BF16)

        x = jnp.concatenate([xs_ref[pl.ds(s, tm, stride=ROW_SLABS), :].astype(BF16) for s in range(ROW_SLABS)],
                            axis=1)
        hgu = _dot(x, wgu_b[...])
        hg = hgu[:, :EXPERT_DIM]
        hu = hgu[:, EXPERT_DIM:]
        act = (hg * _sigmoid(hg) * hu).astype(BF16)
        y = _dot(act, wd_b[...])
        for s in range(ROW_SLABS):
            ys_ref[pl.ds(s, tm, stride=ROW_SLABS), :] = y[:, s * LANES:(s + 1) * LANES]

    @pl.when(jnp.logical_not(used))
    def _():
        ys_ref[...] = jnp.zeros_like(ys_ref)


def _experts(tile_expert, n_used, next_expert, tile_slot, xs, w_gu, w_down, layer):
    n_tiles = xs.shape[0] // (EXPERT_TM * ROW_SLABS)
    blk = EXPERT_TM * ROW_SLABS
    return pl.pallas_call(
        functools.partial(_expert_body, layer=layer),
        grid_spec=pltpu.PrefetchScalarGridSpec(
            num_scalar_prefetch=4,
            grid=(n_tiles,),
            in_specs=[pl.BlockSpec((blk, LANES), lambda i, te, nu, nx, sl: (jnp.minimum(i, nu[0] - 1), 0)),
                      pl.BlockSpec(memory_space=pl.ANY),
                      pl.BlockSpec(memory_space=pl.ANY)],
            out_specs=pl.BlockSpec((blk, LANES), lambda i, te, nu, nx, sl: (i, 0)),
            scratch_shapes=[pltpu.VMEM((2, D_MODEL, 2 * EXPERT_DIM), F32),
                            pltpu.VMEM((2, EXPERT_DIM, D_MODEL), F32),
                            pltpu.VMEM((D_MODEL, 2 * EXPERT_DIM), BF16),
                            pltpu.VMEM((EXPERT_DIM, D_MODEL), BF16),
                            pltpu.SemaphoreType.DMA((2, 2))]),
        out_shape=jax.ShapeDtypeStruct(xs.shape, F32),
        compiler_params=_cparams(("arbitrary",)),
    )(tile_expert, n_used, next_expert, tile_slot, xs, w_gu, w_down)


GATHER_PITCH = ROW_SLABS + SUBLANES


def _final_body(pos_ref, x_ref, ys_ref, route_ref, gate_ref, g_ref, b_ref, out_ref, buf, sem, *, t_rows):
    tm = ROW_TM
    i = pl.program_id(0)
    slot = lax.rem(i, 2)

    def gather_copy(tile, sl, t, k):
        dst = buf.at[sl, pl.ds(pl.multiple_of((k * tm + t) * GATHER_PITCH, SUBLANES), ROW_SLABS), :]
        return pltpu.make_async_copy(_token_rows(ys_ref, pos_ref[k * t_rows + tile * tm + t]), dst, sem.at[sl])

    def start_tile(tile, sl):
        def issue(t, carry):
            for k in range(2):
                gather_copy(tile, sl, t, k).start(priority=k)
            return carry
        lax.fori_loop(0, tm, issue, 0, unroll=DMA_UNROLL // 2)

    @pl.when(i == 0)
    def _():
        start_tile(0, 0)

    @pl.when(i + 1 < pl.num_programs(0))
    def _():
        start_tile(i + 1, 1 - slot)

    def drain(t, carry):
        for k in range(2):
            gather_copy(i, slot, 0, k).wait()
        return carry
    lax.fori_loop(0, tm, drain, 0, unroll=DMA_UNROLL // 2)

    rt = route_ref[...]
    w1 = rt[:, RT_W1:RT_W1 + 1]
    w2 = rt[:, RT_W2:RT_W2 + 1]
    f = jnp.concatenate(
        [w1 * buf[slot, pl.ds(s, tm, stride=GATHER_PITCH), :]
         + w2 * buf[slot, pl.ds(tm * GATHER_PITCH + s, tm, stride=GATHER_PITCH), :] for s in range(ROW_SLABS)],
        axis=1)
    out_ref[...] = _residual_ln(x_ref[...], gate_ref[...], f, g_ref[...], b_ref[...])


def _final(pos, x1, ys, route, mod, layer, ln_g, ln_b, t_rows):
    brow = _mod_row(ROW_TM)
    vec = pl.BlockSpec((1, D_MODEL), lambda i, pos: (0, 0))
    return pl.pallas_call(
        functools.partial(_final_body, t_rows=t_rows),
        grid_spec=pltpu.PrefetchScalarGridSpec(
            num_scalar_prefetch=1,
            grid=(t_rows // ROW_TM,),
            in_specs=[pl.BlockSpec((ROW_TM, D_MODEL), lambda i, pos: (i, 0)),
                      pl.BlockSpec(memory_space=pl.ANY),
                      pl.BlockSpec((ROW_TM, LANES), lambda i, pos: (i, 0)),
                      pl.BlockSpec((None, 1, D_MODEL), lambda i, pos: (layer * SUBLANES + brow(i), 0, 5)),
                      vec, vec],
            out_specs=pl.BlockSpec((ROW_TM, D_MODEL), lambda i, pos: (i, 0)),
            scratch_shapes=[pltpu.VMEM((2, 2 * ROW_TM * GATHER_PITCH, LANES), F32),
                            pltpu.SemaphoreType.DMA((2,))]),
        out_shape=jax.ShapeDtypeStruct((t_rows, D_MODEL), F32),
        compiler_params=_cparams(("arbitrary",)),
    )(pos, x1, ys, route, mod, ln_g, ln_b)


MOE_TILES = -(-(2 * T_ALL + N_EXPERTS * (EXPERT_TM - 1)) // EXPERT_TM)


def _moe(x1, mod, layer, router_w, w_gu, w_down, ln_g, ln_b, t_rows, slot_buf):
    h, route, cnt = _router(x1, mod, layer, *router_w, t_rows)
    n_tiles = MOE_TILES
    if slot_buf is None:
        slot_buf = jnp.zeros((n_tiles * EXPERT_TM * ROW_SLABS, LANES), F32)
    counts = cnt[0, :N_EXPERTS].astype(jnp.int32)
    tiles_per = (counts + EXPERT_TM - 1) // EXPERT_TM
    tile_end = jnp.cumsum(tiles_per)
    tile_start = tile_end - tiles_per
    n_used = tile_end[-1:]
    route_t = route.T
    e = route_t[RT_E1:RT_E2 + 1].astype(jnp.int32)
    rank = route_t[RT_RANK1:RT_RANK2 + 1].astype(jnp.int32)
    ids = jnp.arange(N_EXPERTS, dtype=jnp.int32)

    def lookup(table, idx):
        hit = idx[None] == ids.reshape((N_EXPERTS,) + (1,) * idx.ndim)
        return jnp.sum(jnp.where(hit, table.reshape((N_EXPERTS,) + (1,) * idx.ndim), 0), axis=0)

    pos = (lookup(tile_start, e) * EXPERT_TM + rank).reshape(-1)
    ti = jnp.minimum(jnp.arange(n_tiles, dtype=jnp.int32), n_used[0] - 1)
    tile_expert = jnp.sum((ti[:, None] >= tile_end[None, :]).astype(jnp.int32), axis=1)
    owns = tiles_per > 0
    later = jnp.where(owns[None, :] & (ids[None, :] > ids[:, None]), ids[None, :], N_EXPERTS)
    nxt_e = jnp.min(later, axis=1)
    nxt_e = jnp.where(nxt_e == N_EXPERTS, -1, nxt_e)
    order_e = jnp.cumsum(owns.astype(jnp.int32)) - 1
    next_expert = lookup(nxt_e, tile_expert)
    tile_slot = lookup(order_e, tile_expert) % 2
    xs = _dispatch(pos, h, slot_buf)
    ys = _experts(tile_expert, n_used, next_expert, tile_slot, xs, w_gu, w_down, layer)
    return _final(pos, x1, ys, route, mod, layer, ln_g, ln_b, t_rows), ys


def _prep_w_in(w):
    offs = np.concatenate([[0], np.cumsum(ALL_SPLITS)])
    piece = lambda i: w[:, int(offs[i]):int(offs[i + 1])]
    w_main = jnp.concatenate([piece(i) for i in _PIECE_ORDER], axis=1).astype(BF16)
    w_lr = jnp.pad(piece(_LR_PIECE), ((0, 0), (0, LANES - 2 * GLA_RANK))).astype(BF16)
    return w_main, w_lr


def kernel(x, c, ctx, c_ctx, w_mod, b_mod, w_in, na_rpb, gla_w_gate, gla_b_gate, gla_norm, diff_lambda, diff_norm,
           w_branch, w_out, ln1_g, ln1_b, ln2_g, ln2_b, w_group, b_group, w_router, b_router, w_gu, w_down):
    x_lat, x_ctx = x.reshape(T_LAT, D_MODEL), ctx.reshape(T_CTX, D_MODEL)
    c8 =jnp.concatenate([c, c_ctx[None], jnp.zeros((SUBLANES - BATCH - 1, D_MODEL), F32)], axis=0)
    mod = _modulation(c8, w_mod, b_mod).reshape(DEPTH * SUBLANES, 1, 6 * D_MODEL)
    rope = _rope_tables()
    w_gu_flat = w_gu.reshape(DEPTH * N_EXPERTS, D_MODEL, 2 * EXPERT_DIM)
    w_down_flat = w_down.reshape(DEPTH * N_EXPERTS, EXPERT_DIM, D_MODEL)
    s_zero = jnp.zeros((BATCH, GLA_HEADS // 2, 4, GLA_VAL_DIM, LANES), F32)
    slot_buf = None

    for l in range(DEPTH):
        last = l == DEPTH - 1
        t_rows = T_LAT if last else T_ALL
        lam_init = 0.8 - 0.6 * math.exp(-0.3 * l)
        lq1, lk1, lq2, lk2 = diff_lambda[l].astype(F32)
        lam = jnp.exp(jnp.sum(lq1 * lk1)) - jnp.exp(jnp.sum(lq2 * lk2)) + lam_init
        lam_row = jnp.full((1, LANES), lam, F32)

        w_main, w_lr = _prep_w_in(w_in[l])
        p, lr = _projection(x_lat, x_ctx, mod, l, w_main, w_lr, skip_ctx_gates=last)

        gla_w = _gla_gate_weights(gla_w_gate[l], gla_b_gate[l])
        gla_gain = gla_norm[l].reshape(1, GLA_VAL_DIM)
        diff_gain = diff_norm[l].reshape(1, DIFF_V_DIM)
        y_gla_ctx, states = _gla(p, lr, *gla_w, gla_gain, s_zero, ctx=True)
        y_gla, _ = _gla(p, lr, *gla_w, gla_gain, states, ctx=False)
        ys_lat = (_na_lat(p, _na_bias_tiles(na_rpb[l])), y_gla, _fnet(p, ctx=False),
                  _diff_lat(p, rope, lam_row, diff_gain, lam_init))
        ys_ctx = None
        if not last:
            ys_ctx = (_na_ctx(p), y_gla_ctx, _fnet(p, ctx=True), _diff_ctx(p, lam_row, diff_gain, lam_init))

        m = _merge(p, ys_lat, ys_ctx, w_branch[l].astype(BF16))
        x1 = _outproj(m, x_lat, x_ctx, mod, l, w_out[l].astype(BF16), ln1_g[l].reshape(1, -1),
                      ln1_b[l].reshape(1, -1), t_rows)
        router_w = _router_weights(w_group[l], b_group[l], w_router[l], b_router[l])
        x_lat, slot_buf = _moe(x1, mod, l, router_w, w_gu_flat, w_down_flat, ln2_g[l].reshape(1, -1),
                               ln2_b[l].reshape(1, -1), t_rows, slot_buf)
        x_ctx = None
    return x_lat.reshape(BATCH, SEQ, D_MODEL)
```
